```python
import math
import jax, jax.numpy as jnp
from jax import lax
import numpy as np

D_MODEL = 1024
BATCH = 8
SEQ = 4096
DEPTH = 2

CHUNK = 64
N_MIXERS = 4
MIX_WIDTH = D_MODEL
GROUP_WIDTH = MIX_WIDTH // N_MIXERS
N_HEADS = 4
HEAD_DIM = GROUP_WIDTH // N_HEADS
CONV_WIDTH = 4
GDN_DK = HEAD_DIM
GDN_DV = HEAD_DIM
DIFF_DH = HEAD_DIM // 2
DIFF_DV = HEAD_DIM
Q_BLOCK = 128
REL_BUCKETS = 32
REL_MAX_DIST = 128
RG_C = 8.0
RG_BLOCKS = N_HEADS
RG_BLOCK_W = GROUP_WIDTH // RG_BLOCKS
GLA_DK = HEAD_DIM // 2
GLA_DV = HEAD_DIM
GLA_RANK = 16
GLA_TAU = 16.0
D_FF = 2816
N_EXPERTS = 8
TOP_K = 2
D_FF_EXPERT = 3584
MOE_BLOCK = 128
N_DENSE = (DEPTH + 1) // 2
N_MOE = DEPTH // 2
EPS = 1e-6

SPLIT_SIZES = (
    N_HEADS * GDN_DK, N_HEADS * GDN_DK, N_HEADS * GDN_DV, N_HEADS * GDN_DV, N_HEADS, N_HEADS,
    N_HEADS * 2 * DIFF_DH, N_HEADS * 2 * DIFF_DH, N_HEADS * DIFF_DV,
    GROUP_WIDTH, GROUP_WIDTH,
    N_HEADS * GLA_DK, N_HEADS * GLA_DK, N_HEADS * GLA_DV, N_HEADS * GLA_DV, GLA_RANK,
)
D_IN = sum(SPLIT_SIZES)
SPLIT_POINTS = tuple(int(v) for v in np.cumsum(SPLIT_SIZES)[:-1])
GDN_CONV_CH = N_HEADS * (2 * GDN_DK + GDN_DV)

kernel_name = 'hybrid_parallel_heads_chunk_causal_encoder'

f32 = jnp.float32


def rmsnorm(x, g):
    xf = x.astype(f32)
    y = xf * lax.rsqrt(jnp.mean(xf * xf, axis=-1, keepdims=True) + EPS)
    return (y * g.astype(f32)).astype(x.dtype)


def l2norm(x):
    return x * lax.rsqrt(jnp.sum(x * x, axis=-1, keepdims=True) + EPS)


def causal_dwconv(x, w):
    S = x.shape[1]
    width = w.shape[0]
    xp = jnp.pad(x, ((0, 0), (width - 1, 0), (0, 0)))
    y = xp[:, 0:S] * w[0]
    for j in range(1, width):
        y = y + xp[:, j:j + S] * w[j]
    return y


def to_chunks(t):
    B, S, H, d = t.shape
    return t.reshape(B, S // CHUNK, CHUNK, H, d).transpose(0, 3, 1, 2, 4)


def from_chunks(t):
    B, H, N, C, d = t.shape
    return t.transpose(0, 2, 3, 1, 4).reshape(B, N * C, H, d)


def gated_deltanet(q, k, v, gate, beta_logit, alpha_in, conv_w, A_log, dt_bias, norm_g):
    B, S, _ = q.shape
    qkv = jax.nn.silu(causal_dwconv(jnp.concatenate([q, k, v], axis=-1), conv_w.astype(f32)))
    q, k, v = jnp.split(qkv, [N_HEADS * GDN_DK, 2 * N_HEADS * GDN_DK], axis=-1)
    q = l2norm(q.reshape(B, S, N_HEADS, GDN_DK)) * GDN_DK ** -0.5
    k = l2norm(k.reshape(B, S, N_HEADS, GDN_DK))
    v = v.reshape(B, S, N_HEADS, GDN_DV)
    beta = jax.nn.sigmoid(beta_logit)[..., None]
    g = (-jnp.exp(A_log.astype(f32)) * jax.nn.softplus(alpha_in + dt_bias.astype(f32)))[..., None]
    qc, kc, vc, bc = to_chunks(q), to_chunks(k), to_chunks(v), to_chunks(beta)
    G = jnp.cumsum(to_chunks(g)[..., 0], axis=-1)
    idx = jnp.arange(CHUNK)
    causal = idx[:, None] >= idx[None, :]
    strict = idx[:, None] > idx[None, :]
    gamma = jnp.exp(jnp.where(causal, G[..., :, None] - G[..., None, :], -jnp.inf))
    kb = kc * bc
    a_kk = jnp.where(strict, jnp.einsum('bhnid,bhnjd->bhnij', kb, kc) * gamma, 0.0)
    rhs = jnp.concatenate([vc * bc, kb * jnp.exp(G)[..., None]], axis=-1)
    sol = lax.linalg.triangular_solve(a_kk, rhs, left_side=True, lower=True, unit_diagonal=True)
    u, w = sol[..., :GDN_DV], sol[..., GDN_DV:]
    a_qk = jnp.einsum('bhnid,bhnjd->bhnij', qc, kc) * gamma
    qg = qc * jnp.exp(G)[..., None]
    g_last = G[..., -1]
    kd = kc * jnp.exp(g_last[..., None] - G)[..., None]
    dl = jnp.exp(g_last)

    def step(state, xs):
        u_, w_, qg_, aqk_, kd_, dl_ = xs
        v_new = u_ - jnp.einsum('bhck,bhkv->bhcv', w_, state)
        o = jnp.einsum('bhck,bhkv->bhcv', qg_, state) + jnp.einsum('bhij,bhjv->bhiv', aqk_, v_new)
        state = dl_[..., None, None] * state + jnp.einsum('bhck,bhcv->bhkv', kd_, v_new)
        return state, o

    xs = tuple(jnp.moveaxis(t, 2, 0) for t in (u, w, qg, a_qk, kd, dl))
    _, o = lax.scan(step, jnp.zeros((B, N_HEADS, GDN_DK, GDN_DV), f32), xs)
    o = from_chunks(jnp.moveaxis(o, 0, 2))
    o = rmsnorm(o, norm_g) * jax.nn.silu(gate.reshape(B, S, N_HEADS, GDN_DV))
    return o.reshape(B, S, N_HEADS * GDN_DV)


def t5_bucket(rel):
    nb = REL_BUCKETS // 2
    bucket = jnp.where(rel > 0, nb, 0)
    n = jnp.abs(rel)
    max_exact = nb // 2
    large = max_exact + (jnp.log(jnp.maximum(n, 1).astype(f32) / max_exact)
                         / math.log(REL_MAX_DIST / max_exact) * (nb - max_exact)).astype(jnp.int32)
    large = jnp.minimum(large, nb - 1)
    return bucket + jnp.where(n < max_exact, n, large)


def diff_attention(q, k, v, lam_vecs, lam_init, rel_bias, norm_g):
    B, S, _ = q.shape
    q = q.reshape(B, S, N_HEADS, 2, DIFF_DH) * DIFF_DH ** -0.5
    k = k.reshape(B, S, N_HEADS, 2, DIFF_DH)
    v = v.reshape(B, S, N_HEADS, DIFF_DV)
    lv = lam_vecs.astype(f32)
    lam = jnp.exp(jnp.sum(lv[0] * lv[1])) - jnp.exp(jnp.sum(lv[2] * lv[3])) + lam_init
    n_qb = S // Q_BLOCK
    qb = q.reshape(B, n_qb, Q_BLOCK, N_HEADS, 2, DIFF_DH).transpose(1, 0, 3, 4, 2, 5)
    kt = k.transpose(0, 2, 3, 1, 4)
    vt = v.transpose(0, 2, 1, 3)
    kpos = jnp.arange(S)
    table = rel_bias.astype(f32)

    def block(args):
        qblk, bi = args
        qpos = bi * Q_BLOCK + jnp.arange(Q_BLOCK)
        bias = table[t5_bucket(kpos[None, :] - qpos[:, None])].transpose(2, 0, 1)[:, None]
        mask = (kpos[None, :] // CHUNK) <= (qpos[:, None] // CHUNK)
        s = jnp.einsum('bhcqd,bhckd->bhcqk', qblk, kt) + bias
        p = jax.nn.softmax(jnp.where(mask, s, -jnp.inf), axis=-1)
        attn = p[:, :, 0] - lam * p[:, :, 1]
        return jnp.einsum('bhqk,bhkd->bhqd', attn, vt)

    o = lax.map(block, (qb, jnp.arange(n_qb)))
    o = o.transpose(1, 0, 3, 2, 4).reshape(B, S, N_HEADS, DIFF_DV)
    o = rmsnorm(o, norm_g) * (1.0 - lam_init)
    return o.reshape(B, S, N_HEADS * DIFF_DV)


def rglru_block(xb, gb, conv_w, conv_b, w_a, b_a, w_x, b_x, a_param):
    B, S, W = xb.shape
    xc = causal_dwconv(xb, conv_w.astype(f32)) + conv_b.astype(f32)
    xh = xc.reshape(B, S, RG_BLOCKS, RG_BLOCK_W)
    gate_a = jax.nn.sigmoid(jnp.einsum('bsnc,ncd->bsnd', xh, w_a.astype(f32)).reshape(B, S, W) + b_a)
    gate_x = jax.nn.sigmoid(jnp.einsum('bsnc,ncd->bsnd', xh, w_x.astype(f32)).reshape(B, S, W) + b_x)
    log_a = -RG_C * gate_a * jax.nn.softplus(a_param.astype(f32))
    a = jnp.exp(log_a)
    u = xc * gate_x * jnp.sqrt(-jnp.expm1(2.0 * log_a))

    def combine(lhs, rhs):
        return (lhs[0] * rhs[0], rhs[0] * lhs[1] + rhs[1])

    _, h = lax.associative_scan(combine, (a, u), axis=1)
    return h * jax.nn.gelu(gb)


def gla(q, k, v, r, lr, w_lr, b_lr, norm_g):
    B, S, _ = q.shape
    q = q.reshape(B, S, N_HEADS, GLA_DK) * GLA_DK ** -0.5
    k = k.reshape(B, S, N_HEADS, GLA_DK)
    v = v.reshape(B, S, N_HEADS, GLA_DV)
    log_alpha = jax.nn.log_sigmoid(lr @ w_lr.astype(f32) + b_lr.astype(f32)) / GLA_TAU
    log_alpha = log_alpha.reshape(B, S, N_HEADS, GLA_DK)
    qc, kc, vc = to_chunks(q), to_chunks(k), to_chunks(v)
    G = jnp.cumsum(to_chunks(log_alpha), axis=3)
    ref = G[:, :, :, CHUNK // 2:CHUNK // 2 + 1]
    idx = jnp.arange(CHUNK)
    causal = idx[:, None] >= idx[None, :]
    a_in = jnp.einsum('bhnid,bhnjd->bhnij', qc * jnp.exp(G - ref), kc * jnp.exp(ref - G))
    o_intra = jnp.einsum('bhnij,bhnjv->bhniv', jnp.where(causal, a_in, 0.0), vc)
    qg = qc * jnp.exp(G)
    g_last = G[:, :, :, -1]
    kd = kc * jnp.exp(g_last[:, :, :, None, :] - G)
    dl = jnp.exp(g_last)

    def step(state, xs):
        qg_, kd_, v_, dl_ = xs
        o = jnp.einsum('bhck,bhkv->bhcv', qg_, state)
        state = dl_[..., :, None] * state + jnp.einsum('bhck,bhcv->bhkv', kd_, v_)
        return state, o

    xs = tuple(jnp.moveaxis(t, 2, 0) for t in (qg, kd, vc, dl))
    _, o_inter = lax.scan(step, jnp.zeros((B, N_HEADS, GLA_DK, GLA_DV), f32), xs)
    o = from_chunks(o_intra + jnp.moveaxis(o_inter, 0, 2))
    o = rmsnorm(o, norm_g) * jax.nn.silu(r.reshape(B, S, N_HEADS, GLA_DV))
    return o.reshape(B, S, N_HEADS * GLA_DV)


def swiglu(h, w_gate, w_up, w_down):
    return (jax.nn.silu(h @ w_gate) * (h @ w_up)) @ w_down


def moe_swiglu(h, w_router, w_gate, w_up, w_down):
    B, S, D = h.shape
    T = B * S
    hf = h.reshape(T, D)
    logits = (hf @ w_router).astype(f32)
    top_logit, top_e = lax.top_k(logits, TOP_K)
    top_w = jax.nn.softmax(top_logit, axis=-1)
    n_assign = T * TOP_K
    flat_e = top_e.reshape(n_assign)
    order = jnp.argsort(flat_e)
    sorted_e = flat_e[order]
    counts = jnp.bincount(flat_e, length=N_EXPERTS)
    padded = (counts + MOE_BLOCK - 1) // MOE_BLOCK * MOE_BLOCK
    pad_end = jnp.cumsum(padded)
    pad_start = pad_end - padded
    grp_start = jnp.cumsum(counts) - counts
    dest = pad_start[sorted_e] + jnp.arange(n_assign) - grp_start[sorted_e]
    n_blocks = -(-n_assign // MOE_BLOCK) + N_EXPERTS
    n_slots = n_blocks * MOE_BLOCK
    slot_tok = jnp.zeros((n_slots,), jnp.int32).at[dest].set((order // TOP_K).astype(jnp.int32))
    slot_w = jnp.zeros((n_slots,), f32).at[dest].set(top_w.reshape(n_assign)[order])
    block_e = jnp.minimum(jnp.searchsorted(pad_end, jnp.arange(n_blocks) * MOE_BLOCK, side='right'),
                          N_EXPERTS - 1)
    xs = hf[slot_tok].reshape(n_blocks, MOE_BLOCK, D)

    def expert_block(args):
        xb, e = args
        return (jax.nn.silu(xb @ w_gate[e]) * (xb @ w_up[e])) @ w_down[e]

    ys = lax.map(expert_block, (xs, block_e)).reshape(n_slots, D)
    y = jnp.zeros((T, D), f32).at[slot_tok].add(ys.astype(f32) * slot_w[:, None])
    return y.astype(h.dtype).reshape(B, S, D)


def setup_inputs(seed: int = 0) -> dict:
    key = jax.random.key(seed)
    k = jax.random.split(key, 30)
    L = DEPTH

    def nrm(i, shape, scale):
        return jax.random.normal(k[i], shape, f32) * scale

    def gain(i, shape):
        return 1.0 + nrm(i, shape, 0.02)

    A_log = jnp.log(jax.random.uniform(k[4], (L, N_HEADS), f32, 1.0, 16.0))
    dt = jnp.exp(jax.random.uniform(k[5], (L, N_HEADS), f32, math.log(1e-3), math.log(1e-1)))
    dt_bias = dt + jnp.log(-jnp.expm1(-dt))
    sp = -0.5 * jnp.log(jax.random.uniform(k[16], (L, GROUP_WIDTH), f32, 0.81, 0.998))
    a_param = jnp.log(jnp.expm1(sp))
    return {
        'x': nrm(0, (BATCH, SEQ, D_MODEL), 1.0),
        'norm_mix': gain(1, (L, D_MODEL)),
        'w_in': nrm(2, (L, D_MODEL, D_IN), D_MODEL ** -0.5),
        'a_conv': nrm(3, (L, CONV_WIDTH, GDN_CONV_CH), CONV_WIDTH ** -0.5),
        'a_A_log': A_log,
        'a_dt_bias': dt_bias,
        'a_norm': gain(6, (L, GDN_DV)),
        'b_lambda': nrm(7, (L, 4, DIFF_DH), 0.1),
        'b_norm': gain(8, (L, DIFF_DV)),
        'rel_bias': nrm(9, (REL_BUCKETS, N_HEADS), 0.5),
        'c_conv_w': nrm(10, (L, CONV_WIDTH, GROUP_WIDTH), CONV_WIDTH ** -0.5),
        'c_conv_b': nrm(11, (L, GROUP_WIDTH), 0.02),
        'c_w_a': nrm(12, (L, RG_BLOCKS, RG_BLOCK_W, RG_BLOCK_W), RG_BLOCK_W ** -0.5),
        'c_b_a': nrm(13, (L, GROUP_WIDTH), 0.1),
        'c_w_x': nrm(14, (L, RG_BLOCKS, RG_BLOCK_W, RG_BLOCK_W), RG_BLOCK_W ** -0.5),
        'c_b_x': nrm(15, (L, GROUP_WIDTH), 0.1),
        'c_a_param': a_param,
        'd_w_lr': nrm(17, (L, GLA_RANK, N_HEADS * GLA_DK), GLA_RANK ** -0.5),
        'd_b_lr': nrm(18, (L, N_HEADS * GLA_DK), 0.1),
        'd_norm': gain(19, (L, GLA_DV)),
        'w_out': nrm(20, (L, MIX_WIDTH, D_MODEL), MIX_WIDTH ** -0.5),
        'norm_ffn': gain(21, (L, D_MODEL)),
        'ffn_w_gate': nrm(22, (N_DENSE, D_MODEL, D_FF), D_MODEL ** -0.5),
        'ffn_w_up': nrm(23, (N_DENSE, D_MODEL, D_FF), D_MODEL ** -0.5),
        'ffn_w_down': nrm(24, (N_DENSE, D_FF, D_MODEL), D_FF ** -0.5),
        'moe_router': nrm(25, (N_MOE, D_MODEL, N_EXPERTS), D_MODEL ** -0.5),
        'moe_w_gate': nrm(26, (N_MOE, N_EXPERTS, D_MODEL, D_FF_EXPERT), D_MODEL ** -0.5),
        'moe_w_up': nrm(27, (N_MOE, N_EXPERTS, D_MODEL, D_FF_EXPERT), D_MODEL ** -0.5),
        'moe_w_down': nrm(28, (N_MOE, N_EXPERTS, D_FF_EXPERT, D_MODEL), D_FF_EXPERT ** -0.5),
        'norm_final': gain(29, (D_MODEL,)),
    }


def reference(x, norm_mix, w_in, a_conv, a_A_log, a_dt_bias, a_norm, b_lambda, b_norm, rel_bias,
              c_conv_w, c_conv_b, c_w_a, c_b_a, c_w_x, c_b_x, c_a_param, d_w_lr, d_b_lr, d_norm,
              w_out, norm_ffn, ffn_w_gate, ffn_w_up, ffn_w_down, moe_router, moe_w_gate, moe_w_up,
              moe_w_down, norm_final):
    for l in range(DEPTH):
        h = rmsnorm(x, norm_mix[l])
        proj = (h @ w_in[l]).astype(f32)
        (qa, ka, va, ga, ba, aa, qb, kb, vb, xc, gc, qd, kd, vd, rd, lrd) = jnp.split(proj, SPLIT_POINTS, axis=-1)
        o_a = gated_deltanet(qa, ka, va, ga, ba, aa, a_conv[l], a_A_log[l], a_dt_bias[l], a_norm[l])
        lam_init = 0.8 - 0.6 * math.exp(-0.3 * l)
        o_b = diff_attention(qb, kb, vb, b_lambda[l], lam_init, rel_bias, b_norm[l])
        o_c = rglru_block(xc, gc, c_conv_w[l], c_conv_b[l], c_w_a[l], c_b_a[l], c_w_x[l], c_b_x[l], c_a_param[l])
        o_d = gla(qd, kd, vd, rd, lrd, d_w_lr[l], d_b_lr[l], d_norm[l])
        mix = jnp.concatenate([o_a, o_b, o_c, o_d], axis=-1).astype(x.dtype)
        x = x + mix @ w_out[l]
        h = rmsnorm(x, norm_ffn[l])
        if l % 2 == 0:
            x = x + swiglu(h, ffn_w_gate[l // 2], ffn_w_up[l // 2], ffn_w_down[l // 2])
        else:
            x = x + moe_swiglu(h, moe_router[l // 2], moe_w_gate[l // 2], moe_w_up[l // 2], moe_w_down[l // 2])
    return rmsnorm(x, norm_final)
```

```python
import functools
import math

import jax
import jax.numpy as jnp
import numpy as np
from jax import lax
from jax.experimental import pallas as pl
from jax.experimental.pallas import tpu as pltpu

D_MODEL = 1024
CHUNK = 64
N_HEADS = 4
HEAD_DIM = 64
GROUP_WIDTH = 256
CONV_WIDTH = 4
DIFF_DH = 32
Q_BLOCK = 128
REL_BUCKETS = 32
REL_MAX_DIST = 128
RG_C = 8.0
GLA_DK = 32
GLA_RANK = 16
GLA_TAU = 16.0
D_FF = 2816
N_EXPERTS = 8
TOP_K = 2
D_FF_EXPERT = 3584
EPS = 1e-6

LANE = 128
SUBLANE = 8
VMEM_LIMIT = 56 * 1024 * 1024

F32 = jnp.float32
BF16 = jnp.bfloat16
HI = lax.Precision.HIGHEST
NEG = -1e30

A_W = 1152
B_W = 768
C_W = 512
D_W = 896

PROJ_TM = 512
MIX_TS = 256
ATT_T = 128
FFN_TM = 512
FFN_TF = 1408
MOE_TM = 1024
MOE_TF = 512


def _cparams(sem):
    return pltpu.CompilerParams(dimension_semantics=sem, vmem_limit_bytes=VMEM_LIMIT)


def _dot(a, b, precision=None):
    return jnp.dot(a, b, preferred_element_type=F32, precision=precision)


def _dot_nt(a, b, precision=None):
    return lax.dot_general(a, b, (((1,), (1,)), ((), ())), preferred_element_type=F32,
                           precision=precision)


def _dot_tn(a, b, precision=None):
    return lax.dot_general(a, b, (((0,), (0,)), ((), ())), preferred_element_type=F32,
                           precision=precision)


def _softplus(x):
    return jnp.maximum(x, 0.0) + jnp.log1p(jnp.exp(-jnp.abs(x)))


def _rms(x, g):
    return x * lax.rsqrt(jnp.mean(x * x, axis=-1, keepdims=True) + EPS) * g


def _causal_conv(x, tail, w):
    row = lax.broadcasted_iota(jnp.int32, (SUBLANE, x.shape[1]), 0)
    y = x * w[CONV_WIDTH - 1:CONV_WIDTH, :]
    for d in range(1, CONV_WIDTH):
        rolled = pltpu.roll(x, d, 0)
        first = jnp.where(row < d, pltpu.roll(tail, d, 0), rolled[:SUBLANE])
        shifted = jnp.concatenate([first, rolled[SUBLANE:]], axis=0)
        y = y + shifted * w[CONV_WIDTH - 1 - d:CONV_WIDTH - d, :]
    return y


def _head_mask(width, per_head):
    lane = lax.broadcasted_iota(jnp.int32, (1, width), 1)
    return [(lane // per_head) == h for h in range(N_HEADS)]


def _stack_heads(x, masks):
    return jnp.concatenate([jnp.where(m, x, 0.0) for m in masks], axis=0)


def _unstack_heads(r, masks, c):
    out = jnp.where(masks[0], r[0:c], 0.0)
    for h in range(1, N_HEADS):
        out = out + jnp.where(masks[h], r[h * c:(h + 1) * c], 0.0)
    return out


def _inproj_kernel(x_ref, g_ref, wa_ref, wb_ref, wc_ref, wd_ref, oa_ref, ob_ref, oc_ref, od_ref):
    h = _rms(x_ref[...], g_ref[...]).astype(BF16)
    oa_ref[...] = _dot(h, wa_ref[...])
    ob_ref[...] = _dot(h, wb_ref[...])
    oc_ref[...] = _dot(h, wc_ref[...])
    od_ref[...] = _dot(h, wd_ref[...])


def _inproj(x, g, wa, wb, wc, wd):
    t = x.shape[0]
    tm = min(PROJ_TM, t)
    row = lambda w: pl.BlockSpec((tm, w), lambda i: (i, 0))
    full = lambda a: pl.BlockSpec(a.shape, lambda i: (0, 0))
    return pl.pallas_call(
        _inproj_kernel,
        grid=(t // tm,),
        in_specs=[row(D_MODEL), full(g), full(wa), full(wb), full(wc), full(wd)],
        out_specs=[row(A_W), row(B_W), row(C_W), row(D_W)],
        out_shape=[jax.ShapeDtypeStruct((t, w), F32) for w in (A_W, B_W, C_W, D_W)],
        compiler_params=_cparams(("parallel",)),
    )(x, g, wa, wb, wc, wd)


def _gdn_kernel(a_ref, convw_ref, alog_ref, dtb_ref, gn_ref, bd_ref, lt_ref, eb_ref, ea_ref,
                sel_ref, o_ref, tail_ref, state_ref, *, ts):
    @pl.when(pl.program_id(1) == 0)
    def _init():
        tail_ref[...] = jnp.zeros_like(tail_ref)
        state_ref[...] = jnp.zeros_like(state_ref)

    c = CHUNK
    xin = a_ref[:, 0:768]
    y = _causal_conv(xin, tail_ref[...], convw_ref[...])
    tail_ref[...] = xin[ts - SUBLANE:ts, :]
    y = y * jax.nn.sigmoid(y)
    bd = bd_ref[...]
    q = y[:, 0:256]
    k = y[:, 256:512]
    v = y[:, 512:768]
    q = q * lax.rsqrt(_dot(q * q, bd, HI) + EPS) * (HEAD_DIM ** -0.5)
    k = k * lax.rsqrt(_dot(k * k, bd, HI) + EPS)
    ba = a_ref[:, 1024:1152]
    beta = jax.nn.sigmoid(_dot(ba, eb_ref[...], HI))
    g = -jnp.exp(alog_ref[...]) * _softplus(_dot(ba, ea_ref[...], HI) + dtb_ref[...])
    gate = a_ref[:, 768:1024]

    masks = _head_mask(GROUP_WIDTH, HEAD_DIM)
    ri = lax.broadcasted_iota(jnp.int32, (N_HEADS * c, c), 0) % c
    cj = lax.broadcasted_iota(jnp.int32, (N_HEADS * c, c), 1)
    causal = ri >= cj
    strict = ri > cj
    eye3 = (lax.broadcasted_iota(jnp.int32, (N_HEADS, c, c), 1)
            == lax.broadcasted_iota(jnp.int32, (N_HEADS, c, c), 2)).astype(F32)
    bdm = bd > 0.5
    lt = lt_ref[...]
    sel = sel_ref[...]
    gn = gn_ref[...]

    def bmm(x3, y3):
        return jnp.einsum('hij,hjk->hik', x3, y3, preferred_element_type=F32, precision=HI)

    state = state_ref[...]
    for ci in range(ts // c):
        sl = slice(ci * c, (ci + 1) * c)
        qc, kc, vc, bc = q[sl], k[sl], v[sl], beta[sl]
        gc = _dot(lt, g[sl], HI)
        eg = jnp.exp(gc)
        g_last = gc[c - 1:c, :]
        kb = kc * bc
        gcol = jnp.concatenate([gc[:, h * HEAD_DIM:(h + 1) * HEAD_DIM] for h in range(N_HEADS)], axis=0)
        grow = _dot_nt(sel, gc, HI)
        gamma = jnp.exp(jnp.where(causal, gcol - grow, NEG))
        a_kk = jnp.where(strict, _dot_nt(_stack_heads(kb, masks), kc, HI) * gamma, 0.0)
        a_qk = _dot_nt(_stack_heads(qc, masks), kc, HI) * gamma
        p = -a_kk.reshape(N_HEADS, c, c)
        inv = eye3 + p
        for _ in range(5):
            p = bmm(p, p)
            inv = inv + bmm(inv, p)
        rhs = jnp.concatenate([vc * bc, kb * eg], axis=1)
        sol = _dot(inv.reshape(N_HEADS * c, c), rhs, HI)
        u = _unstack_heads(sol[:, 0:256], masks, c)
        w = _unstack_heads(sol[:, 256:512], masks, c)
        ws_qs = _dot(jnp.concatenate([w, qc * eg], axis=0), state, HI)
        v_new = u - ws_qs[0:c]
        o = ws_qs[c:2 * c] + _unstack_heads(_dot(a_qk, v_new, HI), masks, c)
        kd = kc * jnp.exp(g_last - gc)
        state = state * jnp.exp(g_last) + jnp.where(bdm, _dot_tn(kd, v_new, HI), 0.0)
        o = o * lax.rsqrt(_dot(o * o, bd, HI) * (1.0 / HEAD_DIM) + EPS) * gn
        gt = gate[sl]
        o_ref[sl, :] = (o * (gt * jax.nn.sigmoid(gt))).astype(o_ref.dtype)
    state_ref[...] = state


def _block_diag_ones(n, blk):
    i = np.arange(n)
    return jnp.asarray((i[:, None] // blk) == (i[None, :] // blk), F32)


def _gdn(pa, conv_w, a_log, dt_bias, norm_g, batch, seq):
    ts = min(MIX_TS, seq)
    nst = seq // ts
    bd = _block_diag_ones(GROUP_WIDTH, HEAD_DIM)
    lt = jnp.asarray(np.tril(np.ones((CHUNK, CHUNK), np.float32)))
    lane = np.arange(GROUP_WIDTH)
    eb = np.zeros((LANE, GROUP_WIDTH), np.float32)
    ea = np.zeros((LANE, GROUP_WIDTH), np.float32)
    sel = np.zeros((N_HEADS * CHUNK, GROUP_WIDTH), np.float32)
    for h in range(N_HEADS):
        eb[h, lane // HEAD_DIM == h] = 1.0
        ea[N_HEADS + h, lane // HEAD_DIM == h] = 1.0
        sel[h * CHUNK:(h + 1) * CHUNK, h * HEAD_DIM] = 1.0
    rep = lambda p: jnp.repeat(p.astype(F32), HEAD_DIM)[None, :]
    consts = [conv_w.astype(F32), rep(a_log), rep(dt_bias),
              jnp.tile(norm_g.astype(F32), N_HEADS)[None, :], bd, lt,
              jnp.asarray(eb), jnp.asarray(ea), jnp.asarray(sel)]
    full = lambda a: pl.BlockSpec(a.shape, lambda b, s: (0, 0))
    return pl.pallas_call(
        functools.partial(_gdn_kernel, ts=ts),
        grid=(batch, nst),
        in_specs=[pl.BlockSpec((ts, A_W), lambda b, s: (b * nst + s, 0))] + [full(a) for a in consts],
        out_specs=pl.BlockSpec((ts, GROUP_WIDTH), lambda b, s: (b * nst + s, 0)),
        out_shape=jax.ShapeDtypeStruct((batch * seq, GROUP_WIDTH), BF16),
        scratch_shapes=[pltpu.VMEM((SUBLANE, 768), F32), pltpu.VMEM((GROUP_WIDTH, GROUP_WIDTH), F32)],
        compiler_params=_cparams(("parallel", "arbitrary")),
    )(pa, *consts)


def _attn_kernel(q_ref, k_ref, v_ref, bdiag_ref, bnear_ref, bfar_ref, lamv_ref, gn_ref, o_ref,
                 kb_ref, vt_ref, qt_ref, m_ref, l_ref, acc_ref, *, lam_init, seq):
    t = ATT_T
    qi = pl.program_id(1)
    nlan = 2 * N_HEADS * t

    @pl.when(qi == 0)
    def _stage_kv():
        def body(j, carry):
            rows = pl.ds(pl.multiple_of(j * t, t), t)
            kb_ref[rows, :] = k_ref[rows, :].astype(BF16)
            vt_ref[j] = v_ref[rows, :].T.astype(BF16)
            return carry
        lax.fori_loop(0, seq // t, body, 0)

    qt = (q_ref[...] * (DIFF_DH ** -0.5)).T
    feat = lax.broadcasted_iota(jnp.int32, (GROUP_WIDTH, t), 0) // DIFF_DH
    for idx in range(2 * N_HEADS):
        qt_ref[:, idx * t:(idx + 1) * t] = jnp.where(feat == idx, qt, 0.0).astype(BF16)
    m_ref[...] = jnp.full((1, nlan), NEG, F32)
    l_ref[...] = jnp.zeros((1, nlan), F32)
    acc_ref[...] = jnp.zeros((HEAD_DIM, nlan), F32)

    def step(kt, bias):
        rows = pl.ds(pl.multiple_of(kt * t, t), t)
        s = _dot(kb_ref[rows, :], qt_ref[...]) + bias
        m_prev = m_ref[...]
        m_new = jnp.maximum(m_prev, jnp.max(s, axis=0, keepdims=True))
        alpha = jnp.exp(m_prev - m_new)
        p = jnp.exp(s - m_new)
        l_ref[...] = alpha * l_ref[...] + jnp.sum(p, axis=0, keepdims=True)
        m_ref[...] = m_new
        pb = p.astype(BF16)
        for h in range(N_HEADS):
            cols = slice(2 * h * t, (2 * h + 2) * t)
            pv = _dot(vt_ref[kt, h * HEAD_DIM:(h + 1) * HEAD_DIM, :], pb[:, cols])
            acc_ref[:, cols] = alpha[:, cols] * acc_ref[:, cols] + pv

    def far_body(kt, carry):
        step(kt, bfar_ref[...])
        return carry
    lax.fori_loop(0, jnp.maximum(qi - 1, 0), far_body, 0)

    @pl.when(qi >= 1)
    def _near():
        step(qi - 1, bnear_ref[...])

    step(qi, bdiag_ref[...])

    lv = lamv_ref[...]
    lam = (jnp.exp(jnp.sum(lv[0:1] * lv[1:2], axis=1, keepdims=True))
           - jnp.exp(jnp.sum(lv[2:3] * lv[3:4], axis=1, keepdims=True)) + lam_init)
    inv_l = 1.0 / l_ref[...]
    outs = []
    for h in range(N_HEADS):
        c0 = slice(2 * h * t, (2 * h + 1) * t)
        c1 = slice((2 * h + 1) * t, (2 * h + 2) * t)
        oh = acc_ref[:, c0] * inv_l[:, c0] - lam * (acc_ref[:, c1] * inv_l[:, c1])
        oh = oh * lax.rsqrt(jnp.mean(oh * oh, axis=0, keepdims=True) + EPS)
        outs.append(oh)
    o = jnp.concatenate(outs, axis=0).T
    o_ref[...] = (o * gn_ref[...] * (1.0 - lam_init)).astype(o_ref.dtype)


def _t5_bucket(rel):
    nb = REL_BUCKETS // 2
    bucket = jnp.where(rel > 0, nb, 0)
    n = jnp.abs(rel)
    max_exact = nb // 2
    large = max_exact + (jnp.log(jnp.maximum(n, 1).astype(F32) / max_exact)
                         / math.log(REL_MAX_DIST / max_exact) * (nb - max_exact)).astype(jnp.int32)
    large = jnp.minimum(large, nb - 1)
    return bucket + jnp.where(n < max_exact, n, large)


def _attn_bias_tiles(rel_bias):
    t = ATT_T
    table = rel_bias.astype(F32)
    kk = jnp.arange(t)[:, None]
    qq = jnp.arange(t)[None, :]

    def expand(b):
        b = jnp.transpose(b, (0, 2, 1))
        b = jnp.broadcast_to(b[:, :, None, :], (t, N_HEADS, 2, t))
        return b.reshape(t, 2 * N_HEADS * t)

    diag = table[_t5_bucket(kk - qq)]
    diag = jnp.where(((kk // CHUNK) <= (qq // CHUNK))[:, :, None], diag, NEG)
    near = table[_t5_bucket(kk - qq - t)]
    far = table[_t5_bucket(jnp.full((1, 1), -(REL_MAX_DIST + 1), jnp.int32))]
    far = jnp.broadcast_to(far, (1, t, N_HEADS))
    return expand(diag), expand(near), expand(jnp.broadcast_to(far, (t, t, N_HEADS)))[0:1]


def _diff_attn(pb, lam_vecs, lam_init, bias_tiles, norm_g, batch, seq):
    t = ATT_T
    nq = seq // t
    bdiag, bnear, bfar = bias_tiles
    gn = jnp.tile(norm_g.astype(F32), N_HEADS)[None, :]
    lamv = lam_vecs.astype(F32)
    nlan = 2 * N_HEADS * t
    full = lambda a: pl.BlockSpec(a.shape, lambda b, i: (0, 0))
    return pl.pallas_call(
        functools.partial(_attn_kernel, lam_init=lam_init, seq=seq),
        grid=(batch, nq),
        in_specs=[pl.BlockSpec((t, GROUP_WIDTH), lambda b, i: (b * nq + i, 0)),
                  pl.BlockSpec((seq, GROUP_WIDTH), lambda b, i: (b, 1)),
                  pl.BlockSpec((seq, GROUP_WIDTH), lambda b, i: (b, 2)),
                  full(bdiag), full(bnear), full(bfar), full(lamv), full(gn)],
        out_specs=pl.BlockSpec((t, GROUP_WIDTH), lambda b, i: (b * nq + i, 0)),
        out_shape=jax.ShapeDtypeStruct((batch * seq, GROUP_WIDTH), BF16),
        scratch_shapes=[pltpu.VMEM((seq, GROUP_WIDTH), BF16), pltpu.VMEM((seq // t, GROUP_WIDTH, t), BF16),
                        pltpu.VMEM((GROUP_WIDTH, nlan), BF16), pltpu.VMEM((1, nlan), F32),
                        pltpu.VMEM((1, nlan), F32), pltpu.VMEM((HEAD_DIM, nlan), F32)],
        compiler_params=_cparams(("parallel", "arbitrary")),
    )(pb, pb, pb, bdiag, bnear, bfar, lamv, gn)


def _rglru_kernel(c_ref, convw_ref, convb_ref, wa_ref, ba_ref, wx_ref, bx_ref, ap_ref, o_ref,
                  tail_ref, h_ref, *, ts):
    @pl.when(pl.program_id(1) == 0)
    def _init():
        tail_ref[...] = jnp.zeros_like(tail_ref)
        h_ref[...] = jnp.zeros_like(h_ref)

    xb = c_ref[:, 0:256]
    gb = c_ref[:, 256:512]
    xc = _causal_conv(xb, tail_ref[...], convw_ref[...]) + convb_ref[...]
    tail_ref[...] = xb[ts - SUBLANE:ts, :]
    xcb = xc.astype(BF16)
    gate_a = jax.nn.sigmoid(_dot(xcb, wa_ref[...]) + ba_ref[...])
    gate_x = jax.nn.sigmoid(_dot(xcb, wx_ref[...]) + bx_ref[...])
    log_a = -RG_C * gate_a * _softplus(ap_ref[...])
    a = jnp.exp(log_a)
    th = jnp.tanh(log_a)
    u = xc * gate_x * jnp.sqrt(-2.0 * th / (1.0 - th))
    row = lax.broadcasted_iota(jnp.int32, (ts, GROUP_WIDTH), 0)
    d = 1
    while d < ts:
        keep = row >= d
        a_sh = jnp.where(keep, pltpu.roll(a, d, 0), 1.0)
        u_sh = jnp.where(keep, pltpu.roll(u, d, 0), 0.0)
        u = u + a * u_sh
        a = a * a_sh
        d *= 2
    h = u + a * h_ref[...]
    h_ref[...] = h[ts - 1:ts, :]
    gelu = 0.5 * gb * (1.0 + jnp.tanh(math.sqrt(2.0 / math.pi) * (gb + 0.044715 * (gb * gb * gb))))
    o_ref[...] = (h * gelu).astype(o_ref.dtype)


def _block_diag_weight(w):
    nb, wi, wo = w.shape
    out = jnp.zeros((nb * wi, nb * wo), w.dtype)
    for i in range(nb):
        out = out.at[i * wi:(i + 1) * wi, i * wo:(i + 1) * wo].set(w[i])
    return out


def _rglru(pc, conv_w, conv_b, w_a, b_a, w_x, b_x, a_param, batch, seq):
    ts = min(MIX_TS, seq)
    nst = seq // ts
    r = lambda p: p.astype(F32)[None, :]
    consts = [conv_w.astype(F32), r(conv_b), _block_diag_weight(w_a).astype(BF16), r(b_a),
              _block_diag_weight(w_x).astype(BF16), r(b_x), r(a_param)]
    full = lambda a: pl.BlockSpec(a.shape, lambda b, s: (0, 0))
    return pl.pallas_call(
        functools.partial(_rglru_kernel, ts=ts),
        grid=(batch, nst),
        in_specs=[pl.BlockSpec((ts, C_W), lambda b, s: (b * nst + s, 0))] + [full(a) for a in consts],
        out_specs=pl.BlockSpec((ts, GROUP_WIDTH), lambda b, s: (b * nst + s, 0)),
        out_shape=jax.ShapeDtypeStruct((batch * seq, GROUP_WIDTH), BF16),
        scratch_shapes=[pltpu.VMEM((SUBLANE, GROUP_WIDTH), F32), pltpu.VMEM((1, GROUP_WIDTH), F32)],
        compiler_params=_cparams(("parallel", "arbitrary")),
    )(pc, *consts)


def _gla_kernel(d_ref, wlr_ref, blr_ref, gn_ref, bd_ref, bdt_ref, lt_ref, o_ref, state_ref, *, ts):
    @pl.when(pl.program_id(1) == 0)
    def _init():
        state_ref[...] = jnp.zeros_like(state_ref)

    c = CHUNK
    scale = GLA_DK ** -0.5
    q = d_ref[:, 0:128] * scale
    k = d_ref[:, 128:256]
    v = d_ref[:, 256:512]
    r = d_ref[:, 512:768]
    z = _dot(d_ref[:, 768:896], wlr_ref[...], HI) + blr_ref[...]
    la = (jnp.minimum(z, 0.0) - jnp.log1p(jnp.exp(-jnp.abs(z)))) * (1.0 / GLA_TAU)
    kmasks = _head_mask(N_HEADS * GLA_DK, GLA_DK)
    vmasks = _head_mask(GROUP_WIDTH, HEAD_DIM)
    ri = lax.broadcasted_iota(jnp.int32, (N_HEADS * c, c), 0) % c
    cj = lax.broadcasted_iota(jnp.int32, (N_HEADS * c, c), 1)
    causal = ri >= cj
    bd = bd_ref[...]
    bdtm = bdt_ref[...] > 0.5
    lt = lt_ref[...]
    gn = gn_ref[...]
    state = state_ref[...]
    for ci in range(ts // c):
        sl = slice(ci * c, (ci + 1) * c)
        qc, kc, vc = q[sl], k[sl], v[sl]
        gc = _dot(lt, la[sl], HI)
        ref = gc[c // 2:c // 2 + 1, :]
        a_in = _dot_nt(_stack_heads(qc * jnp.exp(gc - ref), kmasks), kc * jnp.exp(ref - gc), HI)
        a_in = jnp.where(causal, a_in, 0.0)
        o_intra = _unstack_heads(_dot(a_in, vc, HI), vmasks, c)
        o_inter = _dot_nt(qc * jnp.exp(gc), state, HI)
        g_last = gc[c - 1:c, :]
        kd = kc * jnp.exp(g_last - gc)
        state = state * jnp.exp(g_last) + jnp.where(bdtm, _dot_tn(vc, kd, HI), 0.0)
        o = o_intra + o_inter
        o = o * lax.rsqrt(_dot(o * o, bd, HI) * (1.0 / HEAD_DIM) + EPS) * gn
        rt = r[sl]
        o_ref[sl, :] = (o * (rt * jax.nn.sigmoid(rt))).astype(o_ref.dtype)
    state_ref[...] = state


def _gla(pd, w_lr, b_lr, norm_g, batch, seq):
    ts = min(MIX_TS, seq)
    nst = seq // ts
    kw = N_HEADS * GLA_DK
    wlr = jnp.zeros((LANE, kw), F32).at[0:GLA_RANK, :].set(w_lr.astype(F32))
    bd = _block_diag_ones(GROUP_WIDTH, HEAD_DIM)
    iv = np.arange(GROUP_WIDTH)[:, None] // HEAD_DIM
    ik = np.arange(kw)[None, :] // GLA_DK
    bdt = jnp.asarray(iv == ik, F32)
    lt = jnp.asarray(np.tril(np.ones((CHUNK, CHUNK), np.float32)))
    consts = [wlr, b_lr.astype(F32)[None, :], jnp.tile(norm_g.astype(F32), N_HEADS)[None, :], bd, bdt, lt]
    full = lambda a: pl.BlockSpec(a.shape, lambda b, s: (0, 0))
    return pl.pallas_call(
        functools.partial(_gla_kernel, ts=ts),
        grid=(batch, nst),
        in_specs=[pl.BlockSpec((ts, D_W), lambda b, s: (b * nst + s, 0))] + [full(a) for a in consts],
        out_specs=pl.BlockSpec((ts, GROUP_WIDTH), lambda b, s: (b * nst + s, 0)),
        out_shape=jax.ShapeDtypeStruct((batch * seq, GROUP_WIDTH), BF16),
        scratch_shapes=[pltpu.VMEM((GROUP_WIDTH, kw), F32)],
        compiler_params=_cparams(("parallel", "arbitrary")),
    )(pd, *consts)


def _mix_outproj(x_ref, ma_ref, mb_ref, mc_ref, md_ref, wo_ref):
    mix = jnp.concatenate([ma_ref[...], mb_ref[...], mc_ref[...], md_ref[...]], axis=1)
    return x_ref[...] + _dot(mix, wo_ref[...])


def _ffn_kernel(x_ref, ma_ref, mb_ref, mc_ref, md_ref, wo_ref, g_ref, wg_ref, wu_ref, wd_ref,
                o_ref, h_ref, acc_ref):
    f = pl.program_id(1)

    @pl.when(f == 0)
    def _first():
        x1 = _mix_outproj(x_ref, ma_ref, mb_ref, mc_ref, md_ref, wo_ref)
        acc_ref[...] = x1
        h_ref[...] = _rms(x1, g_ref[...]).astype(BF16)

    h = h_ref[...]
    gt = _dot(h, wg_ref[...])
    act = (gt * jax.nn.sigmoid(gt) * _dot(h, wu_ref[...])).astype(BF16)
    acc_ref[...] += _dot(act, wd_ref[...])

    @pl.when(f == pl.num_programs(1) - 1)
    def _last():
        o_ref[...] = acc_ref[...]


def _outproj_ffn(x, mixes, w_out, g, w_gate, w_up, w_down):
    t = x.shape[0]
    tm = min(FFN_TM, t)
    tf = FFN_TF
    nf = D_FF // tf
    row = lambda w: pl.BlockSpec((tm, w), lambda i, f: (i, 0))
    return pl.pallas_call(
        _ffn_kernel,
        grid=(t // tm, nf),
        in_specs=[row(D_MODEL)] + [row(GROUP_WIDTH)] * 4 + [
            pl.BlockSpec((D_MODEL, D_MODEL), lambda i, f: (0, 0)),
            pl.BlockSpec((1, D_MODEL), lambda i, f: (0, 0)),
            pl.BlockSpec((D_MODEL, tf), lambda i, f: (0, f)),
            pl.BlockSpec((D_MODEL, tf), lambda i, f: (0, f)),
            pl.BlockSpec((tf, D_MODEL), lambda i, f: (f, 0))],
        out_specs=row(D_MODEL),
        out_shape=jax.ShapeDtypeStruct((t, D_MODEL), F32),
        scratch_shapes=[pltpu.VMEM((tm, D_MODEL), BF16), pltpu.VMEM((tm, D_MODEL), F32)],
        compiler_params=_cparams(("parallel", "arbitrary")),
    )(x, *mixes, w_out, g, w_gate, w_up, w_down)


def _router_kernel(x_ref, ma_ref, mb_ref, mc_ref, md_ref, wo_ref, g_ref, wr_ref, x1_ref, h_ref, r_ref):
    x1 = _mix_outproj(x_ref, ma_ref, mb_ref, mc_ref, md_ref, wo_ref)
    x1_ref[...] = x1
    h = _rms(x1, g_ref[...])
    h_ref[...] = h.astype(BF16)
    lane = lax.broadcasted_iota(jnp.int32, (x1.shape[0], LANE), 1)
    logits = jnp.where(lane < N_EXPERTS, _dot(h, wr_ref[...], HI), NEG)
    m1 = jnp.max(logits, axis=1, keepdims=True)
    e1 = jnp.min(jnp.where(logits == m1, lane, LANE), axis=1, keepdims=True)
    rest = jnp.where(lane == e1, NEG, logits)
    m2 = jnp.max(rest, axis=1, keepdims=True)
    e2 = jnp.min(jnp.where(rest == m2, lane, LANE), axis=1, keepdims=True)
    ex = jnp.exp(m2 - m1)
    w1 = 1.0 / (1.0 + ex)
    w2 = ex / (1.0 + ex)
    r_ref[...] = jnp.where(lane == 0, e1.astype(F32),
                           jnp.where(lane == 1, e2.astype(F32),
                                     jnp.where(lane == 2, w1, jnp.where(lane == 3, w2, 0.0))))


def _outproj_router(x, mixes, w_out, g, w_router):
    t = x.shape[0]
    tm = min(FFN_TM, t)
    wr = jnp.zeros((D_MODEL, LANE), F32).at[:, 0:N_EXPERTS].set(w_router.astype(F32))
    row = lambda w: pl.BlockSpec((tm, w), lambda i: (i, 0))
    full = lambda a: pl.BlockSpec(a.shape, lambda i: (0, 0))
    return pl.pallas_call(
        _router_kernel,
        grid=(t // tm,),
        in_specs=[row(D_MODEL)] + [row(GROUP_WIDTH)] * 4 + [full(w_out), full(g), full(wr)],
        out_specs=[row(D_MODEL), row(D_MODEL), row(LANE)],
        out_shape=[jax.ShapeDtypeStruct((t, D_MODEL), F32), jax.ShapeDtypeStruct((t, D_MODEL), BF16),
                   jax.ShapeDtypeStruct((t, LANE), F32)],
        compiler_params=_cparams(("parallel",)),
    )(x, *mixes, w_out, g, wr)


def _moe_kernel(te_ref, na_ref, x_ref, sw_ref, wg_ref, wu_ref, wd_ref, o_ref, acc_ref):
    i = pl.program_id(0)
    f = pl.program_id(1)

    @pl.when(i < na_ref[0])
    def _active():
        @pl.when(f == 0)
        def _zero():
            acc_ref[...] = jnp.zeros_like(acc_ref)

        x = x_ref[...]
        gt = _dot(x, wg_ref[...].astype(BF16))
        act = (gt * jax.nn.sigmoid(gt) * _dot(x, wu_ref[...].astype(BF16))).astype(BF16)
        acc_ref[...] += _dot(act, wd_ref[...].astype(BF16))

        @pl.when(f == pl.num_programs(1) - 1)
        def _last():
            o_ref[...] = acc_ref[...] * sw_ref[...]


def _moe_experts(xs, slot_w, tile_e, n_active, w_gate, w_up, w_down, n_tiles):
    tm, tf = MOE_TM, MOE_TF
    nf = D_FF_EXPERT // tf

    def tile(i, na):
        return jnp.minimum(i, na[0] - 1)

    def fidx(i, f, na):
        return jnp.where(i < na[0], f, nf - 1)

    grid_spec = pltpu.PrefetchScalarGridSpec(
        num_scalar_prefetch=2,
        grid=(n_tiles, nf),
        in_specs=[
            pl.BlockSpec((tm, D_MODEL), lambda i, f, te, na: (tile(i, na), 0)),
            pl.BlockSpec((tm, 1), lambda i, f, te, na: (tile(i, na), 0)),
            pl.BlockSpec((None, D_MODEL, tf), lambda i, f, te, na: (te[tile(i, na)], 0, fidx(i, f, na))),
            pl.BlockSpec((None, D_MODEL, tf), lambda i, f, te, na: (te[tile(i, na)], 0, fidx(i, f, na))),
            pl.BlockSpec((None, tf, D_MODEL), lambda i, f, te, na: (te[tile(i, na)], fidx(i, f, na), 0)),
        ],
        out_specs=pl.BlockSpec((tm, D_MODEL), lambda i, f, te, na: (tile(i, na), 0)),
        scratch_shapes=[pltpu.VMEM((tm, D_MODEL), F32)],
    )
    return pl.pallas_call(
        _moe_kernel,
        grid_spec=grid_spec,
        out_shape=jax.ShapeDtypeStruct((n_tiles * tm, D_MODEL), F32),
        compiler_params=_cparams(("arbitrary", "arbitrary")),
    )(tile_e, n_active, xs, slot_w, w_gate, w_up, w_down)


def _route(route, t):
    tm = MOE_TM
    top_e = route[:, 0:TOP_K].astype(jnp.int32)
    top_w = route[:, TOP_K:2 * TOP_K]
    n_assign = t * TOP_K
    flat_e = top_e.reshape(n_assign)
    onehot = (flat_e[:, None] == jnp.arange(N_EXPERTS)[None, :]).astype(jnp.int32)
    rank = jnp.take_along_axis(jnp.cumsum(onehot, axis=0), flat_e[:, None], axis=1)[:, 0] - 1
    counts = jnp.sum(onehot, axis=0)
    padded = (counts + tm - 1) // tm * tm
    pad_end = jnp.cumsum(padded)
    pad_start = pad_end - padded
    dest = pad_start[flat_e] + rank
    n_tiles = -(-n_assign // tm) + N_EXPERTS
    n_slots = n_tiles * tm
    slot_tok = jnp.zeros((n_slots,), jnp.int32).at[dest].set(jnp.arange(n_assign, dtype=jnp.int32) // TOP_K)
    slot_w = jnp.zeros((n_slots,), F32).at[dest].set(top_w.reshape(n_assign))
    tile_e = jnp.minimum(jnp.searchsorted(pad_end, jnp.arange(n_tiles) * tm, side='right'),
                         N_EXPERTS - 1).astype(jnp.int32)
    n_active = (pad_end[-1] // tm).astype(jnp.int32).reshape(1)
    return slot_tok, slot_w, tile_e, n_active, dest.reshape(t, TOP_K), n_tiles


def _combine_kernel(x_ref, y0_ref, y1_ref, g_ref, o_ref):
    o_ref[...] = _rms(x_ref[...] + (y0_ref[...] + y1_ref[...]), g_ref[...])


def _combine_final(x1, y0, y1, g):
    t = x1.shape[0]
    tm = min(FFN_TM, t)
    row = pl.BlockSpec((tm, D_MODEL), lambda i: (i, 0))
    return pl.pallas_call(
        _combine_kernel,
        grid=(t // tm,),
        in_specs=[row, row, row, pl.BlockSpec((1, D_MODEL), lambda i: (0, 0))],
        out_specs=row,
        out_shape=jax.ShapeDtypeStruct((t, D_MODEL), F32),
        compiler_params=_cparams(("parallel",)),
    )(x1, y0, y1, g)


def _final_norm_kernel(x_ref, g_ref, o_ref):
    o_ref[...] = _rms(x_ref[...], g_ref[...])


def _split_w_in(w):
    def pad(a, width):
        return jnp.pad(a, ((0, 0), (0, width - a.shape[1])))
    a_end = 4 * GROUP_WIDTH + 2 * N_HEADS
    b_end = a_end + 3 * GROUP_WIDTH
    c_end = b_end + 2 * GROUP_WIDTH
    wa = pad(w[:, 0:a_end], A_W)
    wb = w[:, a_end:b_end]
    wc = w[:, b_end:c_end]
    wd = pad(w[:, c_end:], D_W)
    return tuple(m.astype(BF16) for m in (wa, wb, wc, wd))


def kernel(x, norm_mix, w_in, a_conv, a_A_log, a_dt_bias, a_norm, b_lambda, b_norm, rel_bias,
           c_conv_w, c_conv_b, c_w_a, c_b_a, c_w_x, c_b_x, c_a_param, d_w_lr, d_b_lr, d_norm,
           w_out, norm_ffn, ffn_w_gate, ffn_w_up, ffn_w_down, moe_router, moe_w_gate, moe_w_up,
           moe_w_down, norm_final):
    batch, seq, _ = x.shape
    depth = w_in.shape[0]
    t = batch * seq
    xt = x.reshape(t, D_MODEL).astype(F32)
    bias_tiles = _attn_bias_tiles(rel_bias)
    row = lambda p: p.astype(F32)[None, :]
    out = None
    for l in range(depth):
        pa, pb, pc, pd = _inproj(xt, row(norm_mix[l]), *_split_w_in(w_in[l]))
        lam_init = 0.8 - 0.6 * math.exp(-0.3 * l)
        mixes = (
            _gdn(pa, a_conv[l], a_A_log[l], a_dt_bias[l], a_norm[l], batch, seq),
            _diff_attn(pb, b_lambda[l], lam_init, bias_tiles, b_norm[l], batch, seq),
            _rglru(pc, c_conv_w[l], c_conv_b[l], c_w_a[l], c_b_a[l], c_w_x[l], c_b_x[l],
                   c_a_param[l], batch, seq),
            _gla(pd, d_w_lr[l], d_b_lr[l], d_norm[l], batch, seq),
        )
        wo = w_out[l].astype(BF16)
        if l % 2 == 0:
            j = l // 2
            xt = _outproj_ffn(xt, mixes, wo, row(norm_ffn[l]), ffn_w_gate[j].astype(BF16),
                              ffn_w_up[j].astype(BF16), ffn_w_down[j].astype(BF16))
            out = None
        else:
            j = l // 2
            x1, h, route = _outproj_router(xt, mixes, wo, row(norm_ffn[l]), moe_router[j])
            slot_tok, slot_w, tile_e, n_active, dest, n_tiles = _route(route, t)
            xs = jnp.take(h, slot_tok, axis=0)
            ys = _moe_experts(xs, slot_w[:, None], tile_e, n_active, moe_w_gate[j], moe_w_up[j],
                              moe_w_down[j], n_tiles)
            y0 = jnp.take(ys, dest[:, 0], axis=0)
            y1 = jnp.take(ys, dest[:, 1], axis=0)
            if l == depth - 1:
                out = _combine_final(x1, y0, y1, row(norm_final))
            else:
                xt = x1 + (y0 + y1)
    if out is None:
        tm = min(FFN_TM, t)
        rowspec = pl.BlockSpec((tm, D_MODEL), lambda i: (i, 0))
        out = pl.pallas_call(
            _final_norm_kernel, grid=(t // tm,),
            in_specs=[rowspec, pl.BlockSpec((1, D_MODEL), lambda i: (0, 0))],
            out_specs=rowspec, out_shape=jax.ShapeDtypeStruct((t, D_MODEL), F32),
            compiler_params=_cparams(("parallel",)),
        )(xt, row(norm_final))
    return out.reshape(batch, seq, D_MODEL).astype(x.dtype)
```

```python
import functools
import math

import jax
import jax.numpy as jnp
import numpy as np
from jax import lax
from jax.experimental import pallas as pl
from jax.experimental.pallas import tpu as pltpu

D_MODEL = 1024
CHUNK = 64
N_HEADS = 4
HEAD_DIM = 64
GROUP_WIDTH = 256
CONV_WIDTH = 4
DIFF_DH = 32
Q_BLOCK = 128
REL_BUCKETS = 32
REL_MAX_DIST = 128
RG_C = 8.0
GLA_DK = 32
GLA_RANK = 16
GLA_TAU = 16.0
D_FF = 2816
N_EXPERTS = 8
TOP_K = 2
D_FF_EXPERT = 3584
EPS = 1e-6

LANE = 128
SUBLANE = 8
VMEM_LIMIT = 56 * 1024 * 1024

F32 = jnp.float32
BF16 = jnp.bfloat16
HI = lax.Precision.HIGHEST
NEG = -1e30
LOG2E = math.log2(math.e)

A_W = 1152
B_W = 768
C_W = 512
D_W = 896

PROJ_TM = 512
MIX_TS = 256
ATT_T = 128
ATT_VROWS = 80
ATT_UNROLL = 4
FFN_TM = 512
FFN_TF = 1408
MOE_TM = 1024
MOE_TF = 512


def _cparams(sem):
    return pltpu.CompilerParams(dimension_semantics=sem, vmem_limit_bytes=VMEM_LIMIT)


def _dot(a, b, precision=None):
    return jnp.dot(a, b, preferred_element_type=F32, precision=precision)


def _dot_nt(a, b, precision=None):
    return lax.dot_general(a, b, (((1,), (1,)), ((), ())), preferred_element_type=F32,
                           precision=precision)


def _dot_tn(a, b, precision=None):
    return lax.dot_general(a, b, (((0,), (0,)), ((), ())), preferred_element_type=F32,
                           precision=precision)


def _softplus(x):
    return jnp.maximum(x, 0.0) + jnp.log1p(jnp.exp(-jnp.abs(x)))


def _rms(x, g):
    return x * lax.rsqrt(jnp.mean(x * x, axis=-1, keepdims=True) + EPS) * g


def _causal_conv(x, tail, w):
    row = lax.broadcasted_iota(jnp.int32, (SUBLANE, x.shape[1]), 0)
    y = x * w[CONV_WIDTH - 1:CONV_WIDTH, :]
    for d in range(1, CONV_WIDTH):
        rolled = pltpu.roll(x, d, 0)
        first = jnp.where(row < d, pltpu.roll(tail, d, 0), rolled[:SUBLANE])
        shifted = jnp.concatenate([first, rolled[SUBLANE:]], axis=0)
        y = y + shifted * w[CONV_WIDTH - 1 - d:CONV_WIDTH - d, :]
    return y


def _head_mask(width, per_head):
    lane = lax.broadcasted_iota(jnp.int32, (1, width), 1)
    return [(lane // per_head) == h for h in range(N_HEADS)]


def _stack_heads(x, masks):
    return jnp.concatenate([jnp.where(m, x, 0.0) for m in masks], axis=0)


def _unstack_heads(r, masks, c):
    out = jnp.where(masks[0], r[0:c], 0.0)
    for h in range(1, N_HEADS):
        out = out + jnp.where(masks[h], r[h * c:(h + 1) * c], 0.0)
    return out


def _inproj_kernel(x_ref, g_ref, wa_ref, wb_ref, wc_ref, wd_ref, oa_ref, ob_ref, oc_ref, od_ref):
    h = _rms(x_ref[...], g_ref[...]).astype(BF16)
    oa_ref[...] = _dot(h, wa_ref[...])
    ob_ref[...] = _dot(h, wb_ref[...])
    oc_ref[...] = _dot(h, wc_ref[...])
    od_ref[...] = _dot(h, wd_ref[...])


def _inproj(x, g, wa, wb, wc, wd):
    t = x.shape[0]
    tm = min(PROJ_TM, t)
    row = lambda w: pl.BlockSpec((tm, w), lambda i: (i, 0))
    full = lambda a: pl.BlockSpec(a.shape, lambda i: (0, 0))
    return pl.pallas_call(
        _inproj_kernel,
        name="inproj",
        grid=(t // tm,),
        in_specs=[row(D_MODEL), full(g), full(wa), full(wb), full(wc), full(wd)],
        out_specs=[row(A_W), row(B_W), row(C_W), row(D_W)],
        out_shape=[jax.ShapeDtypeStruct((t, w), F32) for w in (A_W, B_W, C_W, D_W)],
        compiler_params=_cparams(("parallel",)),
    )(x, g, wa, wb, wc, wd)


def _gdn_kernel(a_ref, convw_ref, alog_ref, dtb_ref, gn_ref, bd_ref, lt_ref, eb_ref, ea_ref,
                sel_ref, o_ref, tail_ref, state_ref, *, ts):
    @pl.when(pl.program_id(1) == 0)
    def _init():
        tail_ref[...] = jnp.zeros_like(tail_ref)
        state_ref[...] = jnp.zeros_like(state_ref)

    c = CHUNK
    xin = a_ref[:, 0:768]
    y = _causal_conv(xin, tail_ref[...], convw_ref[...])
    tail_ref[...] = xin[ts - SUBLANE:ts, :]
    y = y * jax.nn.sigmoid(y)
    bd = bd_ref[...]
    q = y[:, 0:256]
    k = y[:, 256:512]
    v = y[:, 512:768]
    q = q * lax.rsqrt(_dot(q * q, bd, HI) + EPS) * (HEAD_DIM ** -0.5)
    k = k * lax.rsqrt(_dot(k * k, bd, HI) + EPS)
    ba = a_ref[:, 1024:1152]
    beta = jax.nn.sigmoid(_dot(ba, eb_ref[...], HI))
    g = -jnp.exp(alog_ref[...]) * _softplus(_dot(ba, ea_ref[...], HI) + dtb_ref[...])
    gate = a_ref[:, 768:1024]

    masks = _head_mask(GROUP_WIDTH, HEAD_DIM)
    ri = lax.broadcasted_iota(jnp.int32, (N_HEADS * c, c), 0) % c
    cj = lax.broadcasted_iota(jnp.int32, (N_HEADS * c, c), 1)
    causal = ri >= cj
    strict = ri > cj
    eye3 = (lax.broadcasted_iota(jnp.int32, (N_HEADS, c, c), 1)
            == lax.broadcasted_iota(jnp.int32, (N_HEADS, c, c), 2)).astype(F32)
    bdm = bd > 0.5
    lt = lt_ref[...]
    sel = sel_ref[...]
    gn = gn_ref[...]

    def bmm(x3, y3):
        return jnp.einsum('hij,hjk->hik', x3, y3, preferred_element_type=F32, precision=HI)

    state = state_ref[...]
    for ci in range(ts // c):
        sl = slice(ci * c, (ci + 1) * c)
        qc, kc, vc, bc = q[sl], k[sl], v[sl], beta[sl]
        gc = _dot(lt, g[sl], HI)
        eg = jnp.exp(gc)
        g_last = gc[c - 1:c, :]
        kb = kc * bc
        gcol = jnp.concatenate([gc[:, h * HEAD_DIM:(h + 1) * HEAD_DIM] for h in range(N_HEADS)], axis=0)
        grow = _dot_nt(sel, gc, HI)
        gamma = jnp.exp(jnp.where(causal, gcol - grow, NEG))
        a_kk = jnp.where(strict, _dot_nt(_stack_heads(kb, masks), kc, HI) * gamma, 0.0)
        a_qk = _dot_nt(_stack_heads(qc, masks), kc, HI) * gamma
        p = -a_kk.reshape(N_HEADS, c, c)
        inv = eye3 + p
        for _ in range(5):
            p = bmm(p, p)
            inv = inv + bmm(inv, p)
        rhs = jnp.concatenate([vc * bc, kb * eg], axis=1)
        sol = _dot(inv.reshape(N_HEADS * c, c), rhs, HI)
        u = _unstack_heads(sol[:, 0:256], masks, c)
        w = _unstack_heads(sol[:, 256:512], masks, c)
        ws_qs = _dot(jnp.concatenate([w, qc * eg], axis=0), state, HI)
        v_new = u - ws_qs[0:c]
        o = ws_qs[c:2 * c] + _unstack_heads(_dot(a_qk, v_new, HI), masks, c)
        kd = kc * jnp.exp(g_last - gc)
        state = state * jnp.exp(g_last) + jnp.where(bdm, _dot_tn(kd, v_new, HI), 0.0)
        o = o * lax.rsqrt(_dot(o * o, bd, HI) * (1.0 / HEAD_DIM) + EPS) * gn
        gt = gate[sl]
        o_ref[sl, :] = (o * (gt * jax.nn.sigmoid(gt))).astype(o_ref.dtype)
    state_ref[...] = state


def _block_diag_ones(n, blk):
    i = np.arange(n)
    return jnp.asarray((i[:, None] // blk) == (i[None, :] // blk), F32)


def _gdn(pa, conv_w, a_log, dt_bias, norm_g, batch, seq):
    ts = min(MIX_TS, seq)
    nst = seq // ts
    bd = _block_diag_ones(GROUP_WIDTH, HEAD_DIM)
    lt = jnp.asarray(np.tril(np.ones((CHUNK, CHUNK), np.float32)))
    lane = np.arange(GROUP_WIDTH)
    eb = np.zeros((LANE, GROUP_WIDTH), np.float32)
    ea = np.zeros((LANE, GROUP_WIDTH), np.float32)
    sel = np.zeros((N_HEADS * CHUNK, GROUP_WIDTH), np.float32)
    for h in range(N_HEADS):
        eb[h, lane // HEAD_DIM == h] = 1.0
        ea[N_HEADS + h, lane // HEAD_DIM == h] = 1.0
        sel[h * CHUNK:(h + 1) * CHUNK, h * HEAD_DIM] = 1.0
    rep = lambda p: jnp.repeat(p.astype(F32), HEAD_DIM)[None, :]
    consts = [conv_w.astype(F32), rep(a_log), rep(dt_bias),
              jnp.tile(norm_g.astype(F32), N_HEADS)[None, :], bd, lt,
              jnp.asarray(eb), jnp.asarray(ea), jnp.asarray(sel)]
    full = lambda a: pl.BlockSpec(a.shape, lambda b, s: (0, 0))
    return pl.pallas_call(
        functools.partial(_gdn_kernel, ts=ts),
        name="gdn",
        grid=(batch, nst),
        in_specs=[pl.BlockSpec((ts, A_W), lambda b, s: (b * nst + s, 0))] + [full(a) for a in consts],
        out_specs=pl.BlockSpec((ts, GROUP_WIDTH), lambda b, s: (b * nst + s, 0)),
        out_shape=jax.ShapeDtypeStruct((batch * seq, GROUP_WIDTH), BF16),
        scratch_shapes=[pltpu.VMEM((SUBLANE, 768), F32), pltpu.VMEM((GROUP_WIDTH, GROUP_WIDTH), F32)],
        compiler_params=_cparams(("parallel", "arbitrary")),
    )(pa, *consts)


def _attn_kernel(q_ref, k_ref, v_ref, bias_ref, lamv_ref, gn_ref, o_ref,
                 kb_ref, va_ref, qt_ref, s_ref, p_ref, al_ref, m_ref, l_ref, acc_ref, *, lam_init, seq):
    t = ATT_T
    qi = pl.program_id(1)
    n = qi + 1
    nlan = 2 * N_HEADS * t

    @pl.when(qi == 0)
    def _stage_kv():
        def body(j, carry):
            rows = pl.ds(pl.multiple_of(j * t, t), t)
            kb_ref[rows, :] = k_ref[rows, :].astype(BF16)
            vt = v_ref[rows, :].T.astype(BF16)
            for h in range(N_HEADS):
                va_ref[j, h, 0:HEAD_DIM, :] = vt[h * HEAD_DIM:(h + 1) * HEAD_DIM, :]
                va_ref[j, h, HEAD_DIM:ATT_VROWS, :] = jnp.ones((ATT_VROWS - HEAD_DIM, t), BF16)
            return carry
        lax.fori_loop(0, seq // t, body, 0)

    qt = (q_ref[...] * (DIFF_DH ** -0.5 * LOG2E)).T
    feat = lax.broadcasted_iota(jnp.int32, (GROUP_WIDTH, t), 0) // DIFF_DH
    for idx in range(2 * N_HEADS):
        qt_ref[:, idx * t:(idx + 1) * t] = jnp.where(feat == idx, qt, 0.0).astype(BF16)
    m_ref[...] = jnp.full((1, nlan), NEG, F32)
    l_ref[...] = jnp.zeros((1, nlan), F32)
    acc_ref[...] = jnp.zeros((HEAD_DIM, nlan), F32)
    p_ref[1] = jnp.zeros((t, nlan), BF16)
    al_ref[1] = jnp.ones((1, nlan), F32)

    head_cols = [slice(2 * h * t, (2 * h + 2) * t) for h in range(N_HEADS)]

    def scores(j, slot, h):
        rows = pl.ds(pl.multiple_of(j * t, t), t)
        s_ref[slot, :, head_cols[h]] = _dot(kb_ref[rows, :], qt_ref[:, head_cols[h]])

    def softmax(slot, h, bias_idx):
        cols = head_cols[h]
        s = s_ref[slot, :, cols]
        if bias_idx is not None:
            s = s + bias_ref[bias_idx, :, cols]
        m_prev = m_ref[:, cols]
        m_new = jnp.maximum(m_prev, jnp.max(s, axis=0, keepdims=True))
        al_ref[slot, :, cols] = jnp.exp2(m_prev - m_new)
        m_ref[:, cols] = m_new
        p_ref[slot, :, cols] = jnp.exp2(s - m_new).astype(BF16)

    def values(j, slot, h):
        cols = head_cols[h]
        alpha = al_ref[slot, :, cols]
        pv = _dot(va_ref[jnp.maximum(j, 0), h], p_ref[slot, :, cols])
        acc_ref[:, cols] = alpha * acc_ref[:, cols] + pv[0:HEAD_DIM]
        l_ref[:, cols] = alpha * l_ref[:, cols] + pv[HEAD_DIM:HEAD_DIM + 1]

    def pipe_step(i, slot, bias_idx):
        for h in range(N_HEADS):
            scores(i, slot, h)
            softmax(1 - slot, h, bias_idx)
            values(i - 2, slot, h)

    for h in range(N_HEADS):
        scores(0, 0, h)

    n_far_steps = jnp.maximum(n - 2, 0)

    def unrolled_body(k, carry):
        for u in range(ATT_UNROLL):
            pipe_step(ATT_UNROLL * k + 1 + u, (1 + u) % 2, None)
        return carry
    lax.fori_loop(0, n_far_steps // ATT_UNROLL, unrolled_body, 0)

    i_rem = 1 + (n_far_steps // ATT_UNROLL) * ATT_UNROLL
    for u in range(ATT_UNROLL - 1):
        @pl.when(n_far_steps % ATT_UNROLL > u)
        def _rem_step(u=u):
            pipe_step(i_rem + u, (1 + u) % 2, None)

    def finish(last):
        for h in range(N_HEADS):
            softmax(last, h, 2)
            values(n - 2, 1 - last, h)
        for h in range(N_HEADS):
            values(n - 1, last, h)

    @pl.when(n == 1)
    def _only_diag():
        finish(0)

    for last in range(2):
        @pl.when(jnp.logical_and(n >= 2, (n - 1) % 2 == last))
        def _near_and_diag(last=last):
            pipe_step(n - 1, last, 1)
            finish(last)

    lv = lamv_ref[...]
    lam = (jnp.exp(jnp.sum(lv[0:1] * lv[1:2], axis=1, keepdims=True))
           - jnp.exp(jnp.sum(lv[2:3] * lv[3:4], axis=1, keepdims=True)) + lam_init)
    inv_l = 1.0 / l_ref[...]
    outs = []
    for h in range(N_HEADS):
        c0 = slice(2 * h * t, (2 * h + 1) * t)
        c1 = slice((2 * h + 1) * t, (2 * h + 2) * t)
        oh = acc_ref[:, c0] * inv_l[:, c0] - lam * (acc_ref[:, c1] * inv_l[:, c1])
        oh = oh * lax.rsqrt(jnp.mean(oh * oh, axis=0, keepdims=True) + EPS)
        outs.append(oh)
    o = jnp.concatenate(outs, axis=0).T
    o_ref[...] = (o * gn_ref[...] * (1.0 - lam_init)).astype(o_ref.dtype)


def _t5_bucket(rel):
    nb = REL_BUCKETS // 2
    bucket = jnp.where(rel > 0, nb, 0)
    n = jnp.abs(rel)
    max_exact = nb // 2
    large = max_exact + (jnp.log(jnp.maximum(n, 1).astype(F32) / max_exact)
                         / math.log(REL_MAX_DIST / max_exact) * (nb - max_exact)).astype(jnp.int32)
    large = jnp.minimum(large, nb - 1)
    return bucket + jnp.where(n < max_exact, n, large)


def _attn_bias_tiles(rel_bias):
    t = ATT_T
    table = rel_bias.astype(F32)
    kk = jnp.arange(t)[:, None]
    qq = jnp.arange(t)[None, :]

    def expand(b):
        b = jnp.transpose(b, (0, 2, 1))
        b = jnp.broadcast_to(b[:, :, None, :], (t, N_HEADS, 2, t))
        return b.reshape(t, 2 * N_HEADS * t)

    table = table * LOG2E
    diag = table[_t5_bucket(kk - qq)]
    diag = jnp.where(((kk // CHUNK) <= (qq // CHUNK))[:, :, None], diag, NEG)
    near = table[_t5_bucket(kk - qq - t)]
    far = table[_t5_bucket(jnp.full((1, 1), -(REL_MAX_DIST + 1), jnp.int32))]
    far = jnp.broadcast_to(far, (t, t, N_HEADS))
    return jnp.stack([jnp.zeros((t, 2 * N_HEADS * t), F32), expand(near - far), expand(diag - far)])


def _diff_attn(pb, lam_vecs, lam_init, bias_tiles, norm_g, batch, seq):
    t = ATT_T
    nq = seq // t
    gn = jnp.tile(norm_g.astype(F32), N_HEADS)[None, :]
    lamv = lam_vecs.astype(F32)
    nlan = 2 * N_HEADS * t
    full = lambda a: pl.BlockSpec(a.shape, lambda b, i: (0,) * a.ndim)
    return pl.pallas_call(
        functools.partial(_attn_kernel, lam_init=lam_init, seq=seq),
        name="diffattn",
        grid=(batch, nq),
        in_specs=[pl.BlockSpec((t, GROUP_WIDTH), lambda b, i: (b * nq + i, 0)),
                  pl.BlockSpec((seq, GROUP_WIDTH), lambda b, i: (b, 1)),
                  pl.BlockSpec((seq, GROUP_WIDTH), lambda b, i: (b, 2)),
                  full(bias_tiles), full(lamv), full(gn)],
        out_specs=pl.BlockSpec((t, GROUP_WIDTH), lambda b, i: (b * nq + i, 0)),
        out_shape=jax.ShapeDtypeStruct((batch * seq, GROUP_WIDTH), BF16),
        scratch_shapes=[pltpu.VMEM((seq, GROUP_WIDTH), BF16),
                        pltpu.VMEM((seq // t, N_HEADS, ATT_VROWS, t), BF16),
                        pltpu.VMEM((GROUP_WIDTH, nlan), BF16),
                        pltpu.VMEM((2, t, nlan), F32),
                        pltpu.VMEM((2, t, nlan), BF16),
                        pltpu.VMEM((2, 1, nlan), F32),
                        pltpu.VMEM((1, nlan), F32), pltpu.VMEM((1, nlan), F32),
                        pltpu.VMEM((HEAD_DIM, nlan), F32)],
        compiler_params=_cparams(("parallel", "arbitrary")),
    )(pb, pb, pb, bias_tiles, lamv, gn)


def _rglru_kernel(c_ref, convw_ref, convb_ref, wa_ref, ba_ref, wx_ref, bx_ref, ap_ref, o_ref,
                  tail_ref, h_ref, *, ts):
    @pl.when(pl.program_id(1) == 0)
    def _init():
        tail_ref[...] = jnp.zeros_like(tail_ref)
        h_ref[...] = jnp.zeros_like(h_ref)

    xb = c_ref[:, 0:256]
    gb = c_ref[:, 256:512]
    xc = _causal_conv(xb, tail_ref[...], convw_ref[...]) + convb_ref[...]
    tail_ref[...] = xb[ts - SUBLANE:ts, :]
    xcb = xc.astype(BF16)
    gate_a = jax.nn.sigmoid(_dot(xcb, wa_ref[...]) + ba_ref[...])
    gate_x = jax.nn.sigmoid(_dot(xcb, wx_ref[...]) + bx_ref[...])
    log_a = -RG_C * gate_a * _softplus(ap_ref[...])
    a = jnp.exp(log_a)
    th = jnp.tanh(log_a)
    u = xc * gate_x * jnp.sqrt(-2.0 * th / (1.0 - th))
    row = lax.broadcasted_iota(jnp.int32, (ts, GROUP_WIDTH), 0)
    d = 1
    while d < ts:
        keep = row >= d
        a_sh = jnp.where(keep, pltpu.roll(a, d, 0), 1.0)
        u_sh = jnp.where(keep, pltpu.roll(u, d, 0), 0.0)
        u = u + a * u_sh
        a = a * a_sh
        d *= 2
    h = u + a * h_ref[...]
    h_ref[...] = h[ts - 1:ts, :]
    gelu = 0.5 * gb * (1.0 + jnp.tanh(math.sqrt(2.0 / math.pi) * (gb + 0.044715 * (gb * gb * gb))))
    o_ref[...] = (h * gelu).astype(o_ref.dtype)


def _block_diag_weight(w):
    nb, wi, wo = w.shape
    out = jnp.zeros((nb * wi, nb * wo), w.dtype)
    for i in range(nb):
        out = out.at[i * wi:(i + 1) * wi, i * wo:(i + 1) * wo].set(w[i])
    return out


def _rglru(pc, conv_w, conv_b, w_a, b_a, w_x, b_x, a_param, batch, seq):
    ts = min(MIX_TS, seq)
    nst = seq // ts
    r = lambda p: p.astype(F32)[None, :]
    consts = [conv_w.astype(F32), r(conv_b), _block_diag_weight(w_a).astype(BF16), r(b_a),
              _block_diag_weight(w_x).astype(BF16), r(b_x), r(a_param)]
    full = lambda a: pl.BlockSpec(a.shape, lambda b, s: (0, 0))
    return pl.pallas_call(
        functools.partial(_rglru_kernel, ts=ts),
        name="rglru",
        grid=(batch, nst),
        in_specs=[pl.BlockSpec((ts, C_W), lambda b, s: (b * nst + s, 0))] + [full(a) for a in consts],
        out_specs=pl.BlockSpec((ts, GROUP_WIDTH), lambda b, s: (b * nst + s, 0)),
        out_shape=jax.ShapeDtypeStruct((batch * seq, GROUP_WIDTH), BF16),
        scratch_shapes=[pltpu.VMEM((SUBLANE, GROUP_WIDTH), F32), pltpu.VMEM((1, GROUP_WIDTH), F32)],
        compiler_params=_cparams(("parallel", "arbitrary")),
    )(pc, *consts)


def _gla_kernel(d_ref, wlr_ref, blr_ref, gn_ref, bd_ref, bdt_ref, lt_ref, o_ref, state_ref, *, ts):
    @pl.when(pl.program_id(1) == 0)
    def _init():
        state_ref[...] = jnp.zeros_like(state_ref)

    c = CHUNK
    scale = GLA_DK ** -0.5
    q = d_ref[:, 0:128] * scale
    k = d_ref[:, 128:256]
    v = d_ref[:, 256:512]
    r = d_ref[:, 512:768]
    z = _dot(d_ref[:, 768:896], wlr_ref[...], HI) + blr_ref[...]
    la = (jnp.minimum(z, 0.0) - jnp.log1p(jnp.exp(-jnp.abs(z)))) * (1.0 / GLA_TAU)
    kmasks = _head_mask(N_HEADS * GLA_DK, GLA_DK)
    vmasks = _head_mask(GROUP_WIDTH, HEAD_DIM)
    ri = lax.broadcasted_iota(jnp.int32, (N_HEADS * c, c), 0) % c
    cj = lax.broadcasted_iota(jnp.int32, (N_HEADS * c, c), 1)
    causal = ri >= cj
    bd = bd_ref[...]
    bdtm = bdt_ref[...] > 0.5
    lt = lt_ref[...]
    gn = gn_ref[...]
    state = state_ref[...]
    for ci in range(ts // c):
        sl = slice(ci * c, (ci + 1) * c)
        qc, kc, vc = q[sl], k[sl], v[sl]
        gc = _dot(lt, la[sl], HI)
        ref = gc[c // 2:c // 2 + 1, :]
        a_in = _dot_nt(_stack_heads(qc * jnp.exp(gc - ref), kmasks), kc * jnp.exp(ref - gc), HI)
        a_in = jnp.where(causal, a_in, 0.0)
        o_intra = _unstack_heads(_dot(a_in, vc, HI), vmasks, c)
        o_inter = _dot_nt(qc * jnp.exp(gc), state, HI)
        g_last = gc[c - 1:c, :]
        kd = kc * jnp.exp(g_last - gc)
        state = state * jnp.exp(g_last) + jnp.where(bdtm, _dot_tn(vc, kd, HI), 0.0)
        o = o_intra + o_inter
        o = o * lax.rsqrt(_dot(o * o, bd, HI) * (1.0 / HEAD_DIM) + EPS) * gn
        rt = r[sl]
        o_ref[sl, :] = (o * (rt * jax.nn.sigmoid(rt))).astype(o_ref.dtype)
    state_ref[...] = state


def _gla(pd, w_lr, b_lr, norm_g, batch, seq):
    ts = min(MIX_TS, seq)
    nst = seq // ts
    kw = N_HEADS * GLA_DK
    wlr = jnp.zeros((LANE, kw), F32).at[0:GLA_RANK, :].set(w_lr.astype(F32))
    bd = _block_diag_ones(GROUP_WIDTH, HEAD_DIM)
    iv = np.arange(GROUP_WIDTH)[:, None] // HEAD_DIM
    ik = np.arange(kw)[None, :] // GLA_DK
    bdt = jnp.asarray(iv == ik, F32)
    lt = jnp.asarray(np.tril(np.ones((CHUNK, CHUNK), np.float32)))
    consts = [wlr, b_lr.astype(F32)[None, :], jnp.tile(norm_g.astype(F32), N_HEADS)[None, :], bd, bdt, lt]
    full = lambda a: pl.BlockSpec(a.shape, lambda b, s: (0, 0))
    return pl.pallas_call(
        functools.partial(_gla_kernel, ts=ts),
        name="gla",
        grid=(batch, nst),
        in_specs=[pl.BlockSpec((ts, D_W), lambda b, s: (b * nst + s, 0))] + [full(a) for a in consts],
        out_specs=pl.BlockSpec((ts, GROUP_WIDTH), lambda b, s: (b * nst + s, 0)),
        out_shape=jax.ShapeDtypeStruct((batch * seq, GROUP_WIDTH), BF16),
        scratch_shapes=[pltpu.VMEM((GROUP_WIDTH, kw), F32)],
        compiler_params=_cparams(("parallel", "arbitrary")),
    )(pd, *consts)


def _mix_outproj(x_ref, ma_ref, mb_ref, mc_ref, md_ref, wo_ref):
    mix = jnp.concatenate([ma_ref[...], mb_ref[...], mc_ref[...], md_ref[...]], axis=1)
    return x_ref[...] + _dot(mix, wo_ref[...])


def _ffn_kernel(x_ref, ma_ref, mb_ref, mc_ref, md_ref, wo_ref, g_ref, wg_ref, wu_ref, wd_ref,
                o_ref, h_ref, acc_ref):
    f = pl.program_id(1)

    @pl.when(f == 0)
    def _first():
        x1 = _mix_outproj(x_ref, ma_ref, mb_ref, mc_ref, md_ref, wo_ref)
        acc_ref[...] = x1
        h_ref[...] = _rms(x1, g_ref[...]).astype(BF16)

    h = h_ref[...]
    gt = _dot(h, wg_ref[...])
    act = (gt * jax.nn.sigmoid(gt) * _dot(h, wu_ref[...])).astype(BF16)
    acc_ref[...] += _dot(act, wd_ref[...])

    @pl.when(f == pl.num_programs(1) - 1)
    def _last():
        o_ref[...] = acc_ref[...]


def _outproj_ffn(x, mixes, w_out, g, w_gate, w_up, w_down):
    t = x.shape[0]
    tm = min(FFN_TM, t)
    tf = FFN_TF
    nf = D_FF // tf
    row = lambda w: pl.BlockSpec((tm, w), lambda i, f: (i, 0))
    return pl.pallas_call(
        _ffn_kernel,
        name="outproj_ffn",
        grid=(t // tm, nf),
        in_specs=[row(D_MODEL)] + [row(GROUP_WIDTH)] * 4 + [
            pl.BlockSpec((D_MODEL, D_MODEL), lambda i, f: (0, 0)),
            pl.BlockSpec((1, D_MODEL), lambda i, f: (0, 0)),
            pl.BlockSpec((D_MODEL, tf), lambda i, f: (0, f)),
            pl.BlockSpec((D_MODEL, tf), lambda i, f: (0, f)),
            pl.BlockSpec((tf, D_MODEL), lambda i, f: (f, 0))],
        out_specs=row(D_MODEL),
        out_shape=jax.ShapeDtypeStruct((t, D_MODEL), F32),
        scratch_shapes=[pltpu.VMEM((tm, D_MODEL), BF16), pltpu.VMEM((tm, D_MODEL), F32)],
        compiler_params=_cparams(("parallel", "arbitrary")),
    )(x, *mixes, w_out, g, w_gate, w_up, w_down)


def _router_kernel(x_ref, ma_ref, mb_ref, mc_ref, md_ref, wo_ref, g_ref, wr_ref, x1_ref, h_ref, r_ref):
    x1 = _mix_outproj(x_ref, ma_ref, mb_ref, mc_ref, md_ref, wo_ref)
    x1_ref[...] = x1
    h = _rms(x1, g_ref[...])
    h_ref[...] = h.astype(BF16)
    lane = lax.broadcasted_iota(jnp.int32, (x1.shape[0], LANE), 1)
    logits = jnp.where(lane < N_EXPERTS, _dot(h, wr_ref[...], HI), NEG)
    m1 = jnp.max(logits, axis=1, keepdims=True)
    e1 = jnp.min(jnp.where(logits == m1, lane, LANE), axis=1, keepdims=True)
    rest = jnp.where(lane == e1, NEG, logits)
    m2 = jnp.max(rest, axis=1, keepdims=True)
    e2 = jnp.min(jnp.where(rest == m2, lane, LANE), axis=1, keepdims=True)
    ex = jnp.exp(m2 - m1)
    w1 = 1.0 / (1.0 + ex)
    w2 = ex / (1.0 + ex)
    r_ref[...] = jnp.where(lane == 0, e1.astype(F32),
                           jnp.where(lane == 1, e2.astype(F32),
                                     jnp.where(lane == 2, w1, jnp.where(lane == 3, w2, 0.0))))


def _outproj_router(x, mixes, w_out, g, w_router):
    t = x.shape[0]
    tm = min(FFN_TM, t)
    wr = jnp.zeros((D_MODEL, LANE), F32).at[:, 0:N_EXPERTS].set(w_router.astype(F32))
    row = lambda w: pl.BlockSpec((tm, w), lambda i: (i, 0))
    full = lambda a: pl.BlockSpec(a.shape, lambda i: (0, 0))
    return pl.pallas_call(
        _router_kernel,
        name="outproj_router",
        grid=(t // tm,),
        in_specs=[row(D_MODEL)] + [row(GROUP_WIDTH)] * 4 + [full(w_out), full(g), full(wr)],
        out_specs=[row(D_MODEL), row(D_MODEL), row(LANE)],
        out_shape=[jax.ShapeDtypeStruct((t, D_MODEL), F32), jax.ShapeDtypeStruct((t, D_MODEL), BF16),
                   jax.ShapeDtypeStruct((t, LANE), F32)],
        compiler_params=_cparams(("parallel",)),
    )(x, *mixes, w_out, g, wr)


def _moe_kernel(te_ref, na_ref, x_ref, sw_ref, wg_ref, wu_ref, wd_ref, o_ref, acc_ref):
    i = pl.program_id(0)
    f = pl.program_id(1)

    @pl.when(i < na_ref[0])
    def _active():
        @pl.when(f == 0)
        def _zero():
            acc_ref[...] = jnp.zeros_like(acc_ref)

        x = x_ref[...]
        gt = _dot(x, wg_ref[...].astype(BF16))
        act = (gt * jax.nn.sigmoid(gt) * _dot(x, wu_ref[...].astype(BF16))).astype(BF16)
        acc_ref[...] += _dot(act, wd_ref[...].astype(BF16))

        @pl.when(f == pl.num_programs(1) - 1)
        def _last():
            o_ref[...] = acc_ref[...] * sw_ref[...]


def _moe_experts(xs, slot_w, tile_e, n_active, w_gate, w_up, w_down, n_tiles):
    tm, tf = MOE_TM, MOE_TF
    nf = D_FF_EXPERT // tf

    def tile(i, na):
        return jnp.minimum(i, na[0] - 1)

    def fidx(i, f, na):
        return jnp.where(i < na[0], f, nf - 1)

    grid_spec = pltpu.PrefetchScalarGridSpec(
        num_scalar_prefetch=2,
        grid=(n_tiles, nf),
        in_specs=[
            pl.BlockSpec((tm, D_MODEL), lambda i, f, te, na: (tile(i, na), 0)),
            pl.BlockSpec((tm, 1), lambda i, f, te, na: (tile(i, na), 0)),
            pl.BlockSpec((None, D_MODEL, tf), lambda i, f, te, na: (te[tile(i, na)], 0, fidx(i, f, na))),
            pl.BlockSpec((None, D_MODEL, tf), lambda i, f, te, na: (te[tile(i, na)], 0, fidx(i, f, na))),
            pl.BlockSpec((None, tf, D_MODEL), lambda i, f, te, na: (te[tile(i, na)], fidx(i, f, na), 0)),
        ],
        out_specs=pl.BlockSpec((tm, D_MODEL), lambda i, f, te, na: (tile(i, na), 0)),
        scratch_shapes=[pltpu.VMEM((tm, D_MODEL), F32)],
    )
    return pl.pallas_call(
        _moe_kernel,
        name="moe_experts",
        grid_spec=grid_spec,
        out_shape=jax.ShapeDtypeStruct((n_tiles * tm, D_MODEL), F32),
        compiler_params=_cparams(("arbitrary", "arbitrary")),
    )(tile_e, n_active, xs, slot_w, w_gate, w_up, w_down)


def _route(route, t):
    tm = MOE_TM
    top_e = route[:, 0:TOP_K].astype(jnp.int32)
    top_w = route[:, TOP_K:2 * TOP_K]
    n_assign = t * TOP_K
    flat_e = top_e.reshape(n_assign)
    onehot = (flat_e[:, None] == jnp.arange(N_EXPERTS)[None, :]).astype(jnp.int32)
    rank = jnp.take_along_axis(jnp.cumsum(onehot, axis=0), flat_e[:, None], axis=1)[:, 0] - 1
    counts = jnp.sum(onehot, axis=0)
    padded = (counts + tm - 1) // tm * tm
    pad_end = jnp.cumsum(padded)
    pad_start = pad_end - padded
    dest = pad_start[flat_e] + rank
    n_tiles = -(-n_assign // tm) + N_EXPERTS
    n_slots = n_tiles * tm
    slot_tok = jnp.zeros((n_slots,), jnp.int32).at[dest].set(jnp.arange(n_assign, dtype=jnp.int32) // TOP_K)
    slot_w = jnp.zeros((n_slots,), F32).at[dest].set(top_w.reshape(n_assign))
    tile_e = jnp.minimum(jnp.searchsorted(pad_end, jnp.arange(n_tiles) * tm, side='right'),
                         N_EXPERTS - 1).astype(jnp.int32)
    n_active = (pad_end[-1] // tm).astype(jnp.int32).reshape(1)
    return slot_tok, slot_w, tile_e, n_active, dest.reshape(t, TOP_K), n_tiles


def _combine_kernel(x_ref, y0_ref, y1_ref, g_ref, o_ref):
    o_ref[...] = _rms(x_ref[...] + (y0_ref[...] + y1_ref[...]), g_ref[...])


def _combine_final(x1, y0, y1, g):
    t = x1.shape[0]
    tm = min(FFN_TM, t)
    row = pl.BlockSpec((tm, D_MODEL), lambda i: (i, 0))
    return pl.pallas_call(
        _combine_kernel,
        name="combine_final",
        grid=(t // tm,),
        in_specs=[row, row, row, pl.BlockSpec((1, D_MODEL), lambda i: (0, 0))],
        out_specs=row,
        out_shape=jax.ShapeDtypeStruct((t, D_MODEL), F32),
        compiler_params=_cparams(("parallel",)),
    )(x1, y0, y1, g)


def _final_norm_kernel(x_ref, g_ref, o_ref):
    o_ref[...] = _rms(x_ref[...], g_ref[...])


def _split_w_in(w):
    def pad(a, width):
        return jnp.pad(a, ((0, 0), (0, width - a.shape[1])))
    a_end = 4 * GROUP_WIDTH + 2 * N_HEADS
    b_end = a_end + 3 * GROUP_WIDTH
    c_end = b_end + 2 * GROUP_WIDTH
    wa = pad(w[:, 0:a_end], A_W)
    wb = w[:, a_end:b_end]
    wc = w[:, b_end:c_end]
    wd = pad(w[:, c_end:], D_W)
    return tuple(m.astype(BF16) for m in (wa, wb, wc, wd))


def kernel(x, norm_mix, w_in, a_conv, a_A_log, a_dt_bias, a_norm, b_lambda, b_norm, rel_bias,
           c_conv_w, c_conv_b, c_w_a, c_b_a, c_w_x, c_b_x, c_a_param, d_w_lr, d_b_lr, d_norm,
           w_out, norm_ffn, ffn_w_gate, ffn_w_up, ffn_w_down, moe_router, moe_w_gate, moe_w_up,
           moe_w_down, norm_final):
    batch, seq, _ = x.shape
    depth = w_in.shape[0]
    t = batch * seq
    xt = x.reshape(t, D_MODEL).astype(F32)
    bias_tiles = _attn_bias_tiles(rel_bias)
    row = lambda p: p.astype(F32)[None, :]
    out = None
    for l in range(depth):
        pa, pb, pc, pd = _inproj(xt, row(norm_mix[l]), *_split_w_in(w_in[l]))
        lam_init = 0.8 - 0.6 * math.exp(-0.3 * l)
        mixes = (
            _gdn(pa, a_conv[l], a_A_log[l], a_dt_bias[l], a_norm[l], batch, seq),
            _diff_attn(pb, b_lambda[l], lam_init, bias_tiles, b_norm[l], batch, seq),
            _rglru(pc, c_conv_w[l], c_conv_b[l], c_w_a[l], c_b_a[l], c_w_x[l], c_b_x[l],
                   c_a_param[l], batch, seq),
            _gla(pd, d_w_lr[l], d_b_lr[l], d_norm[l], batch, seq),
        )
        wo = w_out[l].astype(BF16)
        if l % 2 == 0:
            j = l // 2
            xt = _outproj_ffn(xt, mixes, wo, row(norm_ffn[l]), ffn_w_gate[j].astype(BF16),
                              ffn_w_up[j].astype(BF16), ffn_w_down[j].astype(BF16))
            out = None
        else:
            j = l // 2
            x1, h, route = _outproj_router(xt, mixes, wo, row(norm_ffn[l]), moe_router[j])
            slot_tok, slot_w, tile_e, n_active, dest, n_tiles = _route(route, t)
            xs = jnp.take(h, slot_tok, axis=0)
            ys = _moe_experts(xs, slot_w[:, None], tile_e, n_active, moe_w_gate[j], moe_w_up[j],
                              moe_w_down[j], n_tiles)
            y0 = jnp.take(ys, dest[:, 0], axis=0)
            y1 = jnp.take(ys, dest[:, 1], axis=0)
            if l == depth - 1:
                out = _combine_final(x1, y0, y1, row(norm_final))
            else:
                xt = x1 + (y0 + y1)
    if out is None:
        tm = min(FFN_TM, t)
        rowspec = pl.BlockSpec((tm, D_MODEL), lambda i: (i, 0))
        out = pl.pallas_call(
            _final_norm_kernel, name="final_norm", grid=(t // tm,),
            in_specs=[rowspec, pl.BlockSpec((1, D_MODEL), lambda i: (0, 0))],
            out_specs=rowspec, out_shape=jax.ShapeDtypeStruct((t, D_MODEL), F32),
            compiler_params=_cparams(("parallel",)),
        )(xt, row(norm_final))
    return out.reshape(batch, seq, D_MODEL).astype(x.dtype)
```

```python
import functools
import math

import jax
import jax.numpy as jnp
import numpy as np
from jax import lax
from jax.experimental import pallas as pl
from jax.experimental.pallas import tpu as pltpu

D_MODEL = 1024
CHUNK = 64
N_HEADS = 4
HEAD_DIM = 64
GROUP_WIDTH = 256
CONV_WIDTH = 4
DIFF_DH = 32
Q_BLOCK = 128
REL_BUCKETS = 32
REL_MAX_DIST = 128
RG_C = 8.0
GLA_DK = 32
GLA_RANK = 16
GLA_TAU = 16.0
D_FF = 2816
N_EXPERTS = 8
TOP_K = 2
D_FF_EXPERT = 3584
EPS = 1e-6
assert CHUNK == HEAD_DIM

LANE = 128
SUBLANE = 8
VMEM_LIMIT = 56 * 1024 * 1024

F32 = jnp.float32
BF16 = jnp.bfloat16
HI = lax.Precision.HIGHEST
NEG = -1e30
LOG2E = math.log2(math.e)

A_W = 1152
B_W = 768
C_W = 512
D_W = 896

PROJ_TM = 512
MIX_TS = 256
GDN_BLK = 16
GDN_NB = 2
ATT_T = 128
ATT_VROWS = 80
ATT_UNROLL = 4
FFN_TM = 512
FFN_TF = 1408
MOE_TM = 1024
MOE_TF = 512


def _cparams(sem):
    return pltpu.CompilerParams(dimension_semantics=sem, vmem_limit_bytes=VMEM_LIMIT)


def _dot(a, b, precision=None):
    return jnp.dot(a, b, preferred_element_type=F32, precision=precision)


def _dot_nt(a, b, precision=None):
    return lax.dot_general(a, b, (((1,), (1,)), ((), ())), preferred_element_type=F32,
                           precision=precision)


def _dot_tn(a, b, precision=None):
    return lax.dot_general(a, b, (((0,), (0,)), ((), ())), preferred_element_type=F32,
                           precision=precision)


def _softplus(x):
    return jnp.maximum(x, 0.0) + jnp.log1p(jnp.exp(-jnp.abs(x)))


def _rms(x, g):
    return x * lax.rsqrt(jnp.mean(x * x, axis=-1, keepdims=True) + EPS) * g


def _causal_conv(x, tail, w):
    row = lax.broadcasted_iota(jnp.int32, (SUBLANE, x.shape[1]), 0)
    y = x * w[CONV_WIDTH - 1:CONV_WIDTH, :]
    for d in range(1, CONV_WIDTH):
        rolled = pltpu.roll(x, d, 0)
        first = jnp.where(row < d, pltpu.roll(tail, d, 0), rolled[:SUBLANE])
        shifted = jnp.concatenate([first, rolled[SUBLANE:]], axis=0)
        y = y + shifted * w[CONV_WIDTH - 1 - d:CONV_WIDTH - d, :]
    return y


def _head_mask(width, per_head):
    lane = lax.broadcasted_iota(jnp.int32, (1, width), 1)
    return [(lane // per_head) == h for h in range(N_HEADS)]


def _stack_heads(x, masks):
    return jnp.concatenate([jnp.where(m, x, 0.0) for m in masks], axis=0)


def _unstack_heads(r, masks, c):
    out = jnp.where(masks[0], r[0:c], 0.0)
    for h in range(1, N_HEADS):
        out = out + jnp.where(masks[h], r[h * c:(h + 1) * c], 0.0)
    return out


def _inproj_kernel(x_ref, g_ref, wa_ref, wb_ref, wc_ref, wd_ref, oa_ref, ob_ref, oc_ref, od_ref):
    h = _rms(x_ref[...], g_ref[...]).astype(BF16)
    oa_ref[...] = _dot(h, wa_ref[...])
    ob_ref[...] = _dot(h, wb_ref[...])
    oc_ref[...] = _dot(h, wc_ref[...])
    od_ref[...] = _dot(h, wd_ref[...])


def _inproj(x, g, wa, wb, wc, wd):
    t = x.shape[0]
    tm = min(PROJ_TM, t)
    row = lambda w: pl.BlockSpec((tm, w), lambda i: (i, 0))
    full = lambda a: pl.BlockSpec(a.shape, lambda i: (0, 0))
    return pl.pallas_call(
        _inproj_kernel,
        name="inproj",
        grid=(t // tm,),
        in_specs=[row(D_MODEL), full(g), full(wa), full(wb), full(wc), full(wd)],
        out_specs=[row(A_W), row(B_W), row(C_W), row(D_W)],
        out_shape=[jax.ShapeDtypeStruct((t, w), F32) for w in (A_W, B_W, C_W, D_W)],
        compiler_params=_cparams(("parallel",)),
    )(x, g, wa, wb, wc, wd)


def _split2(x):
    hi = x.astype(BF16)
    return hi, (x - hi.astype(F32)).astype(BF16)


def _dot_x2(x, w):
    hi, lo = _split2(x)
    return _dot(hi, w) + _dot(lo, w)


def _bd_tile(x, bd):
    return jnp.concatenate([x] * N_HEADS, axis=0) * bd


def _dot_bd(a, bs, bd):
    ab = a.astype(BF16)
    return [_dot(ab, _bd_tile(b.astype(BF16), bd)) for b in bs]


def _gdn_kernel(a_ref, convw_ref, alog_ref, dtb_ref, gn_ref, bd_ref, lt_ref, e_ref,
                o_ref, tail_ref, state_ref, *, ts, nb):
    @pl.when(pl.program_id(1) == 0)
    def _init():
        tail_ref[...] = jnp.zeros_like(tail_ref)
        state_ref[...] = jnp.zeros_like(state_ref)

    c = CHUNK
    bd = bd_ref[...]
    lt = lt_ref[...]
    q, k, v, beta, gcum = [], [], [], [], []
    for b in range(nb):
        xin = a_ref[b, :, 0:768]
        y = _causal_conv(xin, tail_ref[b], convw_ref[...])
        tail_ref[b] = xin[ts - SUBLANE:ts, :]
        y = y * jax.nn.sigmoid(y)
        qb, kb_ = y[:, 0:256], y[:, 256:512]
        q.append(qb * lax.rsqrt(_dot_x2(qb * qb, bd) + EPS) * (HEAD_DIM ** -0.5))
        k.append(kb_ * lax.rsqrt(_dot_x2(kb_ * kb_, bd) + EPS))
        v.append(y[:, 512:768])
        e = _dot_x2(a_ref[b, :, 1024:1152], e_ref[...])
        beta.append(jax.nn.sigmoid(e[:, 0:256]))
        g = -jnp.exp(alog_ref[...]) * _softplus(e[:, 256:512] + dtb_ref[...])
        g_hi, g_lo = _split2(g)
        g_lo2 = (g - g_hi.astype(F32) - g_lo.astype(F32)).astype(BF16)
        gcum.append(_dot(lt, g_hi) + (_dot(lt, g_lo) + _dot(lt, g_lo2)))

    ri = lax.broadcasted_iota(jnp.int32, (c, GROUP_WIDTH), 0)
    cj = lax.broadcasted_iota(jnp.int32, (c, GROUP_WIDTH), 1) % HEAD_DIM
    causal = ri >= cj
    strict = ri > cj
    diag = ri == cj
    same_blk = (ri // GDN_BLK) == (cj // GDN_BLK)
    eye = diag.astype(F32)
    masks = _head_mask(GROUP_WIDTH, HEAD_DIM)
    items = [(b, slice(ci * c, (ci + 1) * c)) for ci in range(ts // c) for b in range(nb)]
    idx = range(len(items))

    a_qk, p0, nn, rhs_u, rhs_w = [], [], [], [], []
    for b, sl in items:
        kc, gc = k[b][sl], gcum[b][sl]
        kb = kc * beta[b][sl]
        grow = jnp.sum(jnp.where(diag, gc, 0.0), axis=0, keepdims=True)
        gamma = jnp.exp(jnp.where(causal, gc - grow, NEG))
        kst = _stack_heads(kc, masks).astype(BF16)
        aa = _dot_nt(jnp.concatenate([kb, q[b][sl]], axis=0).astype(BF16), kst)
        a_kk = jnp.where(strict, aa[0:c] * gamma, 0.0)
        a_qk.append(aa[c:2 * c] * gamma)
        p0.append(jnp.where(same_blk, -a_kk, 0.0))
        nn.append(jnp.where(same_blk, 0.0, a_kk))
        rhs_u.append(v[b][sl] * beta[b][sl])
        rhs_w.append(kb * jnp.exp(gc))
    t1 = [eye + p for p in p0]
    p1 = [_dot_bd(p0[i], [p0[i]], bd)[0] for i in idx]
    pr = [_dot_bd(p1[i], [p1[i], t1[i]], bd) for i in idx]
    p2 = [x[0] for x in pr]
    t2 = [t1[i] + pr[i][1] for i in idx]
    pr = [_dot_bd(p2[i], [p2[i], t2[i]], bd) for i in idx]
    p3 = [x[0] for x in pr]
    t3 = [t2[i] + pr[i][1] for i in idx]
    dinv = [t3[i] + _dot_bd(p3[i], [t3[i]], bd)[0] for i in idx]
    m1 = [_dot_bd(dinv[i], [nn[i]], bd)[0] for i in idx]
    m2 = [_dot_bd(m1[i], [m1[i]], bd)[0] for i in idx]
    im = [eye - m for m in m1]
    qq = [im[i] + _dot_bd(im[i], [m2[i]], bd)[0] for i in idx]
    inv = [_dot_bd(qq[i], [dinv[i]], bd)[0] for i in idx]
    uw = [_dot_bd(inv[i], [rhs_u[i], rhs_w[i]], bd) for i in idx]

    bdf = bd.astype(F32)
    state = [state_ref[b] for b in range(nb)]
    outs = [[] for _ in range(nb)]
    for i, (b, sl) in enumerate(items):
        gc = gcum[b][sl]
        g_last = gc[c - 1:c, :]
        u, w = uw[i]
        ws_qs = _dot(jnp.concatenate([w, q[b][sl] * jnp.exp(gc)], axis=0).astype(BF16), state[b].astype(BF16))
        v_new = (u - ws_qs[0:c]).astype(BF16)
        outs[b].append(ws_qs[c:2 * c] + _dot(a_qk[i].astype(BF16), _bd_tile(v_new, bd)))
        kd = k[b][sl] * jnp.exp(g_last - gc)
        state[b] = state[b] * jnp.exp(g_last) + _dot_tn(kd.astype(BF16), v_new) * bdf
    for b in range(nb):
        state_ref[b] = state[b]
        o = jnp.concatenate(outs[b], axis=0)
        o = o * lax.rsqrt(_dot_x2(o * o, bd) * (1.0 / HEAD_DIM) + EPS) * gn_ref[...]
        gate = a_ref[b, :, 768:1024]
        o_ref[b] = (o * (gate * jax.nn.sigmoid(gate))).astype(o_ref.dtype)


def _block_diag_ones(n, blk):
    i = np.arange(n)
    return jnp.asarray((i[:, None] // blk) == (i[None, :] // blk), F32)


def _gdn(pa, conv_w, a_log, dt_bias, norm_g, batch, seq):
    ts = min(MIX_TS, seq)
    nst = seq // ts
    bd = _block_diag_ones(GROUP_WIDTH, HEAD_DIM).astype(BF16)
    r = np.arange(ts)
    lt = jnp.asarray((r[:, None] // CHUNK == r[None, :] // CHUNK) & (r[:, None] >= r[None, :]), BF16)
    lane = np.arange(GROUP_WIDTH)
    e = np.zeros((LANE, 2 * GROUP_WIDTH), np.float32)
    for h in range(N_HEADS):
        e[h, np.nonzero(lane // HEAD_DIM == h)[0]] = 1.0
        e[N_HEADS + h, GROUP_WIDTH + np.nonzero(lane // HEAD_DIM == h)[0]] = 1.0
    rep = lambda p: jnp.repeat(p.astype(F32), HEAD_DIM)[None, :]
    consts = [conv_w.astype(F32), rep(a_log), rep(dt_bias),
              jnp.tile(norm_g.astype(F32), N_HEADS)[None, :], bd, lt, jnp.asarray(e, BF16)]
    nb = GDN_NB if batch % GDN_NB == 0 else 1
    full = lambda a: pl.BlockSpec(a.shape, lambda b, s: (0, 0))
    out = pl.pallas_call(
        functools.partial(_gdn_kernel, ts=ts, nb=nb),
        name="gdn",
        grid=(batch // nb, nst),
        in_specs=[pl.BlockSpec((nb, ts, A_W), lambda b, s: (b, s, 0))] + [full(a) for a in consts],
        out_specs=pl.BlockSpec((nb, ts, GROUP_WIDTH), lambda b, s: (b, s, 0)),
        out_shape=jax.ShapeDtypeStruct((batch, seq, GROUP_WIDTH), BF16),
        scratch_shapes=[pltpu.VMEM((nb, SUBLANE, 768), F32),
                        pltpu.VMEM((nb, GROUP_WIDTH, GROUP_WIDTH), F32)],
        compiler_params=_cparams(("parallel", "arbitrary")),
    )(pa.reshape(batch, seq, A_W), *consts)
    return out.reshape(batch * seq, GROUP_WIDTH)


def _attn_kernel(q_ref, k_ref, v_ref, bias_ref, lamv_ref, gn_ref, o_ref,
                 kb_ref, va_ref, qt_ref, s_ref, p_ref, al_ref, m_ref, l_ref, acc_ref, *, lam_init, seq):
    t = ATT_T
    qi = pl.program_id(1)
    n = qi + 1
    nlan = 2 * N_HEADS * t

    @pl.when(qi == 0)
    def _stage_kv():
        def body(j, carry):
            rows = pl.ds(pl.multiple_of(j * t, t), t)
            kb_ref[rows, :] = k_ref[rows, :].astype(BF16)
            vt = v_ref[rows, :].T.astype(BF16)
            for h in range(N_HEADS):
                va_ref[j, h, 0:HEAD_DIM, :] = vt[h * HEAD_DIM:(h + 1) * HEAD_DIM, :]
                va_ref[j, h, HEAD_DIM:ATT_VROWS, :] = jnp.ones((ATT_VROWS - HEAD_DIM, t), BF16)
            return carry
        lax.fori_loop(0, seq // t, body, 0)

    qt = (q_ref[...] * (DIFF_DH ** -0.5 * LOG2E)).T
    feat = lax.broadcasted_iota(jnp.int32, (GROUP_WIDTH, t), 0) // DIFF_DH
    for idx in range(2 * N_HEADS):
        qt_ref[:, idx * t:(idx + 1) * t] = jnp.where(feat == idx, qt, 0.0).astype(BF16)
    m_ref[...] = jnp.full((1, nlan), NEG, F32)
    l_ref[...] = jnp.zeros((1, nlan), F32)
    acc_ref[...] = jnp.zeros((HEAD_DIM, nlan), F32)
    p_ref[1] = jnp.zeros((t, nlan), BF16)
    al_ref[1] = jnp.ones((1, nlan), F32)

    head_cols = [slice(2 * h * t, (2 * h + 2) * t) for h in range(N_HEADS)]

    def scores(j, slot, h):
        rows = pl.ds(pl.multiple_of(j * t, t), t)
        s_ref[slot, :, head_cols[h]] = _dot(kb_ref[rows, :], qt_ref[:, head_cols[h]])

    def softmax(slot, h, bias_idx):
        cols = head_cols[h]
        s = s_ref[slot, :, cols]
        if bias_idx is not None:
            s = s + bias_ref[bias_idx, :, cols]
        m_prev = m_ref[:, cols]
        m_new = jnp.maximum(m_prev, jnp.max(s, axis=0, keepdims=True))
        al_ref[slot, :, cols] = jnp.exp2(m_prev - m_new)
        m_ref[:, cols] = m_new
        p_ref[slot, :, cols] = jnp.exp2(s - m_new).astype(BF16)

    def values(j, slot, h):
        cols = head_cols[h]
        alpha = al_ref[slot, :, cols]
        pv = _dot(va_ref[jnp.maximum(j, 0), h], p_ref[slot, :, cols])
        acc_ref[:, cols] = alpha * acc_ref[:, cols] + pv[0:HEAD_DIM]
        l_ref[:, cols] = alpha * l_ref[:, cols] + pv[HEAD_DIM:HEAD_DIM + 1]

    def pipe_step(i, slot, bias_idx):
        for h in range(N_HEADS):
            scores(i, slot, h)
            softmax(1 - slot, h, bias_idx)
            values(i - 2, slot, h)

    for h in range(N_HEADS):
        scores(0, 0, h)

    n_far_steps = jnp.maximum(n - 2, 0)

    def unrolled_body(k, carry):
        for u in range(ATT_UNROLL):
            pipe_step(ATT_UNROLL * k + 1 + u, (1 + u) % 2, None)
        return carry
    lax.fori_loop(0, n_far_steps // ATT_UNROLL, unrolled_body, 0)

    i_rem = 1 + (n_far_steps // ATT_UNROLL) * ATT_UNROLL
    for u in range(ATT_UNROLL - 1):
        @pl.when(n_far_steps % ATT_UNROLL > u)
        def _rem_step(u=u):
            pipe_step(i_rem + u, (1 + u) % 2, None)

    def finish(last):
        for h in range(N_HEADS):
            softmax(last, h, 2)
            values(n - 2, 1 - last, h)
        for h in range(N_HEADS):
            values(n - 1, last, h)

    @pl.when(n == 1)
    def _only_diag():
        finish(0)

    for last in range(2):
        @pl.when(jnp.logical_and(n >= 2, (n - 1) % 2 == last))
        def _near_and_diag(last=last):
            pipe_step(n - 1, last, 1)
            finish(last)

    lv = lamv_ref[...]
    lam = (jnp.exp(jnp.sum(lv[0:1] * lv[1:2], axis=1, keepdims=True))
           - jnp.exp(jnp.sum(lv[2:3] * lv[3:4], axis=1, keepdims=True)) + lam_init)
    inv_l = 1.0 / l_ref[...]
    outs = []
    for h in range(N_HEADS):
        c0 = slice(2 * h * t, (2 * h + 1) * t)
        c1 = slice((2 * h + 1) * t, (2 * h + 2) * t)
        oh = acc_ref[:, c0] * inv_l[:, c0] - lam * (acc_ref[:, c1] * inv_l[:, c1])
        oh = oh * lax.rsqrt(jnp.mean(oh * oh, axis=0, keepdims=True) + EPS)
        outs.append(oh)
    o = jnp.concatenate(outs, axis=0).T
    o_ref[...] = (o * gn_ref[...] * (1.0 - lam_init)).astype(o_ref.dtype)


def _t5_bucket(rel):
    nb = REL_BUCKETS // 2
    bucket = jnp.where(rel > 0, nb, 0)
    n = jnp.abs(rel)
    max_exact = nb // 2
    large = max_exact + (jnp.log(jnp.maximum(n, 1).astype(F32) / max_exact)
                         / math.log(REL_MAX_DIST / max_exact) * (nb - max_exact)).astype(jnp.int32)
    large = jnp.minimum(large, nb - 1)
    return bucket + jnp.where(n < max_exact, n, large)


def _attn_bias_tiles(rel_bias):
    t = ATT_T
    table = rel_bias.astype(F32)
    kk = jnp.arange(t)[:, None]
    qq = jnp.arange(t)[None, :]

    def expand(b):
        b = jnp.transpose(b, (0, 2, 1))
        b = jnp.broadcast_to(b[:, :, None, :], (t, N_HEADS, 2, t))
        return b.reshape(t, 2 * N_HEADS * t)

    table = table * LOG2E
    diag = table[_t5_bucket(kk - qq)]
    diag = jnp.where(((kk // CHUNK) <= (qq // CHUNK))[:, :, None], diag, NEG)
    near = table[_t5_bucket(kk - qq - t)]
    far = table[_t5_bucket(jnp.full((1, 1), -(REL_MAX_DIST + 1), jnp.int32))]
    far = jnp.broadcast_to(far, (t, t, N_HEADS))
    return jnp.stack([jnp.zeros((t, 2 * N_HEADS * t), F32), expand(near - far), expand(diag - far)])


def _diff_attn(pb, lam_vecs, lam_init, bias_tiles, norm_g, batch, seq):
    t = ATT_T
    nq = seq // t
    gn = jnp.tile(norm_g.astype(F32), N_HEADS)[None, :]
    lamv = lam_vecs.astype(F32)
    nlan = 2 * N_HEADS * t
    full = lambda a: pl.BlockSpec(a.shape, lambda b, i: (0,) * a.ndim)
    return pl.pallas_call(
        functools.partial(_attn_kernel, lam_init=lam_init, seq=seq),
        name="diffattn",
        grid=(batch, nq),
        in_specs=[pl.BlockSpec((t, GROUP_WIDTH), lambda b, i: (b * nq + i, 0)),
                  pl.BlockSpec((seq, GROUP_WIDTH), lambda b, i: (b, 1)),
                  pl.BlockSpec((seq, GROUP_WIDTH), lambda b, i: (b, 2)),
                  full(bias_tiles), full(lamv), full(gn)],
        out_specs=pl.BlockSpec((t, GROUP_WIDTH), lambda b, i: (b * nq + i, 0)),
        out_shape=jax.ShapeDtypeStruct((batch * seq, GROUP_WIDTH), BF16),
        scratch_shapes=[pltpu.VMEM((seq, GROUP_WIDTH), BF16),
                        pltpu.VMEM((seq // t, N_HEADS, ATT_VROWS, t), BF16),
                        pltpu.VMEM((GROUP_WIDTH, nlan), BF16),
                        pltpu.VMEM((2, t, nlan), F32),
                        pltpu.VMEM((2, t, nlan), BF16),
                        pltpu.VMEM((2, 1, nlan), F32),
                        pltpu.VMEM((1, nlan), F32), pltpu.VMEM((1, nlan), F32),
                        pltpu.VMEM((HEAD_DIM, nlan), F32)],
        compiler_params=_cparams(("parallel", "arbitrary")),
    )(pb, pb, pb, bias_tiles, lamv, gn)


def _rglru_kernel(c_ref, convw_ref, convb_ref, wa_ref, ba_ref, wx_ref, bx_ref, ap_ref, o_ref,
                  tail_ref, h_ref, *, ts):
    @pl.when(pl.program_id(1) == 0)
    def _init():
        tail_ref[...] = jnp.zeros_like(tail_ref)
        h_ref[...] = jnp.zeros_like(h_ref)

    xb = c_ref[:, 0:256]
    gb = c_ref[:, 256:512]
    xc = _causal_conv(xb, tail_ref[...], convw_ref[...]) + convb_ref[...]
    tail_ref[...] = xb[ts - SUBLANE:ts, :]
    xcb = xc.astype(BF16)
    gate_a = jax.nn.sigmoid(_dot(xcb, wa_ref[...]) + ba_ref[...])
    gate_x = jax.nn.sigmoid(_dot(xcb, wx_ref[...]) + bx_ref[...])
    log_a = -RG_C * gate_a * _softplus(ap_ref[...])
    a = jnp.exp(log_a)
    th = jnp.tanh(log_a)
    u = xc * gate_x * jnp.sqrt(-2.0 * th / (1.0 - th))
    row = lax.broadcasted_iota(jnp.int32, (ts, GROUP_WIDTH), 0)
    d = 1
    while d < ts:
        keep = row >= d
        a_sh = jnp.where(keep, pltpu.roll(a, d, 0), 1.0)
        u_sh = jnp.where(keep, pltpu.roll(u, d, 0), 0.0)
        u = u + a * u_sh
        a = a * a_sh
        d *= 2
    h = u + a * h_ref[...]
    h_ref[...] = h[ts - 1:ts, :]
    gelu = 0.5 * gb * (1.0 + jnp.tanh(math.sqrt(2.0 / math.pi) * (gb + 0.044715 * (gb * gb * gb))))
    o_ref[...] = (h * gelu).astype(o_ref.dtype)


def _block_diag_weight(w):
    nb, wi, wo = w.shape
    out = jnp.zeros((nb * wi, nb * wo), w.dtype)
    for i in range(nb):
        out = out.at[i * wi:(i + 1) * wi, i * wo:(i + 1) * wo].set(w[i])
    return out


def _rglru(pc, conv_w, conv_b, w_a, b_a, w_x, b_x, a_param, batch, seq):
    ts = min(MIX_TS, seq)
    nst = seq // ts
    r = lambda p: p.astype(F32)[None, :]
    consts = [conv_w.astype(F32), r(conv_b), _block_diag_weight(w_a).astype(BF16), r(b_a),
              _block_diag_weight(w_x).astype(BF16), r(b_x), r(a_param)]
    full = lambda a: pl.BlockSpec(a.shape, lambda b, s: (0, 0))
    return pl.pallas_call(
        functools.partial(_rglru_kernel, ts=ts),
        name="rglru",
        grid=(batch, nst),
        in_specs=[pl.BlockSpec((ts, C_W), lambda b, s: (b * nst + s, 0))] + [full(a) for a in consts],
        out_specs=pl.BlockSpec((ts, GROUP_WIDTH), lambda b, s: (b * nst + s, 0)),
        out_shape=jax.ShapeDtypeStruct((batch * seq, GROUP_WIDTH), BF16),
        scratch_shapes=[pltpu.VMEM((SUBLANE, GROUP_WIDTH), F32), pltpu.VMEM((1, GROUP_WIDTH), F32)],
        compiler_params=_cparams(("parallel", "arbitrary")),
    )(pc, *consts)


def _gla_kernel(d_ref, wlr_ref, blr_ref, gn_ref, bd_ref, bdt_ref, lt_ref, o_ref, state_ref, *, ts, nb):
    @pl.when(pl.program_id(1) == 0)
    def _init():
        state_ref[...] = jnp.zeros_like(state_ref)

    c = CHUNK
    bd = bd_ref[...]
    bdt = bdt_ref[...]
    lt = lt_ref[...]
    kmasks = _head_mask(N_HEADS * GLA_DK, GLA_DK)
    ri = lax.broadcasted_iota(jnp.int32, (c, GROUP_WIDTH), 0)
    cj = lax.broadcasted_iota(jnp.int32, (c, GROUP_WIDTH), 1) % HEAD_DIM
    causal = ri >= cj
    q, k, gcum = [], [], []
    for b in range(nb):
        q.append(d_ref[b, :, 0:128] * (GLA_DK ** -0.5))
        k.append(d_ref[b, :, 128:256])
        lr_hi, lr_lo = _split2(d_ref[b, :, 768:896])
        w_hi, w_lo = _split2(wlr_ref[...])
        z = _dot(lr_hi, w_hi) + (_dot(lr_hi, w_lo) + _dot(lr_lo, w_hi)) + blr_ref[...]
        la = (jnp.minimum(z, 0.0) - jnp.log1p(jnp.exp(-jnp.abs(z)))) * (1.0 / GLA_TAU)
        la_hi, la_lo = _split2(la)
        la_lo2 = (la - la_hi.astype(F32) - la_lo.astype(F32)).astype(BF16)
        gcum.append(_dot(lt, la_hi) + (_dot(lt, la_lo) + _dot(lt, la_lo2)))

    items = [(b, slice(ci * c, (ci + 1) * c)) for ci in range(ts // c) for b in range(nb)]
    o_intra, upd = [], []
    for b, sl in items:
        qc, kc, gc = q[b][sl], k[b][sl], gcum[b][sl]
        vc = d_ref[b, sl, 256:512].astype(BF16)
        ref = gc[c // 2:c // 2 + 1, :]
        kst = _stack_heads(kc * jnp.exp(ref - gc), kmasks).astype(BF16)
        a_in = _dot_nt((qc * jnp.exp(gc - ref)).astype(BF16), kst)
        a_in = jnp.where(causal, a_in, 0.0).astype(BF16)
        o_intra.append(_dot(a_in, _bd_tile(vc, bd)))
        kd = kc * jnp.exp(gc[c - 1:c, :] - gc)
        upd.append(_dot_tn(vc, kd.astype(BF16)) * bdt)

    state = [state_ref[b] for b in range(nb)]
    outs = [[] for _ in range(nb)]
    for i, (b, sl) in enumerate(items):
        gc = gcum[b][sl]
        o_inter = _dot_nt((q[b][sl] * jnp.exp(gc)).astype(BF16), state[b].astype(BF16))
        outs[b].append(o_intra[i] + o_inter)
        state[b] = state[b] * jnp.exp(gc[c - 1:c, :]) + upd[i]
    for b in range(nb):
        state_ref[b] = state[b]
        o = jnp.concatenate(outs[b], axis=0)
        o = o * lax.rsqrt(_dot_x2(o * o, bd) * (1.0 / HEAD_DIM) + EPS) * gn_ref[...]
        rt = d_ref[b, :, 512:768]
        o_ref[b] = (o * (rt * jax.nn.sigmoid(rt))).astype(o_ref.dtype)


def _gla(pd, w_lr, b_lr, norm_g, batch, seq):
    ts = min(MIX_TS, seq)
    nst = seq // ts
    kw = N_HEADS * GLA_DK
    wlr = jnp.zeros((LANE, kw), F32).at[0:GLA_RANK, :].set(w_lr.astype(F32))
    bd = _block_diag_ones(GROUP_WIDTH, HEAD_DIM).astype(BF16)
    iv = np.arange(GROUP_WIDTH)[:, None] // HEAD_DIM
    ik = np.arange(kw)[None, :] // GLA_DK
    bdt = jnp.asarray(iv == ik, F32)
    r = np.arange(ts)
    lt = jnp.asarray((r[:, None] // CHUNK == r[None, :] // CHUNK) & (r[:, None] >= r[None, :]), BF16)
    consts = [wlr, b_lr.astype(F32)[None, :], jnp.tile(norm_g.astype(F32), N_HEADS)[None, :], bd, bdt, lt]
    nb = GDN_NB if batch % GDN_NB == 0 else 1
    full = lambda a: pl.BlockSpec(a.shape, lambda b, s: (0, 0))
    out = pl.pallas_call(
        functools.partial(_gla_kernel, ts=ts, nb=nb),
        name="gla",
        grid=(batch // nb, nst),
        in_specs=[pl.BlockSpec((nb, ts, D_W), lambda b, s: (b, s, 0))] + [full(a) for a in consts],
        out_specs=pl.BlockSpec((nb, ts, GROUP_WIDTH), lambda b, s: (b, s, 0)),
        out_shape=jax.ShapeDtypeStruct((batch, seq, GROUP_WIDTH), BF16),
        scratch_shapes=[pltpu.VMEM((nb, GROUP_WIDTH, kw), F32)],
        compiler_params=_cparams(("parallel", "arbitrary")),
    )(pd.reshape(batch, seq, D_W), *consts)
    return out.reshape(batch * seq, GROUP_WIDTH)


def _mix_outproj(x_ref, ma_ref, mb_ref, mc_ref, md_ref, wo_ref):
    mix = jnp.concatenate([ma_ref[...], mb_ref[...], mc_ref[...], md_ref[...]], axis=1)
    return x_ref[...] + _dot(mix, wo_ref[...])


def _ffn_kernel(x_ref, ma_ref, mb_ref, mc_ref, md_ref, wo_ref, g_ref, wg_ref, wu_ref, wd_ref,
                o_ref, h_ref, acc_ref):
    f = pl.program_id(1)

    @pl.when(f == 0)
    def _first():
        x1 = _mix_outproj(x_ref, ma_ref, mb_ref, mc_ref, md_ref, wo_ref)
        acc_ref[...] = x1
        h_ref[...] = _rms(x1, g_ref[...]).astype(BF16)

    h = h_ref[...]
    gt = _dot(h, wg_ref[...])
    act = (gt * jax.nn.sigmoid(gt) * _dot(h, wu_ref[...])).astype(BF16)
    acc_ref[...] += _dot(act, wd_ref[...])

    @pl.when(f == pl.num_programs(1) - 1)
    def _last():
        o_ref[...] = acc_ref[...]


def _outproj_ffn(x, mixes, w_out, g, w_gate, w_up, w_down):
    t = x.shape[0]
    tm = min(FFN_TM, t)
    tf = FFN_TF
    nf = D_FF // tf
    row = lambda w: pl.BlockSpec((tm, w), lambda i, f: (i, 0))
    return pl.pallas_call(
        _ffn_kernel,
        name="outproj_ffn",
        grid=(t // tm, nf),
        in_specs=[row(D_MODEL)] + [row(GROUP_WIDTH)] * 4 + [
            pl.BlockSpec((D_MODEL, D_MODEL), lambda i, f: (0, 0)),
            pl.BlockSpec((1, D_MODEL), lambda i, f: (0, 0)),
            pl.BlockSpec((D_MODEL, tf), lambda i, f: (0, f)),
            pl.BlockSpec((D_MODEL, tf), lambda i, f: (0, f)),
            pl.BlockSpec((tf, D_MODEL), lambda i, f: (f, 0))],
        out_specs=row(D_MODEL),
        out_shape=jax.ShapeDtypeStruct((t, D_MODEL), F32),
        scratch_shapes=[pltpu.VMEM((tm, D_MODEL), BF16), pltpu.VMEM((tm, D_MODEL), F32)],
        compiler_params=_cparams(("parallel", "arbitrary")),
    )(x, *mixes, w_out, g, w_gate, w_up, w_down)


def _router_kernel(x_ref, ma_ref, mb_ref, mc_ref, md_ref, wo_ref, g_ref, wr_ref, x1_ref, h_ref, r_ref):
    x1 = _mix_outproj(x_ref, ma_ref, mb_ref, mc_ref, md_ref, wo_ref)
    x1_ref[...] = x1
    h = _rms(x1, g_ref[...])
    h_ref[...] = h.astype(BF16)
    lane = lax.broadcasted_iota(jnp.int32, (x1.shape[0], LANE), 1)
    logits = jnp.where(lane < N_EXPERTS, _dot(h, wr_ref[...], HI), NEG)
    m1 = jnp.max(logits, axis=1, keepdims=True)
    e1 = jnp.min(jnp.where(logits == m1, lane, LANE), axis=1, keepdims=True)
    rest = jnp.where(lane == e1, NEG, logits)
    m2 = jnp.max(rest, axis=1, keepdims=True)
    e2 = jnp.min(jnp.where(rest == m2, lane, LANE), axis=1, keepdims=True)
    ex = jnp.exp(m2 - m1)
    w1 = 1.0 / (1.0 + ex)
    w2 = ex / (1.0 + ex)
    r_ref[...] = jnp.where(lane == 0, e1.astype(F32),
                           jnp.where(lane == 1, e2.astype(F32),
                                     jnp.where(lane == 2, w1, jnp.where(lane == 3, w2, 0.0))))


def _outproj_router(x, mixes, w_out, g, w_router):
    t = x.shape[0]
    tm = min(FFN_TM, t)
    wr = jnp.zeros((D_MODEL, LANE), F32).at[:, 0:N_EXPERTS].set(w_router.astype(F32))
    row = lambda w: pl.BlockSpec((tm, w), lambda i: (i, 0))
    full = lambda a: pl.BlockSpec(a.shape, lambda i: (0, 0))
    return pl.pallas_call(
        _router_kernel,
        name="outproj_router",
        grid=(t // tm,),
        in_specs=[row(D_MODEL)] + [row(GROUP_WIDTH)] * 4 + [full(w_out), full(g), full(wr)],
        out_specs=[row(D_MODEL), row(D_MODEL), row(LANE)],
        out_shape=[jax.ShapeDtypeStruct((t, D_MODEL), F32), jax.ShapeDtypeStruct((t, D_MODEL), BF16),
                   jax.ShapeDtypeStruct((t, LANE), F32)],
        compiler_params=_cparams(("parallel",)),
    )(x, *mixes, w_out, g, wr)


def _moe_kernel(te_ref, na_ref, x_ref, wg_ref, wu_ref, wd_ref, o_ref, acc_ref):
    i = pl.program_id(0)
    f = pl.program_id(1)

    @pl.when(i < na_ref[0])
    def _active():
        @pl.when(f == 0)
        def _zero():
            acc_ref[...] = jnp.zeros_like(acc_ref)

        x = x_ref[...]
        gt = _dot(x, wg_ref[...].astype(BF16))
        act = (gt * jax.nn.sigmoid(gt) * _dot(x, wu_ref[...].astype(BF16))).astype(BF16)
        acc_ref[...] += _dot(act, wd_ref[...].astype(BF16))

        @pl.when(f == pl.num_programs(1) - 1)
        def _last():
            o_ref[...] = acc_ref[...]

    @pl.when(jnp.logical_and(i >= na_ref[0], f == pl.num_programs(1) - 1))
    def _unused_tile():
        o_ref[...] = jnp.zeros_like(o_ref)


def _moe_experts(xs, tile_e, n_active, w_gate, w_up, w_down, n_tiles):
    tm, tf = MOE_TM, MOE_TF
    nf = D_FF_EXPERT // tf

    def tile(i, na):
        return jnp.minimum(i, na[0] - 1)

    def fidx(i, f, na):
        return jnp.where(i < na[0], f, nf - 1)

    grid_spec = pltpu.PrefetchScalarGridSpec(
        num_scalar_prefetch=2,
        grid=(n_tiles, nf),
        in_specs=[
            pl.BlockSpec((tm, D_MODEL), lambda i, f, te, na: (tile(i, na), 0)),
            pl.BlockSpec((None, D_MODEL, tf), lambda i, f, te, na: (te[tile(i, na)], 0, fidx(i, f, na))),
            pl.BlockSpec((None, D_MODEL, tf), lambda i, f, te, na: (te[tile(i, na)], 0, fidx(i, f, na))),
            pl.BlockSpec((None, tf, D_MODEL), lambda i, f, te, na: (te[tile(i, na)], fidx(i, f, na), 0)),
        ],
        out_specs=pl.BlockSpec((tm, D_MODEL), lambda i, f, te, na: (i, 0)),
        scratch_shapes=[pltpu.VMEM((tm, D_MODEL), F32)],
    )
    return pl.pallas_call(
        _moe_kernel,
        name="moe_experts",
        grid_spec=grid_spec,
        out_shape=jax.ShapeDtypeStruct((n_tiles * tm, D_MODEL), F32),
        compiler_params=_cparams(("arbitrary", "arbitrary")),
    )(tile_e, n_active, xs, w_gate, w_up, w_down)


def _route(route, t):
    tm = MOE_TM
    top_e = route[:, 0:TOP_K].astype(jnp.int32)
    n_assign = t * TOP_K
    flat_e = top_e.reshape(n_assign)
    onehot = (flat_e[:, None] == jnp.arange(N_EXPERTS)[None, :]).astype(jnp.int32)
    rank = jnp.take_along_axis(jnp.cumsum(onehot, axis=0), flat_e[:, None], axis=1)[:, 0] - 1
    counts = jnp.sum(onehot, axis=0)
    padded = (counts + tm - 1) // tm * tm
    pad_end = jnp.cumsum(padded)
    pad_start = pad_end - padded
    dest = pad_start[flat_e] + rank
    n_tiles = -(-n_assign // tm) + N_EXPERTS
    n_slots = n_tiles * tm
    slot_tok = jnp.zeros((n_slots,), jnp.int32).at[dest].set(jnp.arange(n_assign, dtype=jnp.int32) // TOP_K)
    tile_e = jnp.minimum(jnp.searchsorted(pad_end, jnp.arange(n_tiles) * tm, side='right'),
                         N_EXPERTS - 1).astype(jnp.int32)
    n_active = (pad_end[-1] // tm).astype(jnp.int32).reshape(1)
    return slot_tok, tile_e, n_active, dest.reshape(t, TOP_K), n_tiles


def _combine_kernel(x_ref, y0_ref, y1_ref, r_ref, g_ref, o_ref, *, final):
    r = r_ref[...]
    y = x_ref[...] + (y0_ref[...] * r[:, TOP_K:TOP_K + 1] + y1_ref[...] * r[:, TOP_K + 1:TOP_K + 2])
    o_ref[...] = _rms(y, g_ref[...]) if final else y


def _moe_combine(x1, y0, y1, route, g, final):
    t = x1.shape[0]
    tm = min(FFN_TM, t)
    row = pl.BlockSpec((tm, D_MODEL), lambda i: (i, 0))
    return pl.pallas_call(
        functools.partial(_combine_kernel, final=final),
        name="moe_combine",
        grid=(t // tm,),
        in_specs=[row, row, row, pl.BlockSpec((tm, LANE), lambda i: (i, 0)),
                  pl.BlockSpec((1, D_MODEL), lambda i: (0, 0))],
        out_specs=row,
        out_shape=jax.ShapeDtypeStruct((t, D_MODEL), F32),
        compiler_params=_cparams(("parallel",)),
    )(x1, y0, y1, route, g)


def _final_norm_kernel(x_ref, g_ref, o_ref):
    o_ref[...] = _rms(x_ref[...], g_ref[...])


def _split_w_in(w):
    def pad(a, width):
        return jnp.pad(a, ((0, 0), (0, width - a.shape[1])))
    a_end = 4 * GROUP_WIDTH + 2 * N_HEADS
    b_end = a_end + 3 * GROUP_WIDTH
    c_end = b_end + 2 * GROUP_WIDTH
    wa = pad(w[:, 0:a_end], A_W)
    wb = w[:, a_end:b_end]
    wc = w[:, b_end:c_end]
    wd = pad(w[:, c_end:], D_W)
    return tuple(m.astype(BF16) for m in (wa, wb, wc, wd))


def kernel(x, norm_mix, w_in, a_conv, a_A_log, a_dt_bias, a_norm, b_lambda, b_norm, rel_bias,
           c_conv_w, c_conv_b, c_w_a, c_b_a, c_w_x, c_b_x, c_a_param, d_w_lr, d_b_lr, d_norm,
           w_out, norm_ffn, ffn_w_gate, ffn_w_up, ffn_w_down, moe_router, moe_w_gate, moe_w_up,
           moe_w_down, norm_final):
    batch, seq, _ = x.shape
    depth = w_in.shape[0]
    t = batch * seq
    xt = x.reshape(t, D_MODEL).astype(F32)
    bias_tiles = _attn_bias_tiles(rel_bias)
    row = lambda p: p.astype(F32)[None, :]
    out = None
    for l in range(depth):
        pa, pb, pc, pd = _inproj(xt, row(norm_mix[l]), *_split_w_in(w_in[l]))
        lam_init = 0.8 - 0.6 * math.exp(-0.3 * l)
        mixes = (
            _gdn(pa, a_conv[l], a_A_log[l], a_dt_bias[l], a_norm[l], batch, seq),
            _diff_attn(pb, b_lambda[l], lam_init, bias_tiles, b_norm[l], batch, seq),
            _rglru(pc, c_conv_w[l], c_conv_b[l], c_w_a[l], c_b_a[l], c_w_x[l], c_b_x[l],
                   c_a_param[l], batch, seq),
            _gla(pd, d_w_lr[l], d_b_lr[l], d_norm[l], batch, seq),
        )
        wo = w_out[l].astype(BF16)
        if l % 2 == 0:
            j = l // 2
            xt = _outproj_ffn(xt, mixes, wo, row(norm_ffn[l]), ffn_w_gate[j].astype(BF16),
                              ffn_w_up[j].astype(BF16), ffn_w_down[j].astype(BF16))
            out = None
        else:
            j = l // 2
            x1, h, route = _outproj_router(xt, mixes, wo, row(norm_ffn[l]), moe_router[j])
            slot_tok, tile_e, n_active, dest, n_tiles = _route(route, t)
            take = lambda a, i: a.at[i].get(mode='promise_in_bounds')
            xs = take(h, slot_tok)
            ys = _moe_experts(xs, tile_e, n_active, moe_w_gate[j], moe_w_up[j], moe_w_down[j], n_tiles)
            final = l == depth - 1
            y = _moe_combine(x1, take(ys, dest[:, 0]), take(ys, dest[:, 1]), route, row(norm_final), final)
            if final:
                out = y
            else:
                xt = y
    if out is None:
        tm = min(FFN_TM, t)
        rowspec = pl.BlockSpec((tm, D_MODEL), lambda i: (i, 0))
        out = pl.pallas_call(
            _final_norm_kernel, name="final_norm", grid=(t // tm,),
            in_specs=[rowspec, pl.BlockSpec((1, D_MODEL), lambda i: (0, 0))],
            out_specs=rowspec, out_shape=jax.ShapeDtypeStruct((t, D_MODEL), F32),
            compiler_params=_cparams(("parallel",)),
        )(xt, row(norm_final))
    return out.reshape(batch, seq, D_MODEL).astype(x.dtype)
```

```python
import functools
import math

import jax
import jax.numpy as jnp
import numpy as np
from jax import lax
from jax.experimental import pallas as pl
from jax.experimental.pallas import tpu as pltpu

D_MODEL = 1024
CHUNK = 64
N_HEADS = 4
HEAD_DIM = 64
GROUP_WIDTH = 256
CONV_WIDTH = 4
DIFF_DH = 32
Q_BLOCK = 128
REL_BUCKETS = 32
REL_MAX_DIST = 128
RG_C = 8.0
GLA_DK = 32
GLA_RANK = 16
GLA_TAU = 16.0
D_FF = 2816
N_EXPERTS = 8
TOP_K = 2
D_FF_EXPERT = 3584
EPS = 1e-6
assert CHUNK == HEAD_DIM

LANE = 128
SUBLANE = 8
VMEM_LIMIT = 56 * 1024 * 1024

F32 = jnp.float32
BF16 = jnp.bfloat16
HI = lax.Precision.HIGHEST
NEG = -1e30
LOG2E = math.log2(math.e)

A_W = 1024
B_W = 768
C_W = 512
D_W = 768

PROJ_TM = 512
MIX_TS = 256
GDN_BLK = 16
GDN_NB = 2
ATT_T = 128
ATT_VROWS = 80
FFN_TM = 512
FFN_TF = 1408
MOE_TM = 1024
MOE_TF = 512


def _cparams(sem):
    return pltpu.CompilerParams(dimension_semantics=sem, vmem_limit_bytes=VMEM_LIMIT)


def _dot(a, b, precision=None):
    return jnp.dot(a, b, preferred_element_type=F32, precision=precision)


def _dot_nt(a, b, precision=None):
    return lax.dot_general(a, b, (((1,), (1,)), ((), ())), preferred_element_type=F32,
                           precision=precision)


def _dot_tn(a, b, precision=None):
    return lax.dot_general(a, b, (((0,), (0,)), ((), ())), preferred_element_type=F32,
                           precision=precision)


def _softplus(x):
    return jnp.maximum(x, 0.0) + jnp.log1p(jnp.exp(-jnp.abs(x)))


def _rms(x, g):
    return x * lax.rsqrt(jnp.mean(x * x, axis=-1, keepdims=True) + EPS) * g


def _causal_conv(x, tail, w):
    row = lax.broadcasted_iota(jnp.int32, (SUBLANE, x.shape[1]), 0)
    y = x * w[CONV_WIDTH - 1:CONV_WIDTH, :]
    for d in range(1, CONV_WIDTH):
        rolled = pltpu.roll(x, d, 0)
        first = jnp.where(row < d, pltpu.roll(tail, d, 0), rolled[:SUBLANE])
        shifted = jnp.concatenate([first, rolled[SUBLANE:]], axis=0)
        y = y + shifted * w[CONV_WIDTH - 1 - d:CONV_WIDTH - d, :]
    return y


def _head_mask(width, per_head):
    lane = lax.broadcasted_iota(jnp.int32, (1, width), 1)
    return [(lane // per_head) == h for h in range(N_HEADS)]


def _stack_heads(x, masks):
    return jnp.concatenate([jnp.where(m, x, 0.0) for m in masks], axis=0)


def _unstack_heads(r, masks, c):
    out = jnp.where(masks[0], r[0:c], 0.0)
    for h in range(1, N_HEADS):
        out = out + jnp.where(masks[h], r[h * c:(h + 1) * c], 0.0)
    return out


def _inproj_kernel(x_ref, g_ref, *refs):
    n = len(refs) // 2
    h = _rms(x_ref[...], g_ref[...]).astype(BF16)
    for w_ref, o_ref in zip(refs[:n], refs[n:]):
        o_ref[...] = _dot(h, w_ref[...]).astype(o_ref.dtype)


def _inproj(x, g, weights):
    t = x.shape[0]
    tm = min(PROJ_TM, t)
    row = lambda w: pl.BlockSpec((tm, w), lambda i: (i, 0))
    full = lambda a: pl.BlockSpec(a.shape, lambda i: (0, 0))
    ws = [w for w, _ in weights]
    return pl.pallas_call(
        _inproj_kernel,
        name="inproj",
        grid=(t // tm,),
        in_specs=[row(D_MODEL), full(g)] + [full(w) for w in ws],
        out_specs=[row(w.shape[1]) for w in ws],
        out_shape=[jax.ShapeDtypeStruct((t, w.shape[1]), dt) for w, dt in weights],
        compiler_params=_cparams(("parallel",)),
    )(x, g, *ws)


def _split2(x):
    hi = x.astype(BF16)
    return hi, (x - hi.astype(F32)).astype(BF16)


def _dot_x2(x, w):
    hi, lo = _split2(x)
    return _dot(hi, w) + _dot(lo, w)


def _bd_tile(x, bd):
    return jnp.concatenate([x] * N_HEADS, axis=0) * bd


def _dot_bd(a, bs, bd):
    ab = a.astype(BF16)
    return [_dot(ab, _bd_tile(b.astype(BF16), bd)) for b in bs]


def _gdn_kernel(a_ref, ba_ref, convw_ref, alog_ref, dtb_ref, gn_ref, bd_ref, lt_ref, e_ref,
                o_ref, tail_ref, state_ref, *, ts, nb):
    @pl.when(pl.program_id(1) == 0)
    def _init():
        tail_ref[...] = jnp.zeros_like(tail_ref)
        state_ref[...] = jnp.zeros_like(state_ref)

    c = CHUNK
    bd = bd_ref[...]
    lt = lt_ref[...]
    q, k, v, beta, gcum = [], [], [], [], []
    for b in range(nb):
        xin = a_ref[b, :, 0:768].astype(F32)
        y = _causal_conv(xin, tail_ref[b], convw_ref[...])
        tail_ref[b] = xin[ts - SUBLANE:ts, :]
        y = y * jax.nn.sigmoid(y)
        qb, kb_ = y[:, 0:256], y[:, 256:512]
        q.append(qb * lax.rsqrt(_dot_x2(qb * qb, bd) + EPS) * (HEAD_DIM ** -0.5))
        k.append(kb_ * lax.rsqrt(_dot_x2(kb_ * kb_, bd) + EPS))
        v.append(y[:, 512:768])
        e = _dot_x2(ba_ref[b], e_ref[...])
        beta.append(jax.nn.sigmoid(e[:, 0:256]))
        g = -jnp.exp(alog_ref[...]) * _softplus(e[:, 256:512] + dtb_ref[...])
        g_hi, g_lo = _split2(g)
        g_lo2 = (g - g_hi.astype(F32) - g_lo.astype(F32)).astype(BF16)
        gcum.append(_dot(lt, g_hi) + (_dot(lt, g_lo) + _dot(lt, g_lo2)))

    ri = lax.broadcasted_iota(jnp.int32, (c, GROUP_WIDTH), 0)
    cj = lax.broadcasted_iota(jnp.int32, (c, GROUP_WIDTH), 1) % HEAD_DIM
    causal = ri >= cj
    strict = ri > cj
    diag = ri == cj
    same_blk = (ri // GDN_BLK) == (cj // GDN_BLK)
    eye = diag.astype(F32)
    masks = _head_mask(GROUP_WIDTH, HEAD_DIM)
    items = [(b, slice(ci * c, (ci + 1) * c)) for ci in range(ts // c) for b in range(nb)]
    idx = range(len(items))

    a_qk, p0, nn, rhs_u, rhs_w = [], [], [], [], []
    for b, sl in items:
        kc, gc = k[b][sl], gcum[b][sl]
        kb = kc * beta[b][sl]
        grow = jnp.sum(jnp.where(diag, gc, 0.0), axis=0, keepdims=True)
        gamma = jnp.exp(jnp.where(causal, gc - grow, NEG))
        kst = _stack_heads(kc, masks).astype(BF16)
        aa = _dot_nt(jnp.concatenate([kb, q[b][sl]], axis=0).astype(BF16), kst)
        a_kk = jnp.where(strict, aa[0:c] * gamma, 0.0)
        a_qk.append(aa[c:2 * c] * gamma)
        p0.append(jnp.where(same_blk, -a_kk, 0.0))
        nn.append(jnp.where(same_blk, 0.0, a_kk))
        rhs_u.append(v[b][sl] * beta[b][sl])
        rhs_w.append(kb * jnp.exp(gc))
    t1 = [eye + p for p in p0]
    p1 = [_dot_bd(p0[i], [p0[i]], bd)[0] for i in idx]
    pr = [_dot_bd(p1[i], [p1[i], t1[i]], bd) for i in idx]
    p2 = [x[0] for x in pr]
    t2 = [t1[i] + pr[i][1] for i in idx]
    pr = [_dot_bd(p2[i], [p2[i], t2[i]], bd) for i in idx]
    p3 = [x[0] for x in pr]
    t3 = [t2[i] + pr[i][1] for i in idx]
    dinv = [t3[i] + _dot_bd(p3[i], [t3[i]], bd)[0] for i in idx]
    m1 = [_dot_bd(dinv[i], [nn[i]], bd)[0] for i in idx]
    m2 = [_dot_bd(m1[i], [m1[i]], bd)[0] for i in idx]
    im = [eye - m for m in m1]
    qq = [im[i] + _dot_bd(im[i], [m2[i]], bd)[0] for i in idx]
    inv = [_dot_bd(qq[i], [dinv[i]], bd)[0] for i in idx]
    uw = [_dot_bd(inv[i], [rhs_u[i], rhs_w[i]], bd) for i in idx]

    bdf = bd.astype(F32)
    state = [state_ref[b] for b in range(nb)]
    outs = [[] for _ in range(nb)]
    for i, (b, sl) in enumerate(items):
        gc = gcum[b][sl]
        g_last = gc[c - 1:c, :]
        u, w = uw[i]
        ws_qs = _dot(jnp.concatenate([w, q[b][sl] * jnp.exp(gc)], axis=0).astype(BF16), state[b].astype(BF16))
        v_new = (u - ws_qs[0:c]).astype(BF16)
        outs[b].append(ws_qs[c:2 * c] + _dot(a_qk[i].astype(BF16), _bd_tile(v_new, bd)))
        kd = k[b][sl] * jnp.exp(g_last - gc)
        state[b] = state[b] * jnp.exp(g_last) + _dot_tn(kd.astype(BF16), v_new) * bdf
    for b in range(nb):
        state_ref[b] = state[b]
        o = jnp.concatenate(outs[b], axis=0)
        o = o * lax.rsqrt(_dot_x2(o * o, bd) * (1.0 / HEAD_DIM) + EPS) * gn_ref[...]
        gate = a_ref[b, :, 768:1024].astype(F32)
        o_ref[b] = (o * (gate * jax.nn.sigmoid(gate))).astype(o_ref.dtype)


def _block_diag_ones(n, blk):
    i = np.arange(n)
    return jnp.asarray((i[:, None] // blk) == (i[None, :] // blk), F32)


def _gdn(pa, pba, conv_w, a_log, dt_bias, norm_g, batch, seq):
    ts = min(MIX_TS, seq)
    nst = seq // ts
    bd = _block_diag_ones(GROUP_WIDTH, HEAD_DIM).astype(BF16)
    r = np.arange(ts)
    lt = jnp.asarray((r[:, None] // CHUNK == r[None, :] // CHUNK) & (r[:, None] >= r[None, :]), BF16)
    lane = np.arange(GROUP_WIDTH)
    e = np.zeros((LANE, 2 * GROUP_WIDTH), np.float32)
    for h in range(N_HEADS):
        e[h, np.nonzero(lane // HEAD_DIM == h)[0]] = 1.0
        e[N_HEADS + h, GROUP_WIDTH + np.nonzero(lane // HEAD_DIM == h)[0]] = 1.0
    rep = lambda p: jnp.repeat(p.astype(F32), HEAD_DIM)[None, :]
    consts = [conv_w.astype(F32), rep(a_log), rep(dt_bias),
              jnp.tile(norm_g.astype(F32), N_HEADS)[None, :], bd, lt, jnp.asarray(e, BF16)]
    nb = GDN_NB if batch % GDN_NB == 0 else 1
    full = lambda a: pl.BlockSpec(a.shape, lambda b, s: (0, 0))
    out = pl.pallas_call(
        functools.partial(_gdn_kernel, ts=ts, nb=nb),
        name="gdn",
        grid=(batch // nb, nst),
        in_specs=[pl.BlockSpec((nb, ts, A_W), lambda b, s: (b, s, 0)),
                  pl.BlockSpec((nb, ts, LANE), lambda b, s: (b, s, 0))] + [full(a) for a in consts],
        out_specs=pl.BlockSpec((nb, ts, GROUP_WIDTH), lambda b, s: (b, s, 0)),
        out_shape=jax.ShapeDtypeStruct((batch, seq, GROUP_WIDTH), BF16),
        scratch_shapes=[pltpu.VMEM((nb, SUBLANE, 768), F32),
                        pltpu.VMEM((nb, GROUP_WIDTH, GROUP_WIDTH), F32)],
        compiler_params=_cparams(("parallel", "arbitrary")),
    )(pa.reshape(batch, seq, A_W), pba.reshape(batch, seq, LANE), *consts)
    return out.reshape(batch * seq, GROUP_WIDTH)


def _attn_kernel(qa_ref, qb_ref, k_ref, v_ref, bias_ref, lamv_ref, gn_ref, oa_ref, ob_ref,
                 va_ref, qt_ref, s_ref, p_ref, al_ref, m_ref, l_ref, acc_ref, *, lam_init, seq):
    t = ATT_T
    nq = seq // t
    i = pl.program_id(1)
    na = i + 1
    nlan = 2 * N_HEADS * t
    nsteps = nq + 1

    @pl.when(i == 0)
    def _stage_v():
        def body(j, carry):
            rows = pl.ds(pl.multiple_of(j * t, t), t)
            vt = v_ref[rows, :].astype(F32).T.astype(BF16)
            for h in range(N_HEADS):
                va_ref[j, h, 0:HEAD_DIM, :] = vt[h * HEAD_DIM:(h + 1) * HEAD_DIM, :]
                va_ref[j, h, HEAD_DIM:ATT_VROWS, :] = jnp.ones((ATT_VROWS - HEAD_DIM, t), BF16)
            return carry
        lax.fori_loop(0, seq // t, body, 0)

    feat = lax.broadcasted_iota(jnp.int32, (GROUP_WIDTH, t), 0) // DIFF_DH
    for w, q_ref in enumerate((qa_ref, qb_ref)):
        qt = (q_ref[...].astype(F32) * (DIFF_DH ** -0.5 * LOG2E)).T
        for idx in range(2 * N_HEADS):
            qt_ref[w, :, idx * t:(idx + 1) * t] = jnp.where(feat == idx, qt, 0.0).astype(BF16)
    m_ref[...] = jnp.full((2, 1, nlan), NEG, F32)
    l_ref[...] = jnp.zeros((2, 1, nlan), F32)
    acc_ref[...] = jnp.zeros((2, HEAD_DIM, nlan), F32)

    head_cols = [slice(2 * h * t, (2 * h + 2) * t) for h in range(N_HEADS)]

    def tile_of(s):
        if s >= nq // 2:
            bias = 2 if s == nq else (1 if s == nq - 1 else None)
            return 1, s - na, bias
        w = (s >= na).astype(jnp.int32)
        bias = jnp.where(w == 1, 0, jnp.where(s == na - 1, 2, jnp.where(s == na - 2, 1, 0)))
        return w, s - na * w, bias

    def scores(s, h):
        w, kt, _ = tile_of(s)
        rows = pl.ds(pl.multiple_of(kt * t, t), t)
        s_ref[s % 2, :, head_cols[h]] = _dot(k_ref[rows, :], qt_ref[w, :, head_cols[h]])

    def softmax(s, h):
        w, _, bias = tile_of(s)
        cols = head_cols[h]
        sc = s_ref[s % 2, :, cols]
        if bias is not None:
            sc = sc + bias_ref[bias, :, cols]
        m_prev = m_ref[w, :, cols]
        m_new = jnp.maximum(m_prev, jnp.max(sc, axis=0, keepdims=True))
        al_ref[s % 2, :, cols] = jnp.exp2(m_prev - m_new)
        m_ref[w, :, cols] = m_new
        p_ref[s % 2, :, cols] = jnp.exp2(sc - m_new).astype(BF16)

    def values(s, h):
        w, kt, _ = tile_of(s)
        cols = head_cols[h]
        alpha = al_ref[s % 2, :, cols]
        pv = _dot(va_ref[kt, h], p_ref[s % 2, :, cols])
        acc_ref[w, :, cols] = alpha * acc_ref[w, :, cols] + pv[0:HEAD_DIM]
        l_ref[w, :, cols] = alpha * l_ref[w, :, cols] + pv[HEAD_DIM:HEAD_DIM + 1]

    for s in range(nsteps + 2):
        for h in range(N_HEADS):
            if s < nsteps:
                scores(s, h)
            if 1 <= s <= nsteps:
                softmax(s - 1, h)
            if s >= 2:
                values(s - 2, h)

    lv = lamv_ref[...]
    lam = (jnp.exp(jnp.sum(lv[0:1] * lv[1:2], axis=1, keepdims=True))
           - jnp.exp(jnp.sum(lv[2:3] * lv[3:4], axis=1, keepdims=True)) + lam_init)
    for w, o_ref in enumerate((oa_ref, ob_ref)):
        inv_l = 1.0 / l_ref[w]
        outs = []
        for h in range(N_HEADS):
            c0 = slice(2 * h * t, (2 * h + 1) * t)
            c1 = slice((2 * h + 1) * t, (2 * h + 2) * t)
            oh = acc_ref[w, :, c0] * inv_l[:, c0] - lam * (acc_ref[w, :, c1] * inv_l[:, c1])
            oh = oh * lax.rsqrt(jnp.mean(oh * oh, axis=0, keepdims=True) + EPS)
            outs.append(oh)
        o = jnp.concatenate(outs, axis=0).T
        o_ref[...] = (o * gn_ref[...] * (1.0 - lam_init)).astype(o_ref.dtype)


def _t5_bucket(rel):
    nb = REL_BUCKETS // 2
    bucket = jnp.where(rel > 0, nb, 0)
    n = jnp.abs(rel)
    max_exact = nb // 2
    large = max_exact + (jnp.log(jnp.maximum(n, 1).astype(F32) / max_exact)
                         / math.log(REL_MAX_DIST / max_exact) * (nb - max_exact)).astype(jnp.int32)
    large = jnp.minimum(large, nb - 1)
    return bucket + jnp.where(n < max_exact, n, large)


def _attn_bias_tiles(rel_bias):
    t = ATT_T
    table = rel_bias.astype(F32)
    kk = jnp.arange(t)[:, None]
    qq = jnp.arange(t)[None, :]

    def expand(b):
        b = jnp.transpose(b, (0, 2, 1))
        b = jnp.broadcast_to(b[:, :, None, :], (t, N_HEADS, 2, t))
        return b.reshape(t, 2 * N_HEADS * t)

    table = table * LOG2E
    diag = table[_t5_bucket(kk - qq)]
    diag = jnp.where(((kk // CHUNK) <= (qq // CHUNK))[:, :, None], diag, NEG)
    near = table[_t5_bucket(kk - qq - t)]
    far = table[_t5_bucket(jnp.full((1, 1), -(REL_MAX_DIST + 1), jnp.int32))]
    far = jnp.broadcast_to(far, (t, t, N_HEADS))
    return jnp.stack([jnp.zeros((t, 2 * N_HEADS * t), F32), expand(near - far), expand(diag - far)])


def _diff_attn(pb, lam_vecs, lam_init, bias_tiles, norm_g, batch, seq):
    t = ATT_T
    nq = seq // t
    gn = jnp.tile(norm_g.astype(F32), N_HEADS)[None, :]
    lamv = lam_vecs.astype(F32)
    nlan = 2 * N_HEADS * t
    assert nq % 2 == 0
    nh = nq // 2
    full = lambda a: pl.BlockSpec(a.shape, lambda b, i: (0,) * a.ndim)
    half = jax.ShapeDtypeStruct((batch * nh * t, GROUP_WIDTH), BF16)
    lo, hi = pl.pallas_call(
        functools.partial(_attn_kernel, lam_init=lam_init, seq=seq),
        name="diffattn",
        grid=(batch, nh),
        in_specs=[pl.BlockSpec((t, GROUP_WIDTH), lambda b, i: (b * nq + i, 0)),
                  pl.BlockSpec((t, GROUP_WIDTH), lambda b, i: (b * nq + nq - 1 - i, 0)),
                  pl.BlockSpec((seq, GROUP_WIDTH), lambda b, i: (b, 1)),
                  pl.BlockSpec((seq, GROUP_WIDTH), lambda b, i: (b, 2)),
                  full(bias_tiles), full(lamv), full(gn)],
        out_specs=[pl.BlockSpec((t, GROUP_WIDTH), lambda b, i: (b * nh + i, 0)),
                   pl.BlockSpec((t, GROUP_WIDTH), lambda b, i: (b * nh + nh - 1 - i, 0))],
        out_shape=[half, half],
        scratch_shapes=[pltpu.VMEM((seq // t, N_HEADS, ATT_VROWS, t), BF16),
                        pltpu.VMEM((2, GROUP_WIDTH, nlan), BF16),
                        pltpu.VMEM((2, t, nlan), F32),
                        pltpu.VMEM((2, t, nlan), BF16),
                        pltpu.VMEM((2, 1, nlan), F32),
                        pltpu.VMEM((2, 1, nlan), F32), pltpu.VMEM((2, 1, nlan), F32),
                        pltpu.VMEM((2, HEAD_DIM, nlan), F32)],
        compiler_params=_cparams(("parallel", "arbitrary")),
    )(pb, pb, pb, pb, bias_tiles, lamv, gn)
    out = jnp.concatenate([lo.reshape(batch, nh * t, GROUP_WIDTH), hi.reshape(batch, nh * t, GROUP_WIDTH)],
                          axis=1)
    return out.reshape(batch * seq, GROUP_WIDTH)


def _rglru_kernel(c_ref, convw_ref, convb_ref, wa_ref, ba_ref, wx_ref, bx_ref, ap_ref, o_ref,
                  tail_ref, h_ref, *, ts):
    @pl.when(pl.program_id(1) == 0)
    def _init():
        tail_ref[...] = jnp.zeros_like(tail_ref)
        h_ref[...] = jnp.zeros_like(h_ref)

    xb = c_ref[:, 0:256].astype(F32)
    gb = c_ref[:, 256:512].astype(F32)
    xc = _causal_conv(xb, tail_ref[...], convw_ref[...]) + convb_ref[...]
    tail_ref[...] = xb[ts - SUBLANE:ts, :]
    xcb = xc.astype(BF16)
    gate_a = jax.nn.sigmoid(_dot(xcb, wa_ref[...]) + ba_ref[...])
    gate_x = jax.nn.sigmoid(_dot(xcb, wx_ref[...]) + bx_ref[...])
    log_a = -RG_C * gate_a * _softplus(ap_ref[...])
    a = jnp.exp(log_a)
    th = jnp.tanh(log_a)
    u = xc * gate_x * jnp.sqrt(-2.0 * th / (1.0 - th))
    row = lax.broadcasted_iota(jnp.int32, (ts, GROUP_WIDTH), 0)
    d = 1
    while d < ts:
        keep = row >= d
        a_sh = jnp.where(keep, pltpu.roll(a, d, 0), 1.0)
        u_sh = jnp.where(keep, pltpu.roll(u, d, 0), 0.0)
        u = u + a * u_sh
        a = a * a_sh
        d *= 2
    h = u + a * h_ref[...]
    h_ref[...] = h[ts - 1:ts, :]
    gelu = 0.5 * gb * (1.0 + jnp.tanh(math.sqrt(2.0 / math.pi) * (gb + 0.044715 * (gb * gb * gb))))
    o_ref[...] = (h * gelu).astype(o_ref.dtype)


def _block_diag_weight(w):
    nb, wi, wo = w.shape
    out = jnp.zeros((nb * wi, nb * wo), w.dtype)
    for i in range(nb):
        out = out.at[i * wi:(i + 1) * wi, i * wo:(i + 1) * wo].set(w[i])
    return out


def _rglru(pc, conv_w, conv_b, w_a, b_a, w_x, b_x, a_param, batch, seq):
    ts = min(MIX_TS, seq)
    nst = seq // ts
    r = lambda p: p.astype(F32)[None, :]
    consts = [conv_w.astype(F32), r(conv_b), _block_diag_weight(w_a).astype(BF16), r(b_a),
              _block_diag_weight(w_x).astype(BF16), r(b_x), r(a_param)]
    full = lambda a: pl.BlockSpec(a.shape, lambda b, s: (0, 0))
    return pl.pallas_call(
        functools.partial(_rglru_kernel, ts=ts),
        name="rglru",
        grid=(batch, nst),
        in_specs=[pl.BlockSpec((ts, C_W), lambda b, s: (b * nst + s, 0))] + [full(a) for a in consts],
        out_specs=pl.BlockSpec((ts, GROUP_WIDTH), lambda b, s: (b * nst + s, 0)),
        out_shape=jax.ShapeDtypeStruct((batch * seq, GROUP_WIDTH), BF16),
        scratch_shapes=[pltpu.VMEM((SUBLANE, GROUP_WIDTH), F32), pltpu.VMEM((1, GROUP_WIDTH), F32)],
        compiler_params=_cparams(("parallel", "arbitrary")),
    )(pc, *consts)


def _gla_kernel(d_ref, lr_ref, wlr_ref, blr_ref, gn_ref, bd_ref, bdt_ref, lt_ref, o_ref, state_ref,
                *, ts, nb):
    @pl.when(pl.program_id(1) == 0)
    def _init():
        state_ref[...] = jnp.zeros_like(state_ref)

    c = CHUNK
    bd = bd_ref[...]
    bdt = bdt_ref[...]
    lt = lt_ref[...]
    kmasks = _head_mask(N_HEADS * GLA_DK, GLA_DK)
    ri = lax.broadcasted_iota(jnp.int32, (c, GROUP_WIDTH), 0)
    cj = lax.broadcasted_iota(jnp.int32, (c, GROUP_WIDTH), 1) % HEAD_DIM
    causal = ri >= cj
    q, k, gcum = [], [], []
    for b in range(nb):
        q.append(d_ref[b, :, 0:128].astype(F32) * (GLA_DK ** -0.5))
        k.append(d_ref[b, :, 128:256].astype(F32))
        lr_hi, lr_lo = _split2(lr_ref[b])
        w_hi, w_lo = _split2(wlr_ref[...])
        z = _dot(lr_hi, w_hi) + (_dot(lr_hi, w_lo) + _dot(lr_lo, w_hi)) + blr_ref[...]
        la = (jnp.minimum(z, 0.0) - jnp.log1p(jnp.exp(-jnp.abs(z)))) * (1.0 / GLA_TAU)
        la_hi, la_lo = _split2(la)
        la_lo2 = (la - la_hi.astype(F32) - la_lo.astype(F32)).astype(BF16)
        gcum.append(_dot(lt, la_hi) + (_dot(lt, la_lo) + _dot(lt, la_lo2)))

    items = [(b, slice(ci * c, (ci + 1) * c)) for ci in range(ts // c) for b in range(nb)]
    o_intra, upd = [], []
    for b, sl in items:
        qc, kc, gc = q[b][sl], k[b][sl], gcum[b][sl]
        vc = d_ref[b, sl, 256:512]
        ref = gc[c // 2:c // 2 + 1, :]
        kst = _stack_heads(kc * jnp.exp(ref - gc), kmasks).astype(BF16)
        a_in = _dot_nt((qc * jnp.exp(gc - ref)).astype(BF16), kst)
        a_in = jnp.where(causal, a_in, 0.0).astype(BF16)
        o_intra.append(_dot(a_in, _bd_tile(vc, bd)))
        kd = kc * jnp.exp(gc[c - 1:c, :] - gc)
        upd.append(_dot_tn(vc, kd.astype(BF16)) * bdt)

    state = [state_ref[b] for b in range(nb)]
    outs = [[] for _ in range(nb)]
    for i, (b, sl) in enumerate(items):
        gc = gcum[b][sl]
        o_inter = _dot_nt((q[b][sl] * jnp.exp(gc)).astype(BF16), state[b].astype(BF16))
        outs[b].append(o_intra[i] + o_inter)
        state[b] = state[b] * jnp.exp(gc[c - 1:c, :]) + upd[i]
    for b in range(nb):
        state_ref[b] = state[b]
        o = jnp.concatenate(outs[b], axis=0)
        o = o * lax.rsqrt(_dot_x2(o * o, bd) * (1.0 / HEAD_DIM) + EPS) * gn_ref[...]
        rt = d_ref[b, :, 512:768].astype(F32)
        o_ref[b] = (o * (rt * jax.nn.sigmoid(rt))).astype(o_ref.dtype)


def _gla(pd, plr, w_lr, b_lr, norm_g, batch, seq):
    ts = min(MIX_TS, seq)
    nst = seq // ts
    kw = N_HEADS * GLA_DK
    wlr = jnp.zeros((LANE, kw), F32).at[0:GLA_RANK, :].set(w_lr.astype(F32))
    bd = _block_diag_ones(GROUP_WIDTH, HEAD_DIM).astype(BF16)
    iv = np.arange(GROUP_WIDTH)[:, None] // HEAD_DIM
    ik = np.arange(kw)[None, :] // GLA_DK
    bdt = jnp.asarray(iv == ik, F32)
    r = np.arange(ts)
    lt = jnp.asarray((r[:, None] // CHUNK == r[None, :] // CHUNK) & (r[:, None] >= r[None, :]), BF16)
    consts = [wlr, b_lr.astype(F32)[None, :], jnp.tile(norm_g.astype(F32), N_HEADS)[None, :], bd, bdt, lt]
    nb = GDN_NB if batch % GDN_NB == 0 else 1
    full = lambda a: pl.BlockSpec(a.shape, lambda b, s: (0, 0))
    out = pl.pallas_call(
        functools.partial(_gla_kernel, ts=ts, nb=nb),
        name="gla",
        grid=(batch // nb, nst),
        in_specs=[pl.BlockSpec((nb, ts, D_W), lambda b, s: (b, s, 0)),
                  pl.BlockSpec((nb, ts, LANE), lambda b, s: (b, s, 0))] + [full(a) for a in consts],
        out_specs=pl.BlockSpec((nb, ts, GROUP_WIDTH), lambda b, s: (b, s, 0)),
        out_shape=jax.ShapeDtypeStruct((batch, seq, GROUP_WIDTH), BF16),
        scratch_shapes=[pltpu.VMEM((nb, GROUP_WIDTH, kw), F32)],
        compiler_params=_cparams(("parallel", "arbitrary")),
    )(pd.reshape(batch, seq, D_W), plr.reshape(batch, seq, LANE), *consts)
    return out.reshape(batch * seq, GROUP_WIDTH)


def _mix_outproj(x_ref, ma_ref, mb_ref, mc_ref, md_ref, wo_ref):
    mix = jnp.concatenate([ma_ref[...], mb_ref[...], mc_ref[...], md_ref[...]], axis=1)
    return x_ref[...] + _dot(mix, wo_ref[...])


def _ffn_kernel(x_ref, ma_ref, mb_ref, mc_ref, md_ref, wo_ref, g_ref, wg_ref, wu_ref, wd_ref,
                o_ref, h_ref, acc_ref):
    f = pl.program_id(1)

    @pl.when(f == 0)
    def _first():
        x1 = _mix_outproj(x_ref, ma_ref, mb_ref, mc_ref, md_ref, wo_ref)
        acc_ref[...] = x1
        h_ref[...] = _rms(x1, g_ref[...]).astype(BF16)

    h = h_ref[...]
    gt = _dot(h, wg_ref[...])
    act = (gt * jax.nn.sigmoid(gt) * _dot(h, wu_ref[...])).astype(BF16)
    acc_ref[...] += _dot(act, wd_ref[...])

    @pl.when(f == pl.num_programs(1) - 1)
    def _last():
        o_ref[...] = acc_ref[...]


def _outproj_ffn(x, mixes, w_out, g, w_gate, w_up, w_down):
    t = x.shape[0]
    tm = min(FFN_TM, t)
    tf = FFN_TF
    nf = D_FF // tf
    row = lambda w: pl.BlockSpec((tm, w), lambda i, f: (i, 0))
    return pl.pallas_call(
        _ffn_kernel,
        name="outproj_ffn",
        grid=(t // tm, nf),
        in_specs=[row(D_MODEL)] + [row(GROUP_WIDTH)] * 4 + [
            pl.BlockSpec((D_MODEL, D_MODEL), lambda i, f: (0, 0)),
            pl.BlockSpec((1, D_MODEL), lambda i, f: (0, 0)),
            pl.BlockSpec((D_MODEL, tf), lambda i, f: (0, f)),
            pl.BlockSpec((D_MODEL, tf), lambda i, f: (0, f)),
            pl.BlockSpec((tf, D_MODEL), lambda i, f: (f, 0))],
        out_specs=row(D_MODEL),
        out_shape=jax.ShapeDtypeStruct((t, D_MODEL), F32),
        scratch_shapes=[pltpu.VMEM((tm, D_MODEL), BF16), pltpu.VMEM((tm, D_MODEL), F32)],
        compiler_params=_cparams(("parallel", "arbitrary")),
    )(x, *mixes, w_out, g, w_gate, w_up, w_down)


def _router_kernel(x_ref, ma_ref, mb_ref, mc_ref, md_ref, wo_ref, g_ref, wrh_ref, wrl_ref,
                   x1_ref, h_ref, r_ref):
    x1 = _mix_outproj(x_ref, ma_ref, mb_ref, mc_ref, md_ref, wo_ref)
    x1_ref[...] = x1
    h = _rms(x1, g_ref[...])
    h_hi, h_lo = _split2(h)
    h_ref[...] = h_hi
    lane = lax.broadcasted_iota(jnp.int32, (x1.shape[0], LANE), 1)
    logits = _dot(h_hi, wrh_ref[...]) + (_dot(h_hi, wrl_ref[...]) + _dot(h_lo, wrh_ref[...]))
    logits = jnp.where(lane < N_EXPERTS, logits, NEG)
    m1 = jnp.max(logits, axis=1, keepdims=True)
    e1 = jnp.min(jnp.where(logits == m1, lane, LANE), axis=1, keepdims=True)
    rest = jnp.where(lane == e1, NEG, logits)
    m2 = jnp.max(rest, axis=1, keepdims=True)
    e2 = jnp.min(jnp.where(rest == m2, lane, LANE), axis=1, keepdims=True)
    ex = jnp.exp(m2 - m1)
    w1 = 1.0 / (1.0 + ex)
    w2 = ex / (1.0 + ex)
    r_ref[...] = jnp.where(lane == 0, e1.astype(F32),
                           jnp.where(lane == 1, e2.astype(F32),
                                     jnp.where(lane == 2, w1, jnp.where(lane == 3, w2, 0.0))))


def _outproj_router(x, mixes, w_out, g, w_router):
    t = x.shape[0]
    tm = min(FFN_TM, t)
    wr = jnp.pad(w_router.astype(F32), ((0, 0), (0, LANE - N_EXPERTS)))
    wr_hi = wr.astype(BF16)
    wr_lo = (wr - wr_hi.astype(F32)).astype(BF16)
    row = lambda w: pl.BlockSpec((tm, w), lambda i: (i, 0))
    full = lambda a: pl.BlockSpec(a.shape, lambda i: (0, 0))
    return pl.pallas_call(
        _router_kernel,
        name="outproj_router",
        grid=(t // tm,),
        in_specs=[row(D_MODEL)] + [row(GROUP_WIDTH)] * 4 + [full(w_out), full(g), full(wr_hi), full(wr_lo)],
        out_specs=[row(D_MODEL), row(D_MODEL), row(LANE)],
        out_shape=[jax.ShapeDtypeStruct((t, D_MODEL), F32), jax.ShapeDtypeStruct((t, D_MODEL), BF16),
                   jax.ShapeDtypeStruct((t, LANE), F32)],
        compiler_params=_cparams(("parallel",)),
    )(x, *mixes, w_out, g, wr_hi, wr_lo)


def _moe_kernel(te_ref, na_ref, x_ref, wg_ref, wu_ref, wd_ref, o_ref, acc_ref):
    i = pl.program_id(0)
    f = pl.program_id(1)

    @pl.when(i < na_ref[0])
    def _active():
        @pl.when(f == 0)
        def _zero():
            acc_ref[...] = jnp.zeros_like(acc_ref)

        x = x_ref[...]
        gt = _dot(x, wg_ref[...].astype(BF16))
        act = (gt * jax.nn.sigmoid(gt) * _dot(x, wu_ref[...].astype(BF16))).astype(BF16)
        acc_ref[...] += _dot(act, wd_ref[...].astype(BF16))

        @pl.when(f == pl.num_programs(1) - 1)
        def _last():
            o_ref[...] = acc_ref[...]

    @pl.when(jnp.logical_and(i >= na_ref[0], f == pl.num_programs(1) - 1))
    def _unused_tile():
        o_ref[...] = jnp.zeros_like(o_ref)


def _moe_experts(xs, tile_e, n_active, w_gate, w_up, w_down, n_tiles):
    tm, tf = MOE_TM, MOE_TF
    nf = D_FF_EXPERT // tf

    def tile(i, na):
        return jnp.minimum(i, na[0] - 1)

    def fidx(i, f, na):
        return jnp.where(i < na[0], f, nf - 1)

    grid_spec = pltpu.PrefetchScalarGridSpec(
        num_scalar_prefetch=2,
        grid=(n_tiles, nf),
        in_specs=[
            pl.BlockSpec((tm, D_MODEL), lambda i, f, te, na: (tile(i, na), 0)),
            pl.BlockSpec((None, D_MODEL, tf), lambda i, f, te, na: (te[tile(i, na)], 0, fidx(i, f, na))),
            pl.BlockSpec((None, D_MODEL, tf), lambda i, f, te, na: (te[tile(i, na)], 0, fidx(i, f, na))),
            pl.BlockSpec((None, tf, D_MODEL), lambda i, f, te, na: (te[tile(i, na)], fidx(i, f, na), 0)),
        ],
        out_specs=pl.BlockSpec((tm, D_MODEL), lambda i, f, te, na: (i, 0)),
        scratch_shapes=[pltpu.VMEM((tm, D_MODEL), F32)],
    )
    return pl.pallas_call(
        _moe_kernel,
        name="moe_experts",
        grid_spec=grid_spec,
        out_shape=jax.ShapeDtypeStruct((n_tiles * tm, D_MODEL), F32),
        compiler_params=_cparams(("arbitrary", "arbitrary")),
    )(tile_e, n_active, xs, w_gate, w_up, w_down)


def _route(route, t):
    tm = MOE_TM
    top_e = route[:, 0:TOP_K].astype(jnp.int32)
    n_assign = t * TOP_K
    flat_e = top_e.reshape(n_assign)
    onehot = (flat_e[:, None] == jnp.arange(N_EXPERTS)[None, :]).astype(jnp.int32)
    rank = jnp.take_along_axis(jnp.cumsum(onehot, axis=0), flat_e[:, None], axis=1)[:, 0] - 1
    counts = jnp.sum(onehot, axis=0)
    padded = (counts + tm - 1) // tm * tm
    pad_end = jnp.cumsum(padded)
    pad_start = pad_end - padded
    dest = pad_start[flat_e] + rank
    n_tiles = -(-n_assign // tm) + N_EXPERTS
    n_slots = n_tiles * tm
    slot_tok = jnp.zeros((n_slots,), jnp.int32).at[dest].set(jnp.arange(n_assign, dtype=jnp.int32) // TOP_K)
    tile_e = jnp.minimum(jnp.searchsorted(pad_end, jnp.arange(n_tiles) * tm, side='right'),
                         N_EXPERTS - 1).astype(jnp.int32)
    n_active = (pad_end[-1] // tm).astype(jnp.int32).reshape(1)
    return slot_tok, tile_e, n_active, dest.reshape(t, TOP_K), n_tiles


def _combine_kernel(x_ref, y0_ref, y1_ref, r_ref, g_ref, o_ref, *, final):
    r = r_ref[...]
    y = x_ref[...] + (y0_ref[...] * r[:, TOP_K:TOP_K + 1] + y1_ref[...] * r[:, TOP_K + 1:TOP_K + 2])
    o_ref[...] = _rms(y, g_ref[...]) if final else y


def _moe_combine(x1, y0, y1, route, g, final):
    t = x1.shape[0]
    tm = min(FFN_TM, t)
    row = pl.BlockSpec((tm, D_MODEL), lambda i: (i, 0))
    return pl.pallas_call(
        functools.partial(_combine_kernel, final=final),
        name="moe_combine",
        grid=(t // tm,),
        in_specs=[row, row, row, pl.BlockSpec((tm, LANE), lambda i: (i, 0)),
                  pl.BlockSpec((1, D_MODEL), lambda i: (0, 0))],
        out_specs=row,
        out_shape=jax.ShapeDtypeStruct((t, D_MODEL), F32),
        compiler_params=_cparams(("parallel",)),
    )(x1, y0, y1, route, g)


def _final_norm_kernel(x_ref, g_ref, o_ref):
    o_ref[...] = _rms(x_ref[...], g_ref[...])


def _split_w_in(w):
    def pad(a, width):
        return jnp.pad(a, ((0, 0), (0, width - a.shape[1])))
    a_end = A_W + 2 * N_HEADS
    b_end = a_end + B_W
    c_end = b_end + C_W
    d_end = c_end + D_W
    parts = [(w[:, 0:A_W], BF16), (pad(w[:, A_W:a_end], LANE), F32), (w[:, a_end:b_end], BF16),
             (w[:, b_end:c_end], BF16), (w[:, c_end:d_end], BF16), (pad(w[:, d_end:], LANE), F32)]
    return [(m.astype(BF16), dt) for m, dt in parts]


def kernel(x, norm_mix, w_in, a_conv, a_A_log, a_dt_bias, a_norm, b_lambda, b_norm, rel_bias,
           c_conv_w, c_conv_b, c_w_a, c_b_a, c_w_x, c_b_x, c_a_param, d_w_lr, d_b_lr, d_norm,
           w_out, norm_ffn, ffn_w_gate, ffn_w_up, ffn_w_down, moe_router, moe_w_gate, moe_w_up,
           moe_w_down, norm_final):
    batch, seq, _ = x.shape
    depth = w_in.shape[0]
    t = batch * seq
    xt = x.reshape(t, D_MODEL).astype(F32)
    bias_tiles = _attn_bias_tiles(rel_bias)
    row = lambda p: p.astype(F32)[None, :]
    out = None
    for l in range(depth):
        pa, pba, pb, pc, pd, plr = _inproj(xt, row(norm_mix[l]), _split_w_in(w_in[l]))
        lam_init = 0.8 - 0.6 * math.exp(-0.3 * l)
        mixes = (
            _gdn(pa, pba, a_conv[l], a_A_log[l], a_dt_bias[l], a_norm[l], batch, seq),
            _diff_attn(pb, b_lambda[l], lam_init, bias_tiles, b_norm[l], batch, seq),
            _rglru(pc, c_conv_w[l], c_conv_b[l], c_w_a[l], c_b_a[l], c_w_x[l], c_b_x[l],
                   c_a_param[l], batch, seq),
            _gla(pd, plr, d_w_lr[l], d_b_lr[l], d_norm[l], batch, seq),
        )
        wo = w_out[l].astype(BF16)
        if l % 2 == 0:
            j = l // 2
            xt = _outproj_ffn(xt, mixes, wo, row(norm_ffn[l]), ffn_w_gate[j].astype(BF16),
                              ffn_w_up[j].astype(BF16), ffn_w_down[j].astype(BF16))
            out = None
        else:
            j = l // 2
            final = l == depth - 1
            take = lambda a, i: a.at[i].get(mode='promise_in_bounds')
            x1, h, route = _outproj_router(xt, mixes, wo, row(norm_ffn[l]), moe_router[j])
            slot_tok, tile_e, n_active, dest, n_tiles = _route(route, t)
            xs = take(h, slot_tok)
            ys = _moe_experts(xs, tile_e, n_active, moe_w_gate[j], moe_w_up[j], moe_w_down[j], n_tiles)
            y = _moe_combine(x1, take(ys, dest[:, 0]), take(ys, dest[:, 1]), route, row(norm_final), final)
            if final:
                out = y
            else:
                xt = y
    if out is None:
        tm = min(FFN_TM, t)
        rowspec = pl.BlockSpec((tm, D_MODEL), lambda i: (i, 0))
        out = pl.pallas_call(
            _final_norm_kernel, name="final_norm", grid=(t // tm,),
            in_specs=[rowspec, pl.BlockSpec((1, D_MODEL), lambda i: (0, 0))],
            out_specs=rowspec, out_shape=jax.ShapeDtypeStruct((t, D_MODEL), F32),
            compiler_params=_cparams(("parallel",)),
        )(xt, row(norm_final))
    return out.reshape(batch, seq, D_MODEL).astype(x.dtype)
```

```python
import functools
import math

import jax
import jax.numpy as jnp
import numpy as np
from jax import lax
from jax.experimental import pallas as pl
from jax.experimental.pallas import tpu as pltpu

D_MODEL = 1024
CHUNK = 64
N_HEADS = 4
HEAD_DIM = 64
GROUP_WIDTH = 256
CONV_WIDTH = 4
DIFF_DH = 32
Q_BLOCK = 128
REL_BUCKETS = 32
REL_MAX_DIST = 128
RG_C = 8.0
GLA_DK = 32
GLA_RANK = 16
GLA_TAU = 16.0
D_FF = 2816
N_EXPERTS = 8
TOP_K = 2
D_FF_EXPERT = 3584
EPS = 1e-6
assert CHUNK == HEAD_DIM

LANE = 128
SUBLANE = 8
VMEM_LIMIT = 56 * 1024 * 1024

F32 = jnp.float32
BF16 = jnp.bfloat16
HI = lax.Precision.HIGHEST
NEG = -1e30
LOG2E = math.log2(math.e)

A_W = 1024
B_W = 768
C_W = 512
D_W = 768

PROJ_TM = 1024
MIX_TS = 256
GDN_BLK = 16
GDN_NB = 2
ATT_T = 128
ATT_VROWS = 80
FFN_TM = 512
ROUTER_TM = 1024
FFN_TF = 1408
MOE_TM = 1024
MOE_TF = 512


def _cparams(sem):
    return pltpu.CompilerParams(dimension_semantics=sem, vmem_limit_bytes=VMEM_LIMIT)


def _dot(a, b, precision=None):
    return jnp.dot(a, b, preferred_element_type=F32, precision=precision)


def _dot_nt(a, b, precision=None):
    return lax.dot_general(a, b, (((1,), (1,)), ((), ())), preferred_element_type=F32,
                           precision=precision)


def _dot_tn(a, b, precision=None):
    return lax.dot_general(a, b, (((0,), (0,)), ((), ())), preferred_element_type=F32,
                           precision=precision)


def _softplus(x):
    return jnp.maximum(x, 0.0) + jnp.log1p(jnp.exp(-jnp.abs(x)))


def _rms(x, g):
    return x * lax.rsqrt(jnp.mean(x * x, axis=-1, keepdims=True) + EPS) * g


def _causal_conv(x, tail, w):
    row = lax.broadcasted_iota(jnp.int32, (SUBLANE, x.shape[1]), 0)
    y = x * w[CONV_WIDTH - 1:CONV_WIDTH, :]
    for d in range(1, CONV_WIDTH):
        rolled = pltpu.roll(x, d, 0)
        first = jnp.where(row < d, pltpu.roll(tail, d, 0), rolled[:SUBLANE])
        shifted = jnp.concatenate([first, rolled[SUBLANE:]], axis=0)
        y = y + shifted * w[CONV_WIDTH - 1 - d:CONV_WIDTH - d, :]
    return y


def _head_mask(width, per_head):
    lane = lax.broadcasted_iota(jnp.int32, (1, width), 1)
    return [(lane // per_head) == h for h in range(N_HEADS)]


def _stack_heads(x, masks):
    return jnp.concatenate([jnp.where(m, x, 0.0) for m in masks], axis=0)


def _unstack_heads(r, masks, c):
    out = jnp.where(masks[0], r[0:c], 0.0)
    for h in range(1, N_HEADS):
        out = out + jnp.where(masks[h], r[h * c:(h + 1) * c], 0.0)
    return out


def _inproj_kernel(x_ref, g_ref, *refs):
    n = len(refs) // 2
    h = _rms(x_ref[...], g_ref[...]).astype(BF16)
    for w_ref, o_ref in zip(refs[:n], refs[n:]):
        o_ref[...] = _dot(h, w_ref[...]).astype(o_ref.dtype)


def _inproj(x, g, weights):
    t = x.shape[0]
    tm = min(PROJ_TM, t)
    row = lambda w: pl.BlockSpec((tm, w), lambda i: (i, 0))
    full = lambda a: pl.BlockSpec(a.shape, lambda i: (0, 0))
    ws = [w for w, _ in weights]
    return pl.pallas_call(
        _inproj_kernel,
        name="inproj",
        grid=(t // tm,),
        in_specs=[row(D_MODEL), full(g)] + [full(w) for w in ws],
        out_specs=[row(w.shape[1]) for w in ws],
        out_shape=[jax.ShapeDtypeStruct((t, w.shape[1]), dt) for w, dt in weights],
        compiler_params=_cparams(("parallel",)),
    )(x, g, *ws)


def _split2(x):
    hi = x.astype(BF16)
    return hi, (x - hi.astype(F32)).astype(BF16)


def _dot_x2(x, w):
    hi, lo = _split2(x)
    return _dot(hi, w) + _dot(lo, w)


def _bd_tile(x, bd):
    return jnp.concatenate([x] * N_HEADS, axis=0) * bd


def _dot_bd(a, bs, bd):
    ab = a.astype(BF16)
    return [_dot(ab, _bd_tile(b.astype(BF16), bd)) for b in bs]


def _gdn_kernel(a_ref, ba_ref, convw_ref, alog_ref, dtb_ref, gn_ref, bd_ref, lt_ref, e_ref,
                o_ref, tail_ref, state_ref, *, ts, nb):
    @pl.when(pl.program_id(1) == 0)
    def _init():
        tail_ref[...] = jnp.zeros_like(tail_ref)
        state_ref[...] = jnp.zeros_like(state_ref)

    c = CHUNK
    bd = bd_ref[...]
    lt = lt_ref[...]
    q, k, v, beta, gcum = [], [], [], [], []
    for b in range(nb):
        xin = a_ref[b, :, 0:768].astype(F32)
        y = _causal_conv(xin, tail_ref[b], convw_ref[...])
        tail_ref[b] = xin[ts - SUBLANE:ts, :]
        y = y * jax.nn.sigmoid(y)
        qb, kb_ = y[:, 0:256], y[:, 256:512]
        q.append(qb * lax.rsqrt(_dot_x2(qb * qb, bd) + EPS) * (HEAD_DIM ** -0.5))
        k.append(kb_ * lax.rsqrt(_dot_x2(kb_ * kb_, bd) + EPS))
        v.append(y[:, 512:768])
        e = _dot_x2(ba_ref[b], e_ref[...])
        beta.append(jax.nn.sigmoid(e[:, 0:256]))
        g = -jnp.exp(alog_ref[...]) * _softplus(e[:, 256:512] + dtb_ref[...])
        g_hi, g_lo = _split2(g)
        g_lo2 = (g - g_hi.astype(F32) - g_lo.astype(F32)).astype(BF16)
        gcum.append(_dot(lt, g_hi) + (_dot(lt, g_lo) + _dot(lt, g_lo2)))

    ri = lax.broadcasted_iota(jnp.int32, (c, GROUP_WIDTH), 0)
    cj = lax.broadcasted_iota(jnp.int32, (c, GROUP_WIDTH), 1) % HEAD_DIM
    causal = ri >= cj
    strict = ri > cj
    diag = ri == cj
    same_blk = (ri // GDN_BLK) == (cj // GDN_BLK)
    eye = diag.astype(F32)
    masks = _head_mask(GROUP_WIDTH, HEAD_DIM)
    items = [(b, slice(ci * c, (ci + 1) * c)) for ci in range(ts // c) for b in range(nb)]
    idx = range(len(items))

    a_qk, p0, nn, rhs_u, rhs_w = [], [], [], [], []
    for b, sl in items:
        kc, gc = k[b][sl], gcum[b][sl]
        kb = kc * beta[b][sl]
        grow = jnp.sum(jnp.where(diag, gc, 0.0), axis=0, keepdims=True)
        gamma = jnp.exp(jnp.where(causal, gc - grow, NEG))
        kst = _stack_heads(kc, masks).astype(BF16)
        aa = _dot_nt(jnp.concatenate([kb, q[b][sl]], axis=0).astype(BF16), kst)
        a_kk = jnp.where(strict, aa[0:c] * gamma, 0.0)
        a_qk.append(aa[c:2 * c] * gamma)
        p0.append(jnp.where(same_blk, -a_kk, 0.0))
        nn.append(jnp.where(same_blk, 0.0, a_kk))
        rhs_u.append(v[b][sl] * beta[b][sl])
        rhs_w.append(kb * jnp.exp(gc))
    t1 = [eye + p for p in p0]
    p1 = [_dot_bd(p0[i], [p0[i]], bd)[0] for i in idx]
    pr = [_dot_bd(p1[i], [p1[i], t1[i]], bd) for i in idx]
    p2 = [x[0] for x in pr]
    t2 = [t1[i] + pr[i][1] for i in idx]
    pr = [_dot_bd(p2[i], [p2[i], t2[i]], bd) for i in idx]
    p3 = [x[0] for x in pr]
    t3 = [t2[i] + pr[i][1] for i in idx]
    dinv = [t3[i] + _dot_bd(p3[i], [t3[i]], bd)[0] for i in idx]
    m1 = [_dot_bd(dinv[i], [nn[i]], bd)[0] for i in idx]
    m2 = [_dot_bd(m1[i], [m1[i]], bd)[0] for i in idx]
    im = [eye - m for m in m1]
    qq = [im[i] + _dot_bd(im[i], [m2[i]], bd)[0] for i in idx]
    inv = [_dot_bd(qq[i], [dinv[i]], bd)[0] for i in idx]
    uw = [_dot_bd(inv[i], [rhs_u[i], rhs_w[i]], bd) for i in idx]

    bdf = bd.astype(F32)
    state = [state_ref[b] for b in range(nb)]
    outs = [[] for _ in range(nb)]
    for i, (b, sl) in enumerate(items):
        gc = gcum[b][sl]
        g_last = gc[c - 1:c, :]
        u, w = uw[i]
        ws_qs = _dot(jnp.concatenate([w, q[b][sl] * jnp.exp(gc)], axis=0).astype(BF16), state[b].astype(BF16))
        v_new = (u - ws_qs[0:c]).astype(BF16)
        outs[b].append(ws_qs[c:2 * c] + _dot(a_qk[i].astype(BF16), _bd_tile(v_new, bd)))
        kd = k[b][sl] * jnp.exp(g_last - gc)
        state[b] = state[b] * jnp.exp(g_last) + _dot_tn(kd.astype(BF16), v_new) * bdf
    for b in range(nb):
        state_ref[b] = state[b]
        o = jnp.concatenate(outs[b], axis=0)
        o = o * lax.rsqrt(_dot_x2(o * o, bd) * (1.0 / HEAD_DIM) + EPS) * gn_ref[...]
        gate = a_ref[b, :, 768:1024].astype(F32)
        o_ref[b] = (o * (gate * jax.nn.sigmoid(gate))).astype(o_ref.dtype)


def _block_diag_ones(n, blk):
    i = np.arange(n)
    return jnp.asarray((i[:, None] // blk) == (i[None, :] // blk), F32)


def _gdn(pa, pba, conv_w, a_log, dt_bias, norm_g, batch, seq):
    ts = min(MIX_TS, seq)
    nst = seq // ts
    bd = _block_diag_ones(GROUP_WIDTH, HEAD_DIM).astype(BF16)
    r = np.arange(ts)
    lt = jnp.asarray((r[:, None] // CHUNK == r[None, :] // CHUNK) & (r[:, None] >= r[None, :]), BF16)
    lane = np.arange(GROUP_WIDTH)
    e = np.zeros((LANE, 2 * GROUP_WIDTH), np.float32)
    for h in range(N_HEADS):
        e[h, np.nonzero(lane // HEAD_DIM == h)[0]] = 1.0
        e[N_HEADS + h, GROUP_WIDTH + np.nonzero(lane // HEAD_DIM == h)[0]] = 1.0
    rep = lambda p: jnp.repeat(p.astype(F32), HEAD_DIM)[None, :]
    consts = [conv_w.astype(F32), rep(a_log), rep(dt_bias),
              jnp.tile(norm_g.astype(F32), N_HEADS)[None, :], bd, lt, jnp.asarray(e, BF16)]
    nb = GDN_NB if batch % GDN_NB == 0 else 1
    full = lambda a: pl.BlockSpec(a.shape, lambda b, s: (0, 0))
    out = pl.pallas_call(
        functools.partial(_gdn_kernel, ts=ts, nb=nb),
        name="gdn",
        grid=(batch // nb, nst),
        in_specs=[pl.BlockSpec((nb, ts, A_W), lambda b, s: (b, s, 0)),
                  pl.BlockSpec((nb, ts, LANE), lambda b, s: (b, s, 0))] + [full(a) for a in consts],
        out_specs=pl.BlockSpec((nb, ts, GROUP_WIDTH), lambda b, s: (b, s, 0)),
        out_shape=jax.ShapeDtypeStruct((batch, seq, GROUP_WIDTH), BF16),
        scratch_shapes=[pltpu.VMEM((nb, SUBLANE, 768), F32),
                        pltpu.VMEM((nb, GROUP_WIDTH, GROUP_WIDTH), F32)],
        compiler_params=_cparams(("parallel", "arbitrary")),
    )(pa.reshape(batch, seq, A_W), pba.reshape(batch, seq, LANE), *consts)
    return out.reshape(batch * seq, GROUP_WIDTH)


def _attn_kernel(qa_ref, qb_ref, k_ref, v_ref, bias_ref, lamv_ref, gn_ref, oa_ref, ob_ref,
                 va_ref, qt_ref, s_ref, p_ref, al_ref, m_ref, l_ref, acc_ref, *, lam_init, seq):
    t = ATT_T
    nq = seq // t
    i = pl.program_id(1)
    na = i + 1
    nlan = 2 * N_HEADS * t
    nsteps = nq + 1

    @pl.when(i == 0)
    def _stage_v():
        def body(j, carry):
            rows = pl.ds(pl.multiple_of(j * t, t), t)
            vt = v_ref[rows, :].astype(F32).T.astype(BF16)
            for h in range(N_HEADS):
                va_ref[j, h, 0:HEAD_DIM, :] = vt[h * HEAD_DIM:(h + 1) * HEAD_DIM, :]
                va_ref[j, h, HEAD_DIM:ATT_VROWS, :] = jnp.ones((ATT_VROWS - HEAD_DIM, t), BF16)
            return carry
        lax.fori_loop(0, seq // t, body, 0)

    feat = lax.broadcasted_iota(jnp.int32, (GROUP_WIDTH, t), 0) // DIFF_DH
    for w, q_ref in enumerate((qa_ref, qb_ref)):
        qt = (q_ref[...].astype(F32) * (DIFF_DH ** -0.5 * LOG2E)).T
        for idx in range(2 * N_HEADS):
            qt_ref[w, :, idx * t:(idx + 1) * t] = jnp.where(feat == idx, qt, 0.0).astype(BF16)
    m_ref[...] = jnp.full((2, 1, nlan), NEG, F32)
    l_ref[...] = jnp.zeros((2, 1, nlan), F32)
    acc_ref[...] = jnp.zeros((2, HEAD_DIM, nlan), F32)

    head_cols = [slice(2 * h * t, (2 * h + 2) * t) for h in range(N_HEADS)]

    def tile_of(s):
        if s >= nq // 2:
            bias = 2 if s == nq else (1 if s == nq - 1 else None)
            return 1, s - na, bias
        w = (s >= na).astype(jnp.int32)
        bias = jnp.where(w == 1, 0, jnp.where(s == na - 1, 2, jnp.where(s == na - 2, 1, 0)))
        return w, s - na * w, bias

    def scores(s, h):
        w, kt, _ = tile_of(s)
        rows = pl.ds(pl.multiple_of(kt * t, t), t)
        s_ref[s % 2, :, head_cols[h]] = _dot(k_ref[rows, :], qt_ref[w, :, head_cols[h]])

    def softmax(s, h):
        w, _, bias = tile_of(s)
        cols = head_cols[h]
        sc = s_ref[s % 2, :, cols]
        if bias is not None:
            sc = sc + bias_ref[bias, :, cols]
        m_prev = m_ref[w, :, cols]
        m_new = jnp.maximum(m_prev, jnp.max(sc, axis=0, keepdims=True))
        al_ref[s % 2, :, cols] = jnp.exp2(m_prev - m_new)
        m_ref[w, :, cols] = m_new
        p_ref[s % 2, :, cols] = jnp.exp2(sc - m_new).astype(BF16)

    def values(s, h):
        w, kt, _ = tile_of(s)
        cols = head_cols[h]
        alpha = al_ref[s % 2, :, cols]
        pv = _dot(va_ref[kt, h], p_ref[s % 2, :, cols])
        acc_ref[w, :, cols] = alpha * acc_ref[w, :, cols] + pv[0:HEAD_DIM]
        l_ref[w, :, cols] = alpha * l_ref[w, :, cols] + pv[HEAD_DIM:HEAD_DIM + 1]

    for s in range(nsteps + 2):
        for h in range(N_HEADS):
            if s < nsteps:
                scores(s, h)
            if 1 <= s <= nsteps:
                softmax(s - 1, h)
            if s >= 2:
                values(s - 2, h)

    lv = lamv_ref[...]
    lam = (jnp.exp(jnp.sum(lv[0:1] * lv[1:2], axis=1, keepdims=True))
           - jnp.exp(jnp.sum(lv[2:3] * lv[3:4], axis=1, keepdims=True)) + lam_init)
    for w, o_ref in enumerate((oa_ref, ob_ref)):
        inv_l = 1.0 / l_ref[w]
        outs = []
        for h in range(N_HEADS):
            c0 = slice(2 * h * t, (2 * h + 1) * t)
            c1 = slice((2 * h + 1) * t, (2 * h + 2) * t)
            oh = acc_ref[w, :, c0] * inv_l[:, c0] - lam * (acc_ref[w, :, c1] * inv_l[:, c1])
            oh = oh * lax.rsqrt(jnp.mean(oh * oh, axis=0, keepdims=True) + EPS)
            outs.append(oh)
        o = jnp.concatenate(outs, axis=0).T
        o_ref[...] = (o * gn_ref[...] * (1.0 - lam_init)).astype(o_ref.dtype)


def _t5_bucket(rel):
    nb = REL_BUCKETS // 2
    bucket = jnp.where(rel > 0, nb, 0)
    n = jnp.abs(rel)
    max_exact = nb // 2
    large = max_exact + (jnp.log(jnp.maximum(n, 1).astype(F32) / max_exact)
                         / math.log(REL_MAX_DIST / max_exact) * (nb - max_exact)).astype(jnp.int32)
    large = jnp.minimum(large, nb - 1)
    return bucket + jnp.where(n < max_exact, n, large)


def _attn_bias_tiles(rel_bias):
    t = ATT_T
    table = rel_bias.astype(F32)
    kk = jnp.arange(t)[:, None]
    qq = jnp.arange(t)[None, :]

    def expand(b):
        b = jnp.transpose(b, (0, 2, 1))
        b = jnp.broadcast_to(b[:, :, None, :], (t, N_HEADS, 2, t))
        return b.reshape(t, 2 * N_HEADS * t)

    table = table * LOG2E

    def lookup(bucket):
        onehot = (bucket[..., None] == jnp.arange(REL_BUCKETS)).astype(F32)
        return jnp.einsum('kqb,bh->kqh', onehot, table, precision=HI)

    diag = lookup(_t5_bucket(kk - qq))
    diag = jnp.where(((kk // CHUNK) <= (qq // CHUNK))[:, :, None], diag, NEG)
    near = lookup(_t5_bucket(kk - qq - t))
    far = lookup(_t5_bucket(jnp.full((1, 1), -(REL_MAX_DIST + 1), jnp.int32)))
    far = jnp.broadcast_to(far, (t, t, N_HEADS))
    return jnp.stack([jnp.zeros((t, 2 * N_HEADS * t), F32), expand(near - far), expand(diag - far)])


def _diff_attn(pb, lam_vecs, lam_init, bias_tiles, norm_g, batch, seq):
    t = ATT_T
    nq = seq // t
    gn = jnp.tile(norm_g.astype(F32), N_HEADS)[None, :]
    lamv = lam_vecs.astype(F32)
    nlan = 2 * N_HEADS * t
    assert nq % 2 == 0
    nh = nq // 2
    full = lambda a: pl.BlockSpec(a.shape, lambda b, i: (0,) * a.ndim)
    half = jax.ShapeDtypeStruct((batch * nh * t, GROUP_WIDTH), BF16)
    lo, hi = pl.pallas_call(
        functools.partial(_attn_kernel, lam_init=lam_init, seq=seq),
        name="diffattn",
        grid=(batch, nh),
        in_specs=[pl.BlockSpec((t, GROUP_WIDTH), lambda b, i: (b * nq + i, 0)),
                  pl.BlockSpec((t, GROUP_WIDTH), lambda b, i: (b * nq + nq - 1 - i, 0)),
                  pl.BlockSpec((seq, GROUP_WIDTH), lambda b, i: (b, 1)),
                  pl.BlockSpec((seq, GROUP_WIDTH), lambda b, i: (b, 2)),
                  full(bias_tiles), full(lamv), full(gn)],
        out_specs=[pl.BlockSpec((t, GROUP_WIDTH), lambda b, i: (b * nh + i, 0)),
                   pl.BlockSpec((t, GROUP_WIDTH), lambda b, i: (b * nh + nh - 1 - i, 0))],
        out_shape=[half, half],
        scratch_shapes=[pltpu.VMEM((seq // t, N_HEADS, ATT_VROWS, t), BF16),
                        pltpu.VMEM((2, GROUP_WIDTH, nlan), BF16),
                        pltpu.VMEM((2, t, nlan), F32),
                        pltpu.VMEM((2, t, nlan), BF16),
                        pltpu.VMEM((2, 1, nlan), F32),
                        pltpu.VMEM((2, 1, nlan), F32), pltpu.VMEM((2, 1, nlan), F32),
                        pltpu.VMEM((2, HEAD_DIM, nlan), F32)],
        compiler_params=_cparams(("parallel", "arbitrary")),
    )(pb, pb, pb, pb, bias_tiles, lamv, gn)
    out = jnp.concatenate([lo.reshape(batch, nh * t, GROUP_WIDTH), hi.reshape(batch, nh * t, GROUP_WIDTH)],
                          axis=1)
    return out.reshape(batch * seq, GROUP_WIDTH)


def _rglru_kernel(c_ref, convw_ref, convb_ref, wa_ref, ba_ref, wx_ref, bx_ref, ap_ref, o_ref,
                  tail_ref, h_ref, *, ts):
    @pl.when(pl.program_id(1) == 0)
    def _init():
        tail_ref[...] = jnp.zeros_like(tail_ref)
        h_ref[...] = jnp.zeros_like(h_ref)

    xb = c_ref[:, 0:256].astype(F32)
    gb = c_ref[:, 256:512].astype(F32)
    xc = _causal_conv(xb, tail_ref[...], convw_ref[...]) + convb_ref[...]
    tail_ref[...] = xb[ts - SUBLANE:ts, :]
    xcb = xc.astype(BF16)
    gate_a = jax.nn.sigmoid(_dot(xcb, wa_ref[...]) + ba_ref[...])
    gate_x = jax.nn.sigmoid(_dot(xcb, wx_ref[...]) + bx_ref[...])
    log_a = -RG_C * gate_a * _softplus(ap_ref[...])
    a = jnp.exp(log_a)
    th = jnp.tanh(log_a)
    u = xc * gate_x * jnp.sqrt(-2.0 * th / (1.0 - th))
    row = lax.broadcasted_iota(jnp.int32, (ts, GROUP_WIDTH), 0)
    d = 1
    while d < ts:
        keep = row >= d
        a_sh = jnp.where(keep, pltpu.roll(a, d, 0), 1.0)
        u_sh = jnp.where(keep, pltpu.roll(u, d, 0), 0.0)
        u = u + a * u_sh
        a = a * a_sh
        d *= 2
    h = u + a * h_ref[...]
    h_ref[...] = h[ts - 1:ts, :]
    gelu = 0.5 * gb * (1.0 + jnp.tanh(math.sqrt(2.0 / math.pi) * (gb + 0.044715 * (gb * gb * gb))))
    o_ref[...] = (h * gelu).astype(o_ref.dtype)


def _block_diag_weight(w):
    nb, wi, wo = w.shape
    out = jnp.zeros((nb * wi, nb * wo), w.dtype)
    for i in range(nb):
        out = out.at[i * wi:(i + 1) * wi, i * wo:(i + 1) * wo].set(w[i])
    return out


def _rglru(pc, conv_w, conv_b, w_a, b_a, w_x, b_x, a_param, batch, seq):
    ts = min(MIX_TS, seq)
    nst = seq // ts
    r = lambda p: p.astype(F32)[None, :]
    consts = [conv_w.astype(F32), r(conv_b), _block_diag_weight(w_a).astype(BF16), r(b_a),
              _block_diag_weight(w_x).astype(BF16), r(b_x), r(a_param)]
    full = lambda a: pl.BlockSpec(a.shape, lambda b, s: (0, 0))
    return pl.pallas_call(
        functools.partial(_rglru_kernel, ts=ts),
        name="rglru",
        grid=(batch, nst),
        in_specs=[pl.BlockSpec((ts, C_W), lambda b, s: (b * nst + s, 0))] + [full(a) for a in consts],
        out_specs=pl.BlockSpec((ts, GROUP_WIDTH), lambda b, s: (b * nst + s, 0)),
        out_shape=jax.ShapeDtypeStruct((batch * seq, GROUP_WIDTH), BF16),
        scratch_shapes=[pltpu.VMEM((SUBLANE, GROUP_WIDTH), F32), pltpu.VMEM((1, GROUP_WIDTH), F32)],
        compiler_params=_cparams(("parallel", "arbitrary")),
    )(pc, *consts)


def _gla_kernel(d_ref, lr_ref, wlr_ref, blr_ref, gn_ref, bd_ref, bdt_ref, lt_ref, o_ref, state_ref,
                *, ts, nb):
    @pl.when(pl.program_id(1) == 0)
    def _init():
        state_ref[...] = jnp.zeros_like(state_ref)

    c = CHUNK
    bd = bd_ref[...]
    bdt = bdt_ref[...]
    lt = lt_ref[...]
    kmasks = _head_mask(N_HEADS * GLA_DK, GLA_DK)
    ri = lax.broadcasted_iota(jnp.int32, (c, GROUP_WIDTH), 0)
    cj = lax.broadcasted_iota(jnp.int32, (c, GROUP_WIDTH), 1) % HEAD_DIM
    causal = ri >= cj
    q, k, gcum = [], [], []
    for b in range(nb):
        q.append(d_ref[b, :, 0:128].astype(F32) * (GLA_DK ** -0.5))
        k.append(d_ref[b, :, 128:256].astype(F32))
        lr_hi, lr_lo = _split2(lr_ref[b])
        w_hi, w_lo = _split2(wlr_ref[...])
        z = _dot(lr_hi, w_hi) + (_dot(lr_hi, w_lo) + _dot(lr_lo, w_hi)) + blr_ref[...]
        la = (jnp.minimum(z, 0.0) - jnp.log1p(jnp.exp(-jnp.abs(z)))) * (1.0 / GLA_TAU)
        la_hi, la_lo = _split2(la)
        la_lo2 = (la - la_hi.astype(F32) - la_lo.astype(F32)).astype(BF16)
        gcum.append(_dot(lt, la_hi) + (_dot(lt, la_lo) + _dot(lt, la_lo2)))

    items = [(b, slice(ci * c, (ci + 1) * c)) for ci in range(ts // c) for b in range(nb)]
    o_intra, upd = [], []
    for b, sl in items:
        qc, kc, gc = q[b][sl], k[b][sl], gcum[b][sl]
        vc = d_ref[b, sl, 256:512]
        ref = gc[c // 2:c // 2 + 1, :]
        kst = _stack_heads(kc * jnp.exp(ref - gc), kmasks).astype(BF16)
        a_in = _dot_nt((qc * jnp.exp(gc - ref)).astype(BF16), kst)
        a_in = jnp.where(causal, a_in, 0.0).astype(BF16)
        o_intra.append(_dot(a_in, _bd_tile(vc, bd)))
        kd = kc * jnp.exp(gc[c - 1:c, :] - gc)
        upd.append(_dot_tn(vc, kd.astype(BF16)) * bdt)

    state = [state_ref[b] for b in range(nb)]
    outs = [[] for _ in range(nb)]
    for i, (b, sl) in enumerate(items):
        gc = gcum[b][sl]
        o_inter = _dot_nt((q[b][sl] * jnp.exp(gc)).astype(BF16), state[b].astype(BF16))
        outs[b].append(o_intra[i] + o_inter)
        state[b] = state[b] * jnp.exp(gc[c - 1:c, :]) + upd[i]
    for b in range(nb):
        state_ref[b] = state[b]
        o = jnp.concatenate(outs[b], axis=0)
        o = o * lax.rsqrt(_dot_x2(o * o, bd) * (1.0 / HEAD_DIM) + EPS) * gn_ref[...]
        rt = d_ref[b, :, 512:768].astype(F32)
        o_ref[b] = (o * (rt * jax.nn.sigmoid(rt))).astype(o_ref.dtype)


def _gla(pd, plr, w_lr, b_lr, norm_g, batch, seq):
    ts = min(MIX_TS, seq)
    nst = seq // ts
    kw = N_HEADS * GLA_DK
    wlr = jnp.zeros((LANE, kw), F32).at[0:GLA_RANK, :].set(w_lr.astype(F32))
    bd = _block_diag_ones(GROUP_WIDTH, HEAD_DIM).astype(BF16)
    iv = np.arange(GROUP_WIDTH)[:, None] // HEAD_DIM
    ik = np.arange(kw)[None, :] // GLA_DK
    bdt = jnp.asarray(iv == ik, F32)
    r = np.arange(ts)
    lt = jnp.asarray((r[:, None] // CHUNK == r[None, :] // CHUNK) & (r[:, None] >= r[None, :]), BF16)
    consts = [wlr, b_lr.astype(F32)[None, :], jnp.tile(norm_g.astype(F32), N_HEADS)[None, :], bd, bdt, lt]
    nb = GDN_NB if batch % GDN_NB == 0 else 1
    full = lambda a: pl.BlockSpec(a.shape, lambda b, s: (0, 0))
    out = pl.pallas_call(
        functools.partial(_gla_kernel, ts=ts, nb=nb),
        name="gla",
        grid=(batch // nb, nst),
        in_specs=[pl.BlockSpec((nb, ts, D_W), lambda b, s: (b, s, 0)),
                  pl.BlockSpec((nb, ts, LANE), lambda b, s: (b, s, 0))] + [full(a) for a in consts],
        out_specs=pl.BlockSpec((nb, ts, GROUP_WIDTH), lambda b, s: (b, s, 0)),
        out_shape=jax.ShapeDtypeStruct((batch, seq, GROUP_WIDTH), BF16),
        scratch_shapes=[pltpu.VMEM((nb, GROUP_WIDTH, kw), F32)],
        compiler_params=_cparams(("parallel", "arbitrary")),
    )(pd.reshape(batch, seq, D_W), plr.reshape(batch, seq, LANE), *consts)
    return out.reshape(batch * seq, GROUP_WIDTH)


def _mix_outproj(x_ref, ma_ref, mb_ref, mc_ref, md_ref, wo_ref):
    mix = jnp.concatenate([ma_ref[...], mb_ref[...], mc_ref[...], md_ref[...]], axis=1)
    return x_ref[...] + _dot(mix, wo_ref[...])


def _ffn_kernel(x_ref, ma_ref, mb_ref, mc_ref, md_ref, wo_ref, g_ref, wg_ref, wu_ref, wd_ref,
                o_ref, h_ref, acc_ref):
    f = pl.program_id(1)

    @pl.when(f == 0)
    def _first():
        x1 = _mix_outproj(x_ref, ma_ref, mb_ref, mc_ref, md_ref, wo_ref)
        acc_ref[...] = x1
        h_ref[...] = _rms(x1, g_ref[...]).astype(BF16)

    h = h_ref[...]
    gt = _dot(h, wg_ref[...])
    act = (gt * jax.nn.sigmoid(gt) * _dot(h, wu_ref[...])).astype(BF16)
    acc_ref[...] += _dot(act, wd_ref[...])

    @pl.when(f == pl.num_programs(1) - 1)
    def _last():
        o_ref[...] = acc_ref[...]


def _outproj_ffn(x, mixes, w_out, g, w_gate, w_up, w_down):
    t = x.shape[0]
    tm = min(FFN_TM, t)
    tf = FFN_TF
    nf = D_FF // tf
    row = lambda w: pl.BlockSpec((tm, w), lambda i, f: (i, 0))
    return pl.pallas_call(
        _ffn_kernel,
        name="outproj_ffn",
        grid=(t // tm, nf),
        in_specs=[row(D_MODEL)] + [row(GROUP_WIDTH)] * 4 + [
            pl.BlockSpec((D_MODEL, D_MODEL), lambda i, f: (0, 0)),
            pl.BlockSpec((1, D_MODEL), lambda i, f: (0, 0)),
            pl.BlockSpec((D_MODEL, tf), lambda i, f: (0, f)),
            pl.BlockSpec((D_MODEL, tf), lambda i, f: (0, f)),
            pl.BlockSpec((tf, D_MODEL), lambda i, f: (f, 0))],
        out_specs=row(D_MODEL),
        out_shape=jax.ShapeDtypeStruct((t, D_MODEL), F32),
        scratch_shapes=[pltpu.VMEM((tm, D_MODEL), BF16), pltpu.VMEM((tm, D_MODEL), F32)],
        compiler_params=_cparams(("parallel", "arbitrary")),
    )(x, *mixes, w_out, g, w_gate, w_up, w_down)


def _router_kernel(x_ref, ma_ref, mb_ref, mc_ref, md_ref, wo_ref, g_ref, wrh_ref, wrl_ref,
                   x1_ref, h_ref, r_ref):
    x1 = _mix_outproj(x_ref, ma_ref, mb_ref, mc_ref, md_ref, wo_ref)
    x1_ref[...] = x1
    h = _rms(x1, g_ref[...])
    h_hi, h_lo = _split2(h)
    h_ref[...] = h_hi
    lane = lax.broadcasted_iota(jnp.int32, (x1.shape[0], LANE), 1)
    logits = _dot(h_hi, wrh_ref[...]) + (_dot(h_hi, wrl_ref[...]) + _dot(h_lo, wrh_ref[...]))
    logits = jnp.where(lane < N_EXPERTS, logits, NEG)
    m1 = jnp.max(logits, axis=1, keepdims=True)
    e1 = jnp.min(jnp.where(logits == m1, lane, LANE), axis=1, keepdims=True)
    rest = jnp.where(lane == e1, NEG, logits)
    m2 = jnp.max(rest, axis=1, keepdims=True)
    e2 = jnp.min(jnp.where(rest == m2, lane, LANE), axis=1, keepdims=True)
    ex = jnp.exp(m2 - m1)
    w1 = 1.0 / (1.0 + ex)
    w2 = ex / (1.0 + ex)
    r_ref[...] = jnp.where(lane == 0, e1.astype(F32),
                           jnp.where(lane == 1, e2.astype(F32),
                                     jnp.where(lane == 2, w1, jnp.where(lane == 3, w2, 0.0))))


def _outproj_router(x, mixes, w_out, g, w_router):
    t = x.shape[0]
    tm = min(ROUTER_TM, t)
    wr = jnp.pad(w_router.astype(F32), ((0, 0), (0, LANE - N_EXPERTS)))
    wr_hi = wr.astype(BF16)
    wr_lo = (wr - wr_hi.astype(F32)).astype(BF16)
    row = lambda w: pl.BlockSpec((tm, w), lambda i: (i, 0))
    full = lambda a: pl.BlockSpec(a.shape, lambda i: (0, 0))
    return pl.pallas_call(
        _router_kernel,
        name="outproj_router",
        grid=(t // tm,),
        in_specs=[row(D_MODEL)] + [row(GROUP_WIDTH)] * 4 + [full(w_out), full(g), full(wr_hi), full(wr_lo)],
        out_specs=[row(D_MODEL), row(D_MODEL), row(LANE)],
        out_shape=[jax.ShapeDtypeStruct((t, D_MODEL), F32), jax.ShapeDtypeStruct((t, D_MODEL), BF16),
                   jax.ShapeDtypeStruct((t, LANE), F32)],
        compiler_params=_cparams(("parallel",)),
    )(x, *mixes, w_out, g, wr_hi, wr_lo)


def _moe_kernel(te_ref, na_ref, x_ref, wg_ref, wu_ref, wd_ref, o_ref, acc_ref):
    i = pl.program_id(0)
    f = pl.program_id(1)

    @pl.when(i < na_ref[0])
    def _active():
        @pl.when(f == 0)
        def _zero():
            acc_ref[...] = jnp.zeros_like(acc_ref)

        x = x_ref[...]
        gt = _dot(x, wg_ref[...].astype(BF16))
        act = (gt * jax.nn.sigmoid(gt) * _dot(x, wu_ref[...].astype(BF16))).astype(BF16)
        acc_ref[...] += _dot(act, wd_ref[...].astype(BF16))

        @pl.when(f == pl.num_programs(1) - 1)
        def _last():
            o_ref[...] = acc_ref[...]

    @pl.when(jnp.logical_and(i >= na_ref[0], f == pl.num_programs(1) - 1))
    def _unused_tile():
        o_ref[...] = jnp.zeros_like(o_ref)


def _moe_experts(xs, tile_e, n_active, w_gate, w_up, w_down, n_tiles):
    tm, tf = MOE_TM, MOE_TF
    nf = D_FF_EXPERT // tf

    def tile(i, na):
        return jnp.minimum(i, na[0] - 1)

    def fidx(i, f, na):
        return jnp.where(i < na[0], f, nf - 1)

    grid_spec = pltpu.PrefetchScalarGridSpec(
        num_scalar_prefetch=2,
        grid=(n_tiles, nf),
        in_specs=[
            pl.BlockSpec((tm, D_MODEL), lambda i, f, te, na: (tile(i, na), 0)),
            pl.BlockSpec((None, D_MODEL, tf), lambda i, f, te, na: (te[tile(i, na)], 0, fidx(i, f, na))),
            pl.BlockSpec((None, D_MODEL, tf), lambda i, f, te, na: (te[tile(i, na)], 0, fidx(i, f, na))),
            pl.BlockSpec((None, tf, D_MODEL), lambda i, f, te, na: (te[tile(i, na)], fidx(i, f, na), 0)),
        ],
        out_specs=pl.BlockSpec((tm, D_MODEL), lambda i, f, te, na: (i, 0)),
        scratch_shapes=[pltpu.VMEM((tm, D_MODEL), F32)],
    )
    return pl.pallas_call(
        _moe_kernel,
        name="moe_experts",
        grid_spec=grid_spec,
        out_shape=jax.ShapeDtypeStruct((n_tiles * tm, D_MODEL), F32),
        compiler_params=_cparams(("arbitrary", "arbitrary")),
    )(tile_e, n_active, xs, w_gate, w_up, w_down)


def _route(route, t):
    tm = MOE_TM
    n_assign = t * TOP_K
    flat_e = jnp.concatenate([route[:, k].astype(jnp.int32) for k in range(TOP_K)])
    onehot = (flat_e[:, None] == jnp.arange(N_EXPERTS)[None, :]).astype(jnp.int32)
    counts = jnp.sum(onehot, axis=0)
    padded = (counts + tm - 1) // tm * tm
    pad_end = jnp.cumsum(padded)
    pad_start = pad_end - padded
    grp_start = jnp.cumsum(counts) - counts
    dest = jnp.sum(onehot * (jnp.cumsum(onehot, axis=0) - 1 + pad_start[None, :]), axis=1)
    n_tiles = -(-n_assign // tm) + N_EXPERTS
    tile_e = jnp.minimum(jnp.searchsorted(pad_end, jnp.arange(n_tiles) * tm, side='right'),
                         N_EXPERTS - 1).astype(jnp.int32)
    n_active = (pad_end[-1] // tm).astype(jnp.int32).reshape(1)
    order = jnp.argsort(flat_e, stable=True).astype(jnp.int32)
    rank_s = ((jnp.arange(n_tiles, dtype=jnp.int32) * tm - pad_start[tile_e])[:, None]
              + jnp.arange(tm, dtype=jnp.int32)[None, :])
    src = jnp.minimum(grp_start[tile_e][:, None] + rank_s, n_assign - 1)
    slot = jnp.arange(n_tiles * tm, dtype=jnp.int32).reshape(n_tiles, tm)
    slot_tok = jnp.where(rank_s < counts[tile_e][:, None], order[src] % t, slot % t).reshape(n_tiles * tm)
    return slot_tok, tile_e, n_active, [dest[k * t:(k + 1) * t] for k in range(TOP_K)], n_tiles


def _combine_kernel(x_ref, y0_ref, y1_ref, r_ref, g_ref, o_ref, *, final):
    r = r_ref[...]
    y = x_ref[...] + (y0_ref[...] * r[:, TOP_K:TOP_K + 1] + y1_ref[...] * r[:, TOP_K + 1:TOP_K + 2])
    o_ref[...] = _rms(y, g_ref[...]) if final else y


def _moe_combine(x1, y0, y1, route, g, final):
    t = x1.shape[0]
    tm = min(FFN_TM, t)
    row = pl.BlockSpec((tm, D_MODEL), lambda i: (i, 0))
    return pl.pallas_call(
        functools.partial(_combine_kernel, final=final),
        name="moe_combine",
        grid=(t // tm,),
        in_specs=[row, row, row, pl.BlockSpec((tm, LANE), lambda i: (i, 0)),
                  pl.BlockSpec((1, D_MODEL), lambda i: (0, 0))],
        out_specs=row,
        out_shape=jax.ShapeDtypeStruct((t, D_MODEL), F32),
        compiler_params=_cparams(("parallel",)),
    )(x1, y0, y1, route, g)


def _final_norm_kernel(x_ref, g_ref, o_ref):
    o_ref[...] = _rms(x_ref[...], g_ref[...])


def _split_w_in(w):
    def pad(a, width):
        return jnp.pad(a, ((0, 0), (0, width - a.shape[1])))
    a_end = A_W + 2 * N_HEADS
    b_end = a_end + B_W
    c_end = b_end + C_W
    d_end = c_end + D_W
    parts = [(w[:, 0:A_W], BF16), (pad(w[:, A_W:a_end], LANE), F32), (w[:, a_end:b_end], BF16),
             (w[:, b_end:c_end], BF16), (w[:, c_end:d_end], BF16), (pad(w[:, d_end:], LANE), F32)]
    return [(m.astype(BF16), dt) for m, dt in parts]


def kernel(x, norm_mix, w_in, a_conv, a_A_log, a_dt_bias, a_norm, b_lambda, b_norm, rel_bias,
           c_conv_w, c_conv_b, c_w_a, c_b_a, c_w_x, c_b_x, c_a_param, d_w_lr, d_b_lr, d_norm,
           w_out, norm_ffn, ffn_w_gate, ffn_w_up, ffn_w_down, moe_router, moe_w_gate, moe_w_up,
           moe_w_down, norm_final):
    batch, seq, _ = x.shape
    depth = w_in.shape[0]
    t = batch * seq
    xt = x.reshape(t, D_MODEL).astype(F32)
    bias_tiles = _attn_bias_tiles(rel_bias)
    row = lambda p: p.astype(F32)[None, :]
    out = None
    for l in range(depth):
        pa, pba, pb, pc, pd, plr = _inproj(xt, row(norm_mix[l]), _split_w_in(w_in[l]))
        lam_init = 0.8 - 0.6 * math.exp(-0.3 * l)
        mixes = (
            _gdn(pa, pba, a_conv[l], a_A_log[l], a_dt_bias[l], a_norm[l], batch, seq),
            _diff_attn(pb, b_lambda[l], lam_init, bias_tiles, b_norm[l], batch, seq),
            _rglru(pc, c_conv_w[l], c_conv_b[l], c_w_a[l], c_b_a[l], c_w_x[l], c_b_x[l],
                   c_a_param[l], batch, seq),
            _gla(pd, plr, d_w_lr[l], d_b_lr[l], d_norm[l], batch, seq),
        )
        wo = w_out[l].astype(BF16)
        if l % 2 == 0:
            j = l // 2
            xt = _outproj_ffn(xt, mixes, wo, row(norm_ffn[l]), ffn_w_gate[j].astype(BF16),
                              ffn_w_up[j].astype(BF16), ffn_w_down[j].astype(BF16))
            out = None
        else:
            j = l // 2
            final = l == depth - 1
            take = lambda a, i: a.at[i].get(mode='promise_in_bounds')
            x1, h, route = _outproj_router(xt, mixes, wo, row(norm_ffn[l]), moe_router[j])
            slot_tok, tile_e, n_active, dest, n_tiles = _route(route, t)
            xs = take(h, slot_tok)
            ys = _moe_experts(xs, tile_e, n_active, moe_w_gate[j], moe_w_up[j], moe_w_down[j], n_tiles)
            y = _moe_combine(x1, take(ys, dest[0]), take(ys, dest[1]), route, row(norm_final), final)
            if final:
                out = y
            else:
                xt = y
    if out is None:
        tm = min(FFN_TM, t)
        rowspec = pl.BlockSpec((tm, D_MODEL), lambda i: (i, 0))
        out = pl.pallas_call(
            _final_norm_kernel, name="final_norm", grid=(t // tm,),
            in_specs=[rowspec, pl.BlockSpec((1, D_MODEL), lambda i: (0, 0))],
            out_specs=rowspec, out_shape=jax.ShapeDtypeStruct((t, D_MODEL), F32),
            compiler_params=_cparams(("parallel",)),
        )(xt, row(norm_final))
    return out.reshape(batch, seq, D_MODEL).astype(x.dtype)
```

```python
import functools
import math

import jax
import jax.numpy as jnp
import numpy as np
from jax import lax
from jax.experimental import pallas as pl
from jax.experimental.pallas import tpu as pltpu

D_MODEL = 1024
CHUNK = 64
N_HEADS = 4
HEAD_DIM = 64
GROUP_WIDTH = 256
CONV_WIDTH = 4
DIFF_DH = 32
Q_BLOCK = 128
REL_BUCKETS = 32
REL_MAX_DIST = 128
RG_C = 8.0
GLA_DK = 32
GLA_RANK = 16
GLA_TAU = 16.0
D_FF = 2816
N_EXPERTS = 8
TOP_K = 2
D_FF_EXPERT = 3584
EPS = 1e-6
assert CHUNK == HEAD_DIM

LANE = 128
SUBLANE = 8
VMEM_LIMIT = 56 * 1024 * 1024

F32 = jnp.float32
BF16 = jnp.bfloat16
HI = lax.Precision.HIGHEST
NEG = -1e30
LOG2E = math.log2(math.e)

A_W = 1024
B_W = 768
C_W = 512
D_W = 768

PROJ_TM = 1024
MIX_TS = 256
GDN_BLK = 16
GDN_NB = 2
ATT_T = 128
ATT_VROWS = 80
FFN_TM = 512
ROUTER_TM = 1024
FFN_TF = 1408
MOE_TM = 1024
MOE_TF = 512


def _cparams(sem):
    return pltpu.CompilerParams(dimension_semantics=sem, vmem_limit_bytes=VMEM_LIMIT)


def _dot(a, b, precision=None):
    return jnp.dot(a, b, preferred_element_type=F32, precision=precision)


def _dot_nt(a, b, precision=None):
    return lax.dot_general(a, b, (((1,), (1,)), ((), ())), preferred_element_type=F32,
                           precision=precision)


def _dot_tn(a, b, precision=None):
    return lax.dot_general(a, b, (((0,), (0,)), ((), ())), preferred_element_type=F32,
                           precision=precision)


def _softplus(x):
    return jnp.maximum(x, 0.0) + jnp.log1p(jnp.exp(-jnp.abs(x)))


def _rms(x, g):
    return x * lax.rsqrt(jnp.mean(x * x, axis=-1, keepdims=True) + EPS) * g


def _causal_conv(x, tail, w):
    row = lax.broadcasted_iota(jnp.int32, (SUBLANE, x.shape[1]), 0)
    y = x * w[CONV_WIDTH - 1:CONV_WIDTH, :]
    for d in range(1, CONV_WIDTH):
        rolled = pltpu.roll(x, d, 0)
        first = jnp.where(row < d, pltpu.roll(tail, d, 0), rolled[:SUBLANE])
        shifted = jnp.concatenate([first, rolled[SUBLANE:]], axis=0)
        y = y + shifted * w[CONV_WIDTH - 1 - d:CONV_WIDTH - d, :]
    return y


def _head_mask(width, per_head):
    lane = lax.broadcasted_iota(jnp.int32, (1, width), 1)
    return [(lane // per_head) == h for h in range(N_HEADS)]


def _stack_heads(x, masks):
    return jnp.concatenate([jnp.where(m, x, 0.0) for m in masks], axis=0)


def _unstack_heads(r, masks, c):
    out = jnp.where(masks[0], r[0:c], 0.0)
    for h in range(1, N_HEADS):
        out = out + jnp.where(masks[h], r[h * c:(h + 1) * c], 0.0)
    return out


def _inproj_kernel(x_ref, g_ref, *refs):
    n = len(refs) // 2
    h = _rms(x_ref[...], g_ref[...]).astype(BF16)
    for w_ref, o_ref in zip(refs[:n], refs[n:]):
        o_ref[...] = _dot(h, w_ref[...]).astype(o_ref.dtype)


def _inproj(x, g, weights):
    t = x.shape[0]
    tm = min(PROJ_TM, t)
    row = lambda w: pl.BlockSpec((tm, w), lambda i: (i, 0))
    full = lambda a: pl.BlockSpec(a.shape, lambda i: (0, 0))
    ws = [w for w, _ in weights]
    return pl.pallas_call(
        _inproj_kernel,
        name="inproj",
        grid=(t // tm,),
        in_specs=[row(D_MODEL), full(g)] + [full(w) for w in ws],
        out_specs=[row(w.shape[1]) for w in ws],
        out_shape=[jax.ShapeDtypeStruct((t, w.shape[1]), dt) for w, dt in weights],
        compiler_params=_cparams(("parallel",)),
    )(x, g, *ws)


def _split2(x):
    hi = x.astype(BF16)
    return hi, (x - hi.astype(F32)).astype(BF16)


def _dot_x2(x, w):
    hi, lo = _split2(x)
    return _dot(hi, w) + _dot(lo, w)


def _bd_tile(x, bd):
    return jnp.concatenate([x] * N_HEADS, axis=0) * bd


def _dot_bd(a, bs, bd):
    ab = a.astype(BF16)
    return [_dot(ab, _bd_tile(b.astype(BF16), bd)) for b in bs]


def _gdn_kernel(a_ref, ba_ref, convw_ref, alog_ref, dtb_ref, gn_ref, bd_ref, lt_ref, e_ref,
                o_ref, tail_ref, state_ref, *, ts, nb):
    @pl.when(pl.program_id(1) == 0)
    def _init():
        tail_ref[...] = jnp.zeros_like(tail_ref)
        state_ref[...] = jnp.zeros_like(state_ref)

    c = CHUNK
    bd = bd_ref[...]
    lt = lt_ref[...]
    q, k, v, beta, gcum = [], [], [], [], []
    for b in range(nb):
        xin = a_ref[b, :, 0:768].astype(F32)
        y = _causal_conv(xin, tail_ref[b], convw_ref[...])
        tail_ref[b] = xin[ts - SUBLANE:ts, :]
        y = y * jax.nn.sigmoid(y)
        qb, kb_ = y[:, 0:256], y[:, 256:512]
        q.append(qb * lax.rsqrt(_dot_x2(qb * qb, bd) + EPS) * (HEAD_DIM ** -0.5))
        k.append(kb_ * lax.rsqrt(_dot_x2(kb_ * kb_, bd) + EPS))
        v.append(y[:, 512:768])
        e = _dot_x2(ba_ref[b], e_ref[...])
        beta.append(jax.nn.sigmoid(e[:, 0:256]))
        g = -jnp.exp(alog_ref[...]) * _softplus(e[:, 256:512] + dtb_ref[...])
        g_hi, g_lo = _split2(g)
        g_lo2 = (g - g_hi.astype(F32) - g_lo.astype(F32)).astype(BF16)
        gcum.append(_dot(lt, g_hi) + (_dot(lt, g_lo) + _dot(lt, g_lo2)))

    ri = lax.broadcasted_iota(jnp.int32, (c, GROUP_WIDTH), 0)
    cj = lax.broadcasted_iota(jnp.int32, (c, GROUP_WIDTH), 1) % HEAD_DIM
    causal = ri >= cj
    strict = ri > cj
    diag = ri == cj
    same_blk = (ri // GDN_BLK) == (cj // GDN_BLK)
    eye = diag.astype(F32)
    masks = _head_mask(GROUP_WIDTH, HEAD_DIM)
    items = [(b, slice(ci * c, (ci + 1) * c)) for ci in range(ts // c) for b in range(nb)]
    idx = range(len(items))

    a_qk, p0, nn, rhs_u, rhs_w = [], [], [], [], []
    for b, sl in items:
        kc, gc = k[b][sl], gcum[b][sl]
        kb = kc * beta[b][sl]
        grow = jnp.sum(jnp.where(diag, gc, 0.0), axis=0, keepdims=True)
        gamma = jnp.exp(jnp.where(causal, gc - grow, NEG))
        kst = _stack_heads(kc, masks).astype(BF16)
        aa = _dot_nt(jnp.concatenate([kb, q[b][sl]], axis=0).astype(BF16), kst)
        a_kk = jnp.where(strict, aa[0:c] * gamma, 0.0)
        a_qk.append(aa[c:2 * c] * gamma)
        p0.append(jnp.where(same_blk, -a_kk, 0.0))
        nn.append(jnp.where(same_blk, 0.0, a_kk))
        rhs_u.append(v[b][sl] * beta[b][sl])
        rhs_w.append(kb * jnp.exp(gc))
    t1 = [eye + p for p in p0]
    p1 = [_dot_bd(p0[i], [p0[i]], bd)[0] for i in idx]
    pr = [_dot_bd(p1[i], [p1[i], t1[i]], bd) for i in idx]
    p2 = [x[0] for x in pr]
    t2 = [t1[i] + pr[i][1] for i in idx]
    pr = [_dot_bd(p2[i], [p2[i], t2[i]], bd) for i in idx]
    p3 = [x[0] for x in pr]
    t3 = [t2[i] + pr[i][1] for i in idx]
    dinv = [t3[i] + _dot_bd(p3[i], [t3[i]], bd)[0] for i in idx]
    m1 = [_dot_bd(dinv[i], [nn[i]], bd)[0] for i in idx]
    m2 = [_dot_bd(m1[i], [m1[i]], bd)[0] for i in idx]
    im = [eye - m for m in m1]
    qq = [im[i] + _dot_bd(im[i], [m2[i]], bd)[0] for i in idx]
    inv = [_dot_bd(qq[i], [dinv[i]], bd)[0] for i in idx]
    uw = [_dot_bd(inv[i], [rhs_u[i], rhs_w[i]], bd) for i in idx]

    bdf = bd.astype(F32)
    state = [state_ref[b] for b in range(nb)]
    outs = [[] for _ in range(nb)]
    for i, (b, sl) in enumerate(items):
        gc = gcum[b][sl]
        g_last = gc[c - 1:c, :]
        u, w = uw[i]
        ws_qs = _dot(jnp.concatenate([w, q[b][sl] * jnp.exp(gc)], axis=0).astype(BF16), state[b].astype(BF16))
        v_new = (u - ws_qs[0:c]).astype(BF16)
        outs[b].append(ws_qs[c:2 * c] + _dot(a_qk[i].astype(BF16), _bd_tile(v_new, bd)))
        kd = k[b][sl] * jnp.exp(g_last - gc)
        state[b] = state[b] * jnp.exp(g_last) + _dot_tn(kd.astype(BF16), v_new) * bdf
    for b in range(nb):
        state_ref[b] = state[b]
        o = jnp.concatenate(outs[b], axis=0)
        o = o * lax.rsqrt(_dot_x2(o * o, bd) * (1.0 / HEAD_DIM) + EPS) * gn_ref[...]
        gate = a_ref[b, :, 768:1024].astype(F32)
        o_ref[b] = (o * (gate * jax.nn.sigmoid(gate))).astype(o_ref.dtype)


def _block_diag_ones(n, blk):
    i = np.arange(n)
    return jnp.asarray((i[:, None] // blk) == (i[None, :] // blk), F32)


def _gdn(pa, pba, conv_w, a_log, dt_bias, norm_g, batch, seq):
    ts = min(MIX_TS, seq)
    nst = seq // ts
    bd = _block_diag_ones(GROUP_WIDTH, HEAD_DIM).astype(BF16)
    r = np.arange(ts)
    lt = jnp.asarray((r[:, None] // CHUNK == r[None, :] // CHUNK) & (r[:, None] >= r[None, :]), BF16)
    lane = np.arange(GROUP_WIDTH)
    e = np.zeros((LANE, 2 * GROUP_WIDTH), np.float32)
    for h in range(N_HEADS):
        e[h, np.nonzero(lane // HEAD_DIM == h)[0]] = 1.0
        e[N_HEADS + h, GROUP_WIDTH + np.nonzero(lane // HEAD_DIM == h)[0]] = 1.0
    rep = lambda p: jnp.repeat(p.astype(F32), HEAD_DIM)[None, :]
    consts = [conv_w.astype(F32), rep(a_log), rep(dt_bias),
              jnp.tile(norm_g.astype(F32), N_HEADS)[None, :], bd, lt, jnp.asarray(e, BF16)]
    nb = GDN_NB if batch % GDN_NB == 0 else 1
    full = lambda a: pl.BlockSpec(a.shape, lambda b, s: (0, 0))
    out = pl.pallas_call(
        functools.partial(_gdn_kernel, ts=ts, nb=nb),
        name="gdn",
        grid=(batch // nb, nst),
        in_specs=[pl.BlockSpec((nb, ts, A_W), lambda b, s: (b, s, 0)),
                  pl.BlockSpec((nb, ts, LANE), lambda b, s: (b, s, 0))] + [full(a) for a in consts],
        out_specs=pl.BlockSpec((nb, ts, GROUP_WIDTH), lambda b, s: (b, s, 0)),
        out_shape=jax.ShapeDtypeStruct((batch, seq, GROUP_WIDTH), BF16),
        scratch_shapes=[pltpu.VMEM((nb, SUBLANE, 768), F32),
                        pltpu.VMEM((nb, GROUP_WIDTH, GROUP_WIDTH), F32)],
        compiler_params=_cparams(("parallel", "arbitrary")),
    )(pa.reshape(batch, seq, A_W), pba.reshape(batch, seq, LANE), *consts)
    return out.reshape(batch * seq, GROUP_WIDTH)


def _attn_kernel(qa_ref, qb_ref, k_ref, v_ref, bias_ref, lamv_ref, gn_ref, oa_ref, ob_ref,
                 va_ref, qt_ref, s_ref, p_ref, al_ref, mt_ref, m_ref, l_ref, acc_ref, *, lam_init, seq):
    t = ATT_T
    nq = seq // t
    i = pl.program_id(1)
    na = i + 1
    nlan = 2 * N_HEADS * t
    nsteps = nq + 1

    @pl.when(i == 0)
    def _stage_v():
        def body(j, carry):
            rows = pl.ds(pl.multiple_of(j * t, t), t)
            vt = v_ref[rows, :].astype(F32).T.astype(BF16)
            for h in range(N_HEADS):
                va_ref[j, h, 0:HEAD_DIM, :] = vt[h * HEAD_DIM:(h + 1) * HEAD_DIM, :]
                va_ref[j, h, HEAD_DIM:ATT_VROWS, :] = jnp.ones((ATT_VROWS - HEAD_DIM, t), BF16)
            return carry
        lax.fori_loop(0, seq // t, body, 0)

    feat = lax.broadcasted_iota(jnp.int32, (GROUP_WIDTH, t), 0) // DIFF_DH
    for w, q_ref in enumerate((qa_ref, qb_ref)):
        qt = (q_ref[...].astype(F32) * (DIFF_DH ** -0.5 * LOG2E)).T
        for idx in range(2 * N_HEADS):
            qt_ref[w, :, idx * t:(idx + 1) * t] = jnp.where(feat == idx, qt, 0.0).astype(BF16)
    m_ref[...] = jnp.full((2, 1, nlan), NEG, F32)
    l_ref[...] = jnp.zeros((2, 1, nlan), F32)
    acc_ref[...] = jnp.zeros((2, HEAD_DIM, nlan), F32)

    head_cols = [slice(2 * h * t, (2 * h + 2) * t) for h in range(N_HEADS)]

    def tile_of(s):
        if s >= nq // 2:
            bias = 2 if s == nq else (1 if s == nq - 1 else None)
            return 1, s - na, bias
        w = (s >= na).astype(jnp.int32)
        bias = jnp.where(w == 1, 0, jnp.where(s == na - 1, 2, jnp.where(s == na - 2, 1, 0)))
        return w, s - na * w, bias

    def scores(s, h):
        w, kt, bias = tile_of(s)
        rows = pl.ds(pl.multiple_of(kt * t, t), t)
        sc = _dot(k_ref[rows, :], qt_ref[w, :, head_cols[h]])
        if bias is not None:
            sc = sc + bias_ref[bias, :, head_cols[h]]
        s_ref[s % 2, :, head_cols[h]] = sc
        mt_ref[s % 2, :, head_cols[h]] = jnp.max(sc, axis=0, keepdims=True)

    def softmax(s, h):
        w, _, _ = tile_of(s)
        cols = head_cols[h]
        sc = s_ref[s % 2, :, cols]
        m_prev = m_ref[w, :, cols]
        m_new = jnp.maximum(m_prev, mt_ref[s % 2, :, cols])
        al_ref[s % 2, :, cols] = jnp.exp2(m_prev - m_new)
        m_ref[w, :, cols] = m_new
        p_ref[s % 2, :, cols] = jnp.exp2(sc - m_new).astype(BF16)

    def values(s, h):
        w, kt, _ = tile_of(s)
        cols = head_cols[h]
        alpha = al_ref[s % 2, :, cols]
        pv = _dot(va_ref[kt, h], p_ref[s % 2, :, cols])
        acc_ref[w, :, cols] = alpha * acc_ref[w, :, cols] + pv[0:HEAD_DIM]
        l_ref[w, :, cols] = alpha * l_ref[w, :, cols] + pv[HEAD_DIM:HEAD_DIM + 1]

    for s in range(nsteps + 2):
        for h in range(N_HEADS):
            if s < nsteps:
                scores(s, h)
            if 1 <= s <= nsteps:
                softmax(s - 1, h)
            if s >= 2:
                values(s - 2, h)

    lv = lamv_ref[...]
    lam = (jnp.exp(jnp.sum(lv[0:1] * lv[1:2], axis=1, keepdims=True))
           - jnp.exp(jnp.sum(lv[2:3] * lv[3:4], axis=1, keepdims=True)) + lam_init)
    for w, o_ref in enumerate((oa_ref, ob_ref)):
        inv_l = 1.0 / l_ref[w]
        outs = []
        for h in range(N_HEADS):
            c0 = slice(2 * h * t, (2 * h + 1) * t)
            c1 = slice((2 * h + 1) * t, (2 * h + 2) * t)
            oh = acc_ref[w, :, c0] * inv_l[:, c0] - lam * (acc_ref[w, :, c1] * inv_l[:, c1])
            oh = oh * lax.rsqrt(jnp.mean(oh * oh, axis=0, keepdims=True) + EPS)
            outs.append(oh)
        o = jnp.concatenate(outs, axis=0).T
        o_ref[...] = (o * gn_ref[...] * (1.0 - lam_init)).astype(o_ref.dtype)


def _t5_bucket(rel):
    nb = REL_BUCKETS // 2
    bucket = jnp.where(rel > 0, nb, 0)
    n = jnp.abs(rel)
    max_exact = nb // 2
    large = max_exact + (jnp.log(jnp.maximum(n, 1).astype(F32) / max_exact)
                         / math.log(REL_MAX_DIST / max_exact) * (nb - max_exact)).astype(jnp.int32)
    large = jnp.minimum(large, nb - 1)
    return bucket + jnp.where(n < max_exact, n, large)


def _attn_bias_tiles(rel_bias):
    t = ATT_T
    table = rel_bias.astype(F32)
    kk = jnp.arange(t)[:, None]
    qq = jnp.arange(t)[None, :]

    def expand(b):
        b = jnp.transpose(b, (0, 2, 1))
        b = jnp.broadcast_to(b[:, :, None, :], (t, N_HEADS, 2, t))
        return b.reshape(t, 2 * N_HEADS * t)

    table = table * LOG2E

    def lookup(bucket):
        onehot = (bucket[..., None] == jnp.arange(REL_BUCKETS)).astype(F32)
        return jnp.einsum('kqb,bh->kqh', onehot, table, precision=HI)

    diag = lookup(_t5_bucket(kk - qq))
    diag = jnp.where(((kk // CHUNK) <= (qq // CHUNK))[:, :, None], diag, NEG)
    near = lookup(_t5_bucket(kk - qq - t))
    far = lookup(_t5_bucket(jnp.full((1, 1), -(REL_MAX_DIST + 1), jnp.int32)))
    far = jnp.broadcast_to(far, (t, t, N_HEADS))
    return jnp.stack([jnp.zeros((t, 2 * N_HEADS * t), F32), expand(near - far), expand(diag - far)])


def _diff_attn(pb, lam_vecs, lam_init, bias_tiles, norm_g, batch, seq):
    t = ATT_T
    nq = seq // t
    gn = jnp.tile(norm_g.astype(F32), N_HEADS)[None, :]
    lamv = lam_vecs.astype(F32)
    nlan = 2 * N_HEADS * t
    assert nq % 2 == 0
    nh = nq // 2
    full = lambda a: pl.BlockSpec(a.shape, lambda b, i: (0,) * a.ndim)
    half = jax.ShapeDtypeStruct((batch * nh * t, GROUP_WIDTH), BF16)
    lo, hi = pl.pallas_call(
        functools.partial(_attn_kernel, lam_init=lam_init, seq=seq),
        name="diffattn",
        grid=(batch, nh),
        in_specs=[pl.BlockSpec((t, GROUP_WIDTH), lambda b, i: (b * nq + i, 0)),
                  pl.BlockSpec((t, GROUP_WIDTH), lambda b, i: (b * nq + nq - 1 - i, 0)),
                  pl.BlockSpec((seq, GROUP_WIDTH), lambda b, i: (b, 1)),
                  pl.BlockSpec((seq, GROUP_WIDTH), lambda b, i: (b, 2)),
                  full(bias_tiles), full(lamv), full(gn)],
        out_specs=[pl.BlockSpec((t, GROUP_WIDTH), lambda b, i: (b * nh + i, 0)),
                   pl.BlockSpec((t, GROUP_WIDTH), lambda b, i: (b * nh + nh - 1 - i, 0))],
        out_shape=[half, half],
        scratch_shapes=[pltpu.VMEM((seq // t, N_HEADS, ATT_VROWS, t), BF16),
                        pltpu.VMEM((2, GROUP_WIDTH, nlan), BF16),
                        pltpu.VMEM((2, t, nlan), F32),
                        pltpu.VMEM((2, t, nlan), BF16),
                        pltpu.VMEM((2, 1, nlan), F32),
                        pltpu.VMEM((2, 1, nlan), F32),
                        pltpu.VMEM((2, 1, nlan), F32), pltpu.VMEM((2, 1, nlan), F32),
                        pltpu.VMEM((2, HEAD_DIM, nlan), F32)],
        compiler_params=_cparams(("parallel", "arbitrary")),
    )(pb, pb, pb, pb, bias_tiles, lamv, gn)
    out = jnp.concatenate([lo.reshape(batch, nh * t, GROUP_WIDTH), hi.reshape(batch, nh * t, GROUP_WIDTH)],
                          axis=1)
    return out.reshape(batch * seq, GROUP_WIDTH)


def _rglru_kernel(c_ref, convw_ref, convb_ref, wa_ref, ba_ref, wx_ref, bx_ref, ap_ref, o_ref,
                  tail_ref, h_ref, *, ts):
    @pl.when(pl.program_id(1) == 0)
    def _init():
        tail_ref[...] = jnp.zeros_like(tail_ref)
        h_ref[...] = jnp.zeros_like(h_ref)

    xb = c_ref[:, 0:256].astype(F32)
    gb = c_ref[:, 256:512].astype(F32)
    xc = _causal_conv(xb, tail_ref[...], convw_ref[...]) + convb_ref[...]
    tail_ref[...] = xb[ts - SUBLANE:ts, :]
    xcb = xc.astype(BF16)
    gate_a = jax.nn.sigmoid(_dot(xcb, wa_ref[...]) + ba_ref[...])
    gate_x = jax.nn.sigmoid(_dot(xcb, wx_ref[...]) + bx_ref[...])
    log_a = -RG_C * gate_a * _softplus(ap_ref[...])
    a = jnp.exp(log_a)
    th = jnp.tanh(log_a)
    u = xc * gate_x * jnp.sqrt(-2.0 * th / (1.0 - th))
    row = lax.broadcasted_iota(jnp.int32, (ts, GROUP_WIDTH), 0)
    d = 1
    while d < ts:
        keep = row >= d
        a_sh = jnp.where(keep, pltpu.roll(a, d, 0), 1.0)
        u_sh = jnp.where(keep, pltpu.roll(u, d, 0), 0.0)
        u = u + a * u_sh
        a = a * a_sh
        d *= 2
    h = u + a * h_ref[...]
    h_ref[...] = h[ts - 1:ts, :]
    gelu = 0.5 * gb * (1.0 + jnp.tanh(math.sqrt(2.0 / math.pi) * (gb + 0.044715 * (gb * gb * gb))))
    o_ref[...] = (h * gelu).astype(o_ref.dtype)


def _block_diag_weight(w):
    nb, wi, wo = w.shape
    out = jnp.zeros((nb * wi, nb * wo), w.dtype)
    for i in range(nb):
        out = out.at[i * wi:(i + 1) * wi, i * wo:(i + 1) * wo].set(w[i])
    return out


def _rglru(pc, conv_w, conv_b, w_a, b_a, w_x, b_x, a_param, batch, seq):
    ts = min(MIX_TS, seq)
    nst = seq // ts
    r = lambda p: p.astype(F32)[None, :]
    consts = [conv_w.astype(F32), r(conv_b), _block_diag_weight(w_a).astype(BF16), r(b_a),
              _block_diag_weight(w_x).astype(BF16), r(b_x), r(a_param)]
    full = lambda a: pl.BlockSpec(a.shape, lambda b, s: (0, 0))
    return pl.pallas_call(
        functools.partial(_rglru_kernel, ts=ts),
        name="rglru",
        grid=(batch, nst),
        in_specs=[pl.BlockSpec((ts, C_W), lambda b, s: (b * nst + s, 0))] + [full(a) for a in consts],
        out_specs=pl.BlockSpec((ts, GROUP_WIDTH), lambda b, s: (b * nst + s, 0)),
        out_shape=jax.ShapeDtypeStruct((batch * seq, GROUP_WIDTH), BF16),
        scratch_shapes=[pltpu.VMEM((SUBLANE, GROUP_WIDTH), F32), pltpu.VMEM((1, GROUP_WIDTH), F32)],
        compiler_params=_cparams(("parallel", "arbitrary")),
    )(pc, *consts)


def _gla_kernel(d_ref, lr_ref, wlr_ref, blr_ref, gn_ref, bd_ref, bdt_ref, lt_ref, o_ref, state_ref,
                *, ts, nb):
    @pl.when(pl.program_id(1) == 0)
    def _init():
        state_ref[...] = jnp.zeros_like(state_ref)

    c = CHUNK
    bd = bd_ref[...]
    bdt = bdt_ref[...]
    lt = lt_ref[...]
    kmasks = _head_mask(N_HEADS * GLA_DK, GLA_DK)
    ri = lax.broadcasted_iota(jnp.int32, (c, GROUP_WIDTH), 0)
    cj = lax.broadcasted_iota(jnp.int32, (c, GROUP_WIDTH), 1) % HEAD_DIM
    causal = ri >= cj
    q, k, gcum = [], [], []
    for b in range(nb):
        q.append(d_ref[b, :, 0:128].astype(F32) * (GLA_DK ** -0.5))
        k.append(d_ref[b, :, 128:256].astype(F32))
        lr_hi, lr_lo = _split2(lr_ref[b])
        w_hi, w_lo = _split2(wlr_ref[...])
        z = _dot(lr_hi, w_hi) + (_dot(lr_hi, w_lo) + _dot(lr_lo, w_hi)) + blr_ref[...]
        la = (jnp.minimum(z, 0.0) - jnp.log1p(jnp.exp(-jnp.abs(z)))) * (1.0 / GLA_TAU)
        la_hi, la_lo = _split2(la)
        la_lo2 = (la - la_hi.astype(F32) - la_lo.astype(F32)).astype(BF16)
        gcum.append(_dot(lt, la_hi) + (_dot(lt, la_lo) + _dot(lt, la_lo2)))

    items = [(b, slice(ci * c, (ci + 1) * c)) for ci in range(ts // c) for b in range(nb)]
    o_intra, upd = [], []
    for b, sl in items:
        qc, kc, gc = q[b][sl], k[b][sl], gcum[b][sl]
        vc = d_ref[b, sl, 256:512]
        ref = gc[c // 2:c // 2 + 1, :]
        kst = _stack_heads(kc * jnp.exp(ref - gc), kmasks).astype(BF16)
        a_in = _dot_nt((qc * jnp.exp(gc - ref)).astype(BF16), kst)
        a_in = jnp.where(causal, a_in, 0.0).astype(BF16)
        o_intra.append(_dot(a_in, _bd_tile(vc, bd)))
        kd = kc * jnp.exp(gc[c - 1:c, :] - gc)
        upd.append(_dot_tn(vc, kd.astype(BF16)) * bdt)

    state = [state_ref[b] for b in range(nb)]
    outs = [[] for _ in range(nb)]
    for i, (b, sl) in enumerate(items):
        gc = gcum[b][sl]
        o_inter = _dot_nt((q[b][sl] * jnp.exp(gc)).astype(BF16), state[b].astype(BF16))
        outs[b].append(o_intra[i] + o_inter)
        state[b] = state[b] * jnp.exp(gc[c - 1:c, :]) + upd[i]
    for b in range(nb):
        state_ref[b] = state[b]
        o = jnp.concatenate(outs[b], axis=0)
        o = o * lax.rsqrt(_dot_x2(o * o, bd) * (1.0 / HEAD_DIM) + EPS) * gn_ref[...]
        rt = d_ref[b, :, 512:768].astype(F32)
        o_ref[b] = (o * (rt * jax.nn.sigmoid(rt))).astype(o_ref.dtype)


def _gla(pd, plr, w_lr, b_lr, norm_g, batch, seq):
    ts = min(MIX_TS, seq)
    nst = seq // ts
    kw = N_HEADS * GLA_DK
    wlr = jnp.zeros((LANE, kw), F32).at[0:GLA_RANK, :].set(w_lr.astype(F32))
    bd = _block_diag_ones(GROUP_WIDTH, HEAD_DIM).astype(BF16)
    iv = np.arange(GROUP_WIDTH)[:, None] // HEAD_DIM
    ik = np.arange(kw)[None, :] // GLA_DK
    bdt = jnp.asarray(iv == ik, F32)
    r = np.arange(ts)
    lt = jnp.asarray((r[:, None] // CHUNK == r[None, :] // CHUNK) & (r[:, None] >= r[None, :]), BF16)
    consts = [wlr, b_lr.astype(F32)[None, :], jnp.tile(norm_g.astype(F32), N_HEADS)[None, :], bd, bdt, lt]
    nb = GDN_NB if batch % GDN_NB == 0 else 1
    full = lambda a: pl.BlockSpec(a.shape, lambda b, s: (0, 0))
    out = pl.pallas_call(
        functools.partial(_gla_kernel, ts=ts, nb=nb),
        name="gla",
        grid=(batch // nb, nst),
        in_specs=[pl.BlockSpec((nb, ts, D_W), lambda b, s: (b, s, 0)),
                  pl.BlockSpec((nb, ts, LANE), lambda b, s: (b, s, 0))] + [full(a) for a in consts],
        out_specs=pl.BlockSpec((nb, ts, GROUP_WIDTH), lambda b, s: (b, s, 0)),
        out_shape=jax.ShapeDtypeStruct((batch, seq, GROUP_WIDTH), BF16),
        scratch_shapes=[pltpu.VMEM((nb, GROUP_WIDTH, kw), F32)],
        compiler_params=_cparams(("parallel", "arbitrary")),
    )(pd.reshape(batch, seq, D_W), plr.reshape(batch, seq, LANE), *consts)
    return out.reshape(batch * seq, GROUP_WIDTH)


def _mix_outproj(x_ref, ma_ref, mb_ref, mc_ref, md_ref, wo_ref):
    mix = jnp.concatenate([ma_ref[...], mb_ref[...], mc_ref[...], md_ref[...]], axis=1)
    return x_ref[...] + _dot(mix, wo_ref[...])


def _ffn_kernel(x_ref, ma_ref, mb_ref, mc_ref, md_ref, wo_ref, g_ref, wg_ref, wu_ref, wd_ref,
                o_ref, h_ref, acc_ref):
    f = pl.program_id(1)

    @pl.when(f == 0)
    def _first():
        x1 = _mix_outproj(x_ref, ma_ref, mb_ref, mc_ref, md_ref, wo_ref)
        acc_ref[...] = x1
        h_ref[...] = _rms(x1, g_ref[...]).astype(BF16)

    h = h_ref[...]
    gt = _dot(h, wg_ref[...])
    act = (gt * jax.nn.sigmoid(gt) * _dot(h, wu_ref[...])).astype(BF16)
    acc_ref[...] += _dot(act, wd_ref[...])

    @pl.when(f == pl.num_programs(1) - 1)
    def _last():
        o_ref[...] = acc_ref[...]


def _outproj_ffn(x, mixes, w_out, g, w_gate, w_up, w_down):
    t = x.shape[0]
    tm = min(FFN_TM, t)
    tf = FFN_TF
    nf = D_FF // tf
    row = lambda w: pl.BlockSpec((tm, w), lambda i, f: (i, 0))
    return pl.pallas_call(
        _ffn_kernel,
        name="outproj_ffn",
        grid=(t // tm, nf),
        in_specs=[row(D_MODEL)] + [row(GROUP_WIDTH)] * 4 + [
            pl.BlockSpec((D_MODEL, D_MODEL), lambda i, f: (0, 0)),
            pl.BlockSpec((1, D_MODEL), lambda i, f: (0, 0)),
            pl.BlockSpec((D_MODEL, tf), lambda i, f: (0, f)),
            pl.BlockSpec((D_MODEL, tf), lambda i, f: (0, f)),
            pl.BlockSpec((tf, D_MODEL), lambda i, f: (f, 0))],
        out_specs=row(D_MODEL),
        out_shape=jax.ShapeDtypeStruct((t, D_MODEL), F32),
        scratch_shapes=[pltpu.VMEM((tm, D_MODEL), BF16), pltpu.VMEM((tm, D_MODEL), F32)],
        compiler_params=_cparams(("parallel", "arbitrary")),
    )(x, *mixes, w_out, g, w_gate, w_up, w_down)


def _router_kernel(x_ref, ma_ref, mb_ref, mc_ref, md_ref, wo_ref, g_ref, wrh_ref, wrl_ref,
                   x1_ref, h_ref, r_ref):
    x1 = _mix_outproj(x_ref, ma_ref, mb_ref, mc_ref, md_ref, wo_ref)
    x1_ref[...] = x1
    h = _rms(x1, g_ref[...])
    h_hi, h_lo = _split2(h)
    h_ref[...] = h_hi
    lane = lax.broadcasted_iota(jnp.int32, (x1.shape[0], LANE), 1)
    logits = _dot(h_hi, wrh_ref[...]) + (_dot(h_hi, wrl_ref[...]) + _dot(h_lo, wrh_ref[...]))
    logits = jnp.where(lane < N_EXPERTS, logits, NEG)
    m1 = jnp.max(logits, axis=1, keepdims=True)
    e1 = jnp.min(jnp.where(logits == m1, lane, LANE), axis=1, keepdims=True)
    rest = jnp.where(lane == e1, NEG, logits)
    m2 = jnp.max(rest, axis=1, keepdims=True)
    e2 = jnp.min(jnp.where(rest == m2, lane, LANE), axis=1, keepdims=True)
    ex = jnp.exp(m2 - m1)
    w1 = 1.0 / (1.0 + ex)
    w2 = ex / (1.0 + ex)
    r_ref[...] = jnp.where(lane == 0, e1.astype(F32),
                           jnp.where(lane == 1, e2.astype(F32),
                                     jnp.where(lane == 2, w1, jnp.where(lane == 3, w2, 0.0))))


def _outproj_router(x, mixes, w_out, g, w_router):
    t = x.shape[0]
    tm = min(ROUTER_TM, t)
    wr = jnp.pad(w_router.astype(F32), ((0, 0), (0, LANE - N_EXPERTS)))
    wr_hi = wr.astype(BF16)
    wr_lo = (wr - wr_hi.astype(F32)).astype(BF16)
    row = lambda w: pl.BlockSpec((tm, w), lambda i: (i, 0))
    full = lambda a: pl.BlockSpec(a.shape, lambda i: (0, 0))
    return pl.pallas_call(
        _router_kernel,
        name="outproj_router",
        grid=(t // tm,),
        in_specs=[row(D_MODEL)] + [row(GROUP_WIDTH)] * 4 + [full(w_out), full(g), full(wr_hi), full(wr_lo)],
        out_specs=[row(D_MODEL), row(D_MODEL), row(LANE)],
        out_shape=[jax.ShapeDtypeStruct((t, D_MODEL), F32), jax.ShapeDtypeStruct((t, D_MODEL), BF16),
                   jax.ShapeDtypeStruct((t, LANE), F32)],
        compiler_params=_cparams(("parallel",)),
    )(x, *mixes, w_out, g, wr_hi, wr_lo)


def _moe_kernel(te_ref, na_ref, x_ref, wg_ref, wu_ref, wd_ref, o_ref, acc_ref):
    i = pl.program_id(0)
    f = pl.program_id(1)

    @pl.when(i < na_ref[0])
    def _active():
        @pl.when(f == 0)
        def _zero():
            acc_ref[...] = jnp.zeros_like(acc_ref)

        x = x_ref[...]
        gt = _dot(x, wg_ref[...].astype(BF16))
        act = (gt * jax.nn.sigmoid(gt) * _dot(x, wu_ref[...].astype(BF16))).astype(BF16)
        acc_ref[...] += _dot(act, wd_ref[...].astype(BF16))

        @pl.when(f == pl.num_programs(1) - 1)
        def _last():
            o_ref[...] = acc_ref[...].astype(o_ref.dtype)

    @pl.when(jnp.logical_and(i >= na_ref[0], f == pl.num_programs(1) - 1))
    def _unused_tile():
        o_ref[...] = jnp.zeros_like(o_ref)


def _moe_experts(xs, tile_e, n_active, w_gate, w_up, w_down, n_tiles):
    tm, tf = MOE_TM, MOE_TF
    nf = D_FF_EXPERT // tf

    def tile(i, na):
        return jnp.minimum(i, na[0] - 1)

    def fidx(i, f, na):
        return jnp.where(i < na[0], f, nf - 1)

    grid_spec = pltpu.PrefetchScalarGridSpec(
        num_scalar_prefetch=2,
        grid=(n_tiles, nf),
        in_specs=[
            pl.BlockSpec((tm, D_MODEL), lambda i, f, te, na: (tile(i, na), 0)),
            pl.BlockSpec((None, D_MODEL, tf), lambda i, f, te, na: (te[tile(i, na)], 0, fidx(i, f, na))),
            pl.BlockSpec((None, D_MODEL, tf), lambda i, f, te, na: (te[tile(i, na)], 0, fidx(i, f, na))),
            pl.BlockSpec((None, tf, D_MODEL), lambda i, f, te, na: (te[tile(i, na)], fidx(i, f, na), 0)),
        ],
        out_specs=pl.BlockSpec((tm, D_MODEL), lambda i, f, te, na: (i, 0)),
        scratch_shapes=[pltpu.VMEM((tm, D_MODEL), F32)],
    )
    return pl.pallas_call(
        _moe_kernel,
        name="moe_experts",
        grid_spec=grid_spec,
        out_shape=jax.ShapeDtypeStruct((n_tiles * tm, D_MODEL), BF16),
        compiler_params=_cparams(("arbitrary", "arbitrary")),
    )(tile_e, n_active, xs, w_gate, w_up, w_down)


def _route(route, t):
    tm = MOE_TM
    n_assign = t * TOP_K
    flat_e = jnp.concatenate([route[:, k].astype(jnp.int32) for k in range(TOP_K)])
    onehot = (flat_e[:, None] == jnp.arange(N_EXPERTS)[None, :]).astype(jnp.int32)
    counts = jnp.sum(onehot, axis=0)
    padded = (counts + tm - 1) // tm * tm
    pad_end = jnp.cumsum(padded)
    pad_start = pad_end - padded
    grp_start = jnp.cumsum(counts) - counts
    dest = jnp.sum(onehot * (jnp.cumsum(onehot, axis=0) - 1 + pad_start[None, :]), axis=1)
    n_tiles = -(-n_assign // tm) + N_EXPERTS
    tile_e = jnp.minimum(jnp.searchsorted(pad_end, jnp.arange(n_tiles) * tm, side='right'),
                         N_EXPERTS - 1).astype(jnp.int32)
    n_active = (pad_end[-1] // tm).astype(jnp.int32).reshape(1)
    order = jnp.argsort(flat_e, stable=True).astype(jnp.int32)
    rank_s = ((jnp.arange(n_tiles, dtype=jnp.int32) * tm - pad_start[tile_e])[:, None]
              + jnp.arange(tm, dtype=jnp.int32)[None, :])
    src = jnp.minimum(grp_start[tile_e][:, None] + rank_s, n_assign - 1)
    slot = jnp.arange(n_tiles * tm, dtype=jnp.int32).reshape(n_tiles, tm)
    slot_tok = jnp.where(rank_s < counts[tile_e][:, None], order[src] % t, slot % t).reshape(n_tiles * tm)
    return slot_tok, tile_e, n_active, [dest[k * t:(k + 1) * t] for k in range(TOP_K)], n_tiles


def _combine_kernel(x_ref, y0_ref, y1_ref, r_ref, g_ref, o_ref, *, final):
    r = r_ref[...]
    y = x_ref[...] + (y0_ref[...].astype(F32) * r[:, TOP_K:TOP_K + 1]
                      + y1_ref[...].astype(F32) * r[:, TOP_K + 1:TOP_K + 2])
    o_ref[...] = _rms(y, g_ref[...]) if final else y


def _moe_combine(x1, y0, y1, route, g, final):
    t = x1.shape[0]
    tm = min(FFN_TM, t)
    row = pl.BlockSpec((tm, D_MODEL), lambda i: (i, 0))
    return pl.pallas_call(
        functools.partial(_combine_kernel, final=final),
        name="moe_combine",
        grid=(t // tm,),
        in_specs=[row, row, row, pl.BlockSpec((tm, LANE), lambda i: (i, 0)),
                  pl.BlockSpec((1, D_MODEL), lambda i: (0, 0))],
        out_specs=row,
        out_shape=jax.ShapeDtypeStruct((t, D_MODEL), F32),
        compiler_params=_cparams(("parallel",)),
    )(x1, y0, y1, route, g)


def _final_norm_kernel(x_ref, g_ref, o_ref):
    o_ref[...] = _rms(x_ref[...], g_ref[...])


def _split_w_in(w):
    def pad(a, width):
        return jnp.pad(a, ((0, 0), (0, width - a.shape[1])))
    a_end = A_W + 2 * N_HEADS
    b_end = a_end + B_W
    c_end = b_end + C_W
    d_end = c_end + D_W
    parts = [(w[:, 0:A_W], BF16), (pad(w[:, A_W:a_end], LANE), F32), (w[:, a_end:b_end], BF16),
             (w[:, b_end:c_end], BF16), (w[:, c_end:d_end], BF16), (pad(w[:, d_end:], LANE), F32)]
    return [(m.astype(BF16), dt) for m, dt in parts]


def kernel(x, norm_mix, w_in, a_conv, a_A_log, a_dt_bias, a_norm, b_lambda, b_norm, rel_bias,
           c_conv_w, c_conv_b, c_w_a, c_b_a, c_w_x, c_b_x, c_a_param, d_w_lr, d_b_lr, d_norm,
           w_out, norm_ffn, ffn_w_gate, ffn_w_up, ffn_w_down, moe_router, moe_w_gate, moe_w_up,
           moe_w_down, norm_final):
    batch, seq, _ = x.shape
    depth = w_in.shape[0]
    t = batch * seq
    xt = x.reshape(t, D_MODEL).astype(F32)
    bias_tiles = _attn_bias_tiles(rel_bias)
    row = lambda p: p.astype(F32)[None, :]
    out = None
    for l in range(depth):
        pa, pba, pb, pc, pd, plr = _inproj(xt, row(norm_mix[l]), _split_w_in(w_in[l]))
        lam_init = 0.8 - 0.6 * math.exp(-0.3 * l)
        mixes = (
            _gdn(pa, pba, a_conv[l], a_A_log[l], a_dt_bias[l], a_norm[l], batch, seq),
            _diff_attn(pb, b_lambda[l], lam_init, bias_tiles, b_norm[l], batch, seq),
            _rglru(pc, c_conv_w[l], c_conv_b[l], c_w_a[l], c_b_a[l], c_w_x[l], c_b_x[l],
                   c_a_param[l], batch, seq),
            _gla(pd, plr, d_w_lr[l], d_b_lr[l], d_norm[l], batch, seq),
        )
        wo = w_out[l].astype(BF16)
        if l % 2 == 0:
            j = l // 2
            xt = _outproj_ffn(xt, mixes, wo, row(norm_ffn[l]), ffn_w_gate[j].astype(BF16),
                              ffn_w_up[j].astype(BF16), ffn_w_down[j].astype(BF16))
            out = None
        else:
            j = l // 2
            final = l == depth - 1
            take = lambda a, i: a.at[i].get(mode='promise_in_bounds')
            x1, h, route = _outproj_router(xt, mixes, wo, row(norm_ffn[l]), moe_router[j])
            slot_tok, tile_e, n_active, dest, n_tiles = _route(route, t)
            xs = take(h, slot_tok)
            ys = _moe_experts(xs, tile_e, n_active, moe_w_gate[j], moe_w_up[j], moe_w_down[j], n_tiles)
            y = _moe_combine(x1, take(ys, dest[0]), take(ys, dest[1]), route, row(norm_final), final)
            if final:
                out = y
            else:
                xt = y
    if out is None:
        tm = min(FFN_TM, t)
        rowspec = pl.BlockSpec((tm, D_MODEL), lambda i: (i, 0))
        out = pl.pallas_call(
            _final_norm_kernel, name="final_norm", grid=(t // tm,),
            in_specs=[rowspec, pl.BlockSpec((1, D_MODEL), lambda i: (0, 0))],
            out_specs=rowspec, out_shape=jax.ShapeDtypeStruct((t, D_MODEL), F32),
            compiler_params=_cparams(("parallel",)),
        )(xt, row(norm_final))
    return out.reshape(batch, seq, D_MODEL).astype(x.dtype)
```

```python
import functools
import math

import jax
import jax.numpy as jnp
import numpy as np
from jax import lax
from jax.experimental import pallas as pl
from jax.experimental.pallas import tpu as pltpu

D_MODEL = 1024
CHUNK = 64
N_HEADS = 4
HEAD_DIM = 64
GROUP_WIDTH = 256
CONV_WIDTH = 4
DIFF_DH = 32
Q_BLOCK = 128
REL_BUCKETS = 32
REL_MAX_DIST = 128
RG_C = 8.0
GLA_DK = 32
GLA_RANK = 16
GLA_TAU = 16.0
D_FF = 2816
N_EXPERTS = 8
TOP_K = 2
D_FF_EXPERT = 3584
EPS = 1e-6
assert CHUNK == HEAD_DIM

LANE = 128
SUBLANE = 8
VMEM_LIMIT = 56 * 1024 * 1024

F32 = jnp.float32
BF16 = jnp.bfloat16
HI = lax.Precision.HIGHEST
NEG = -1e30
LOG2E = math.log2(math.e)

A_W = 1024
B_W = 768
C_W = 512
D_W = 768
SIDE_LR = 2 * N_HEADS

PROJ_TM = 1024
MIX_TS = 256
GDN_BLK = 16
GDN_NB = 4
ATT_T = 128
ATT_VROWS = 80
FFN_TM = 512
ROUTER_TM = 1024
FFN_TF = 1408
MOE_TM = 1024
MOE_TF = 512


def _cparams(sem):
    return pltpu.CompilerParams(dimension_semantics=sem, vmem_limit_bytes=VMEM_LIMIT)


def _dot(a, b, precision=None):
    return jnp.dot(a, b, preferred_element_type=F32, precision=precision)


def _dot_nt(a, b, precision=None):
    return lax.dot_general(a, b, (((1,), (1,)), ((), ())), preferred_element_type=F32,
                           precision=precision)


def _dot_tn(a, b, precision=None):
    return lax.dot_general(a, b, (((0,), (0,)), ((), ())), preferred_element_type=F32,
                           precision=precision)


def _softplus(x):
    return jnp.maximum(x, 0.0) + jnp.log1p(jnp.exp(-jnp.abs(x)))


def _rms(x, g):
    return x * lax.rsqrt(jnp.mean(x * x, axis=-1, keepdims=True) + EPS) * g


def _causal_conv(x, tail, w):
    row = lax.broadcasted_iota(jnp.int32, (SUBLANE, x.shape[1]), 0)
    y = x * w[CONV_WIDTH - 1:CONV_WIDTH, :]
    for d in range(1, CONV_WIDTH):
        rolled = pltpu.roll(x, d, 0)
        first = jnp.where(row < d, pltpu.roll(tail, d, 0), rolled[:SUBLANE])
        shifted = jnp.concatenate([first, rolled[SUBLANE:]], axis=0)
        y = y + shifted * w[CONV_WIDTH - 1 - d:CONV_WIDTH - d, :]
    return y


def _head_mask(width, per_head):
    lane = lax.broadcasted_iota(jnp.int32, (1, width), 1)
    return [(lane // per_head) == h for h in range(N_HEADS)]


def _stack_heads(x, masks):
    return jnp.concatenate([jnp.where(m, x, 0.0) for m in masks], axis=0)


def _unstack_heads(r, masks, c):
    out = jnp.where(masks[0], r[0:c], 0.0)
    for h in range(1, N_HEADS):
        out = out + jnp.where(masks[h], r[h * c:(h + 1) * c], 0.0)
    return out


def _inproj_kernel(x_ref, g_ref, *refs):
    n = len(refs) // 2
    h = _rms(x_ref[...], g_ref[...]).astype(BF16)
    for w_ref, o_ref in zip(refs[:n], refs[n:]):
        o_ref[...] = _dot(h, w_ref[...]).astype(o_ref.dtype)


def _inproj(x, g, weights):
    t = x.shape[0]
    tm = min(PROJ_TM, t)
    row = lambda w: pl.BlockSpec((tm, w), lambda i: (i, 0))
    full = lambda a: pl.BlockSpec(a.shape, lambda i: (0, 0))
    ws = [w for w, _ in weights]
    return pl.pallas_call(
        _inproj_kernel,
        name="inproj",
        grid=(t // tm,),
        in_specs=[row(D_MODEL), full(g)] + [full(w) for w in ws],
        out_specs=[row(w.shape[1]) for w in ws],
        out_shape=[jax.ShapeDtypeStruct((t, w.shape[1]), dt) for w, dt in weights],
        compiler_params=_cparams(("parallel",)),
    )(x, g, *ws)


def _split2(x):
    hi = x.astype(BF16)
    return hi, (x - hi.astype(F32)).astype(BF16)


def _dot_x2(x, w):
    hi, lo = _split2(x)
    return _dot(hi, w) + _dot(lo, w)


def _bd_tile(x, bd):
    return jnp.concatenate([x] * N_HEADS, axis=0) * bd


def _dot_bd(a, bs, bd):
    ab = a.astype(BF16)
    return [_dot(ab, _bd_tile(b.astype(BF16), bd)) for b in bs]


def _gdn_kernel(a_ref, ba_ref, convw_ref, alog_ref, dtb_ref, gn_ref, bd_ref, lt_ref, e_ref,
                o_ref, tail_ref, state_ref, *, ts, nb):
    @pl.when(pl.program_id(1) == 0)
    def _init():
        tail_ref[...] = jnp.zeros_like(tail_ref)
        state_ref[...] = jnp.zeros_like(state_ref)

    c = CHUNK
    bd = bd_ref[...]
    lt = lt_ref[...]
    q, k, v, beta, gcum = [], [], [], [], []
    for b in range(nb):
        xin = a_ref[b, :, 0:768].astype(F32)
        y = _causal_conv(xin, tail_ref[b], convw_ref[...])
        tail_ref[b] = xin[ts - SUBLANE:ts, :]
        y = y * jax.nn.sigmoid(y)
        qb, kb_ = y[:, 0:256], y[:, 256:512]
        q.append(qb * lax.rsqrt(_dot_x2(qb * qb, bd) + EPS) * (HEAD_DIM ** -0.5))
        k.append(kb_ * lax.rsqrt(_dot_x2(kb_ * kb_, bd) + EPS))
        v.append(y[:, 512:768])
        e = _dot_x2(ba_ref[b], e_ref[...])
        beta.append(jax.nn.sigmoid(e[:, 0:256]))
        g = -jnp.exp(alog_ref[...]) * _softplus(e[:, 256:512] + dtb_ref[...])
        g_hi, g_lo = _split2(g)
        g_lo2 = (g - g_hi.astype(F32) - g_lo.astype(F32)).astype(BF16)
        gcum.append(_dot(lt, g_hi) + (_dot(lt, g_lo) + _dot(lt, g_lo2)))

    ri = lax.broadcasted_iota(jnp.int32, (c, GROUP_WIDTH), 0)
    cj = lax.broadcasted_iota(jnp.int32, (c, GROUP_WIDTH), 1) % HEAD_DIM
    causal = ri >= cj
    strict = ri > cj
    diag = ri == cj
    same_blk = (ri // GDN_BLK) == (cj // GDN_BLK)
    eye = diag.astype(F32)
    masks = _head_mask(GROUP_WIDTH, HEAD_DIM)
    items = [(b, slice(ci * c, (ci + 1) * c)) for ci in range(ts // c) for b in range(nb)]
    idx = range(len(items))

    a_qk, p0, nn, rhs_u, rhs_w = [], [], [], [], []
    for b, sl in items:
        kc, gc = k[b][sl], gcum[b][sl]
        kb = kc * beta[b][sl]
        grow = jnp.sum(jnp.where(diag, gc, 0.0), axis=0, keepdims=True)
        gamma = jnp.exp(jnp.where(causal, gc - grow, NEG))
        kst = _stack_heads(kc, masks).astype(BF16)
        aa = _dot_nt(jnp.concatenate([kb, q[b][sl]], axis=0).astype(BF16), kst)
        a_kk = jnp.where(strict, aa[0:c] * gamma, 0.0)
        a_qk.append(aa[c:2 * c] * gamma)
        p0.append(jnp.where(same_blk, -a_kk, 0.0))
        nn.append(jnp.where(same_blk, 0.0, a_kk))
        rhs_u.append(v[b][sl] * beta[b][sl])
        rhs_w.append(kb * jnp.exp(gc))
    t1 = [eye + p for p in p0]
    p1 = [_dot_bd(p0[i], [p0[i]], bd)[0] for i in idx]
    pr = [_dot_bd(p1[i], [p1[i], t1[i]], bd) for i in idx]
    p2 = [x[0] for x in pr]
    t2 = [t1[i] + pr[i][1] for i in idx]
    pr = [_dot_bd(p2[i], [p2[i], t2[i]], bd) for i in idx]
    p3 = [x[0] for x in pr]
    t3 = [t2[i] + pr[i][1] for i in idx]
    dinv = [t3[i] + _dot_bd(p3[i], [t3[i]], bd)[0] for i in idx]
    m1 = [_dot_bd(dinv[i], [nn[i]], bd)[0] for i in idx]
    m2 = [_dot_bd(m1[i], [m1[i]], bd)[0] for i in idx]
    im = [eye - m for m in m1]
    qq = [im[i] + _dot_bd(im[i], [m2[i]], bd)[0] for i in idx]
    inv = [_dot_bd(qq[i], [dinv[i]], bd)[0] for i in idx]
    uw = [_dot_bd(inv[i], [rhs_u[i], rhs_w[i]], bd) for i in idx]

    bdf = bd.astype(F32)
    state = [state_ref[b] for b in range(nb)]
    outs = [[] for _ in range(nb)]
    for i, (b, sl) in enumerate(items):
        gc = gcum[b][sl]
        g_last = gc[c - 1:c, :]
        u, w = uw[i]
        ws_qs = _dot(jnp.concatenate([w, q[b][sl] * jnp.exp(gc)], axis=0).astype(BF16), state[b].astype(BF16))
        v_new = (u - ws_qs[0:c]).astype(BF16)
        outs[b].append(ws_qs[c:2 * c] + _dot(a_qk[i].astype(BF16), _bd_tile(v_new, bd)))
        kd = k[b][sl] * jnp.exp(g_last - gc)
        state[b] = state[b] * jnp.exp(g_last) + _dot_tn(kd.astype(BF16), v_new) * bdf
    for b in range(nb):
        state_ref[b] = state[b]
        o = jnp.concatenate(outs[b], axis=0)
        o = o * lax.rsqrt(_dot_x2(o * o, bd) * (1.0 / HEAD_DIM) + EPS) * gn_ref[...]
        gate = a_ref[b, :, 768:1024].astype(F32)
        o_ref[b] = (o * (gate * jax.nn.sigmoid(gate))).astype(o_ref.dtype)


def _block_diag_ones(n, blk):
    i = np.arange(n)
    return jnp.asarray((i[:, None] // blk) == (i[None, :] // blk), F32)


def _gdn(pa, pba, conv_w, a_log, dt_bias, norm_g, batch, seq):
    ts = min(MIX_TS, seq)
    nst = seq // ts
    bd = _block_diag_ones(GROUP_WIDTH, HEAD_DIM).astype(BF16)
    r = np.arange(ts)
    lt = jnp.asarray((r[:, None] // CHUNK == r[None, :] // CHUNK) & (r[:, None] >= r[None, :]), BF16)
    lane = np.arange(GROUP_WIDTH)
    e = np.zeros((LANE, 2 * GROUP_WIDTH), np.float32)
    for h in range(N_HEADS):
        e[h, np.nonzero(lane // HEAD_DIM == h)[0]] = 1.0
        e[N_HEADS + h, GROUP_WIDTH + np.nonzero(lane // HEAD_DIM == h)[0]] = 1.0
    rep = lambda p: jnp.repeat(p.astype(F32), HEAD_DIM)[None, :]
    consts = [conv_w.astype(F32), rep(a_log), rep(dt_bias),
              jnp.tile(norm_g.astype(F32), N_HEADS)[None, :], bd, lt, jnp.asarray(e, BF16)]
    nb = GDN_NB if batch % GDN_NB == 0 else 1
    full = lambda a: pl.BlockSpec(a.shape, lambda b, s: (0, 0))
    out = pl.pallas_call(
        functools.partial(_gdn_kernel, ts=ts, nb=nb),
        name="gdn",
        grid=(batch // nb, nst),
        in_specs=[pl.BlockSpec((nb, ts, A_W), lambda b, s: (b, s, 0)),
                  pl.BlockSpec((nb, ts, LANE), lambda b, s: (b, s, 0))] + [full(a) for a in consts],
        out_specs=pl.BlockSpec((nb, ts, GROUP_WIDTH), lambda b, s: (b, s, 0)),
        out_shape=jax.ShapeDtypeStruct((batch, seq, GROUP_WIDTH), BF16),
        scratch_shapes=[pltpu.VMEM((nb, SUBLANE, 768), F32),
                        pltpu.VMEM((nb, GROUP_WIDTH, GROUP_WIDTH), F32)],
        compiler_params=_cparams(("parallel", "arbitrary")),
    )(pa.reshape(batch, seq, A_W), pba.reshape(batch, seq, LANE), *consts)
    return out.reshape(batch * seq, GROUP_WIDTH)


def _attn_kernel(qa_ref, qb_ref, k_ref, v_ref, bias_ref, lamv_ref, gn_ref, oa_ref, ob_ref,
                 va_ref, qt_ref, s_ref, p_ref, al_ref, mt_ref, m_ref, l_ref, acc_ref, *, lam_init, seq):
    t = ATT_T
    nq = seq // t
    i = pl.program_id(1)
    na = i + 1
    nlan = 2 * N_HEADS * t
    nsteps = nq + 1

    @pl.when(i == 0)
    def _stage_v():
        def body(j, carry):
            rows = pl.ds(pl.multiple_of(j * t, t), t)
            vt = v_ref[rows, :].astype(F32).T.astype(BF16)
            for h in range(N_HEADS):
                va_ref[j, h, 0:HEAD_DIM, :] = vt[h * HEAD_DIM:(h + 1) * HEAD_DIM, :]
                va_ref[j, h, HEAD_DIM:ATT_VROWS, :] = jnp.ones((ATT_VROWS - HEAD_DIM, t), BF16)
            return carry
        lax.fori_loop(0, seq // t, body, 0)

    feat = lax.broadcasted_iota(jnp.int32, (GROUP_WIDTH, t), 0) // DIFF_DH
    for w, q_ref in enumerate((qa_ref, qb_ref)):
        qt = (q_ref[...].astype(F32) * (DIFF_DH ** -0.5 * LOG2E)).T
        for idx in range(2 * N_HEADS):
            qt_ref[w, :, idx * t:(idx + 1) * t] = jnp.where(feat == idx, qt, 0.0).astype(BF16)
    m_ref[...] = jnp.full((2, 1, nlan), NEG, F32)
    l_ref[...] = jnp.zeros((2, 1, nlan), F32)
    acc_ref[...] = jnp.zeros((2, HEAD_DIM, nlan), F32)

    head_cols = [slice(2 * h * t, (2 * h + 2) * t) for h in range(N_HEADS)]

    def tile_of(s):
        if s >= nq // 2:
            bias = 2 if s == nq else (1 if s == nq - 1 else None)
            return 1, s - na, bias
        w = (s >= na).astype(jnp.int32)
        bias = jnp.where(w == 1, 0, jnp.where(s == na - 1, 2, jnp.where(s == na - 2, 1, 0)))
        return w, s - na * w, bias

    def scores(s, h):
        w, kt, bias = tile_of(s)
        rows = pl.ds(pl.multiple_of(kt * t, t), t)
        sc = _dot(k_ref[rows, :], qt_ref[w, :, head_cols[h]])
        if bias is not None:
            sc = sc + bias_ref[bias, :, head_cols[h]]
        s_ref[s % 2, :, head_cols[h]] = sc
        mt_ref[s % 2, :, head_cols[h]] = jnp.max(sc, axis=0, keepdims=True)

    def softmax(s, h):
        w, _, _ = tile_of(s)
        cols = head_cols[h]
        sc = s_ref[s % 2, :, cols]
        m_prev = m_ref[w, :, cols]
        m_new = jnp.maximum(m_prev, mt_ref[s % 2, :, cols])
        al_ref[s % 2, :, cols] = jnp.exp2(m_prev - m_new)
        m_ref[w, :, cols] = m_new
        p_ref[s % 2, :, cols] = jnp.exp2(sc - m_new).astype(BF16)

    def values(s, h):
        w, kt, _ = tile_of(s)
        cols = head_cols[h]
        alpha = al_ref[s % 2, :, cols]
        pv = _dot(va_ref[kt, h], p_ref[s % 2, :, cols])
        acc_ref[w, :, cols] = alpha * acc_ref[w, :, cols] + pv[0:HEAD_DIM]
        l_ref[w, :, cols] = alpha * l_ref[w, :, cols] + pv[HEAD_DIM:HEAD_DIM + 1]

    for s in range(nsteps + 2):
        for h in range(N_HEADS):
            if s < nsteps:
                scores(s, h)
            if 1 <= s <= nsteps:
                softmax(s - 1, h)
            if s >= 2:
                values(s - 2, h)

    lv = lamv_ref[...]
    lam = (jnp.exp(jnp.sum(lv[0:1] * lv[1:2], axis=1, keepdims=True))
           - jnp.exp(jnp.sum(lv[2:3] * lv[3:4], axis=1, keepdims=True)) + lam_init)
    for w, o_ref in enumerate((oa_ref, ob_ref)):
        inv_l = 1.0 / l_ref[w]
        outs = []
        for h in range(N_HEADS):
            c0 = slice(2 * h * t, (2 * h + 1) * t)
            c1 = slice((2 * h + 1) * t, (2 * h + 2) * t)
            oh = acc_ref[w, :, c0] * inv_l[:, c0] - lam * (acc_ref[w, :, c1] * inv_l[:, c1])
            oh = oh * lax.rsqrt(jnp.mean(oh * oh, axis=0, keepdims=True) + EPS)
            outs.append(oh)
        o = jnp.concatenate(outs, axis=0).T
        o_ref[...] = (o * gn_ref[...] * (1.0 - lam_init)).astype(o_ref.dtype)


def _t5_bucket(rel):
    nb = REL_BUCKETS // 2
    bucket = jnp.where(rel > 0, nb, 0)
    n = jnp.abs(rel)
    max_exact = nb // 2
    large = max_exact + (jnp.log(jnp.maximum(n, 1).astype(F32) / max_exact)
                         / math.log(REL_MAX_DIST / max_exact) * (nb - max_exact)).astype(jnp.int32)
    large = jnp.minimum(large, nb - 1)
    return bucket + jnp.where(n < max_exact, n, large)


def _attn_bias_tiles(rel_bias):
    t = ATT_T
    table = rel_bias.astype(F32)
    kk = jnp.arange(t)[:, None]
    qq = jnp.arange(t)[None, :]

    def expand(b):
        b = jnp.transpose(b, (0, 2, 1))
        b = jnp.broadcast_to(b[:, :, None, :], (t, N_HEADS, 2, t))
        return b.reshape(t, 2 * N_HEADS * t)

    table = table * LOG2E

    def lookup(bucket):
        onehot = (bucket[..., None] == jnp.arange(REL_BUCKETS)).astype(F32)
        return jnp.einsum('kqb,bh->kqh', onehot, table, precision=HI)

    diag = lookup(_t5_bucket(kk - qq))
    diag = jnp.where(((kk // CHUNK) <= (qq // CHUNK))[:, :, None], diag, NEG)
    near = lookup(_t5_bucket(kk - qq - t))
    far = lookup(_t5_bucket(jnp.full((1, 1), -(REL_MAX_DIST + 1), jnp.int32)))
    far = jnp.broadcast_to(far, (t, t, N_HEADS))
    return jnp.stack([jnp.zeros((t, 2 * N_HEADS * t), F32), expand(near - far), expand(diag - far)])


def _diff_attn(pb, lam_vecs, lam_init, bias_tiles, norm_g, batch, seq):
    t = ATT_T
    nq = seq // t
    gn = jnp.tile(norm_g.astype(F32), N_HEADS)[None, :]
    lamv = lam_vecs.astype(F32)
    nlan = 2 * N_HEADS * t
    assert nq % 2 == 0
    nh = nq // 2
    full = lambda a: pl.BlockSpec(a.shape, lambda b, i: (0,) * a.ndim)
    half = jax.ShapeDtypeStruct((batch * nh * t, GROUP_WIDTH), BF16)
    lo, hi = pl.pallas_call(
        functools.partial(_attn_kernel, lam_init=lam_init, seq=seq),
        name="diffattn",
        grid=(batch, nh),
        in_specs=[pl.BlockSpec((t, GROUP_WIDTH), lambda b, i: (b * nq + i, 0)),
                  pl.BlockSpec((t, GROUP_WIDTH), lambda b, i: (b * nq + nq - 1 - i, 0)),
                  pl.BlockSpec((seq, GROUP_WIDTH), lambda b, i: (b, 1)),
                  pl.BlockSpec((seq, GROUP_WIDTH), lambda b, i: (b, 2)),
                  full(bias_tiles), full(lamv), full(gn)],
        out_specs=[pl.BlockSpec((t, GROUP_WIDTH), lambda b, i: (b * nh + i, 0)),
                   pl.BlockSpec((t, GROUP_WIDTH), lambda b, i: (b * nh + nh - 1 - i, 0))],
        out_shape=[half, half],
        scratch_shapes=[pltpu.VMEM((seq // t, N_HEADS, ATT_VROWS, t), BF16),
                        pltpu.VMEM((2, GROUP_WIDTH, nlan), BF16),
                        pltpu.VMEM((2, t, nlan), F32),
                        pltpu.VMEM((2, t, nlan), BF16),
                        pltpu.VMEM((2, 1, nlan), F32),
                        pltpu.VMEM((2, 1, nlan), F32),
                        pltpu.VMEM((2, 1, nlan), F32), pltpu.VMEM((2, 1, nlan), F32),
                        pltpu.VMEM((2, HEAD_DIM, nlan), F32)],
        compiler_params=_cparams(("parallel", "arbitrary")),
    )(pb, pb, pb, pb, bias_tiles, lamv, gn)
    out = jnp.concatenate([lo.reshape(batch, nh * t, GROUP_WIDTH), hi.reshape(batch, nh * t, GROUP_WIDTH)],
                          axis=1)
    return out.reshape(batch * seq, GROUP_WIDTH)


def _rglru_kernel(c_ref, convw_ref, convb_ref, wa_ref, ba_ref, wx_ref, bx_ref, ap_ref, o_ref,
                  tail_ref, h_ref, *, ts):
    @pl.when(pl.program_id(1) == 0)
    def _init():
        tail_ref[...] = jnp.zeros_like(tail_ref)
        h_ref[...] = jnp.zeros_like(h_ref)

    xb = c_ref[:, 0:256].astype(F32)
    gb = c_ref[:, 256:512].astype(F32)
    xc = _causal_conv(xb, tail_ref[...], convw_ref[...]) + convb_ref[...]
    tail_ref[...] = xb[ts - SUBLANE:ts, :]
    xcb = xc.astype(BF16)
    gate_a = jax.nn.sigmoid(_dot(xcb, wa_ref[...]) + ba_ref[...])
    gate_x = jax.nn.sigmoid(_dot(xcb, wx_ref[...]) + bx_ref[...])
    log_a = -RG_C * gate_a * _softplus(ap_ref[...])
    a = jnp.exp(log_a)
    th = jnp.tanh(log_a)
    u = xc * gate_x * jnp.sqrt(-2.0 * th / (1.0 - th))
    row = lax.broadcasted_iota(jnp.int32, (ts, GROUP_WIDTH), 0)
    d = 1
    while d < ts:
        keep = row >= d
        a_sh = jnp.where(keep, pltpu.roll(a, d, 0), 1.0)
        u_sh = jnp.where(keep, pltpu.roll(u, d, 0), 0.0)
        u = u + a * u_sh
        a = a * a_sh
        d *= 2
    h = u + a * h_ref[...]
    h_ref[...] = h[ts - 1:ts, :]
    gelu = 0.5 * gb * (1.0 + jnp.tanh(math.sqrt(2.0 / math.pi) * (gb + 0.044715 * (gb * gb * gb))))
    o_ref[...] = (h * gelu).astype(o_ref.dtype)


def _block_diag_weight(w):
    nb, wi, wo = w.shape
    out = jnp.zeros((nb * wi, nb * wo), w.dtype)
    for i in range(nb):
        out = out.at[i * wi:(i + 1) * wi, i * wo:(i + 1) * wo].set(w[i])
    return out


def _rglru(pc, conv_w, conv_b, w_a, b_a, w_x, b_x, a_param, batch, seq):
    ts = min(MIX_TS, seq)
    nst = seq // ts
    r = lambda p: p.astype(F32)[None, :]
    consts = [conv_w.astype(F32), r(conv_b), _block_diag_weight(w_a).astype(BF16), r(b_a),
              _block_diag_weight(w_x).astype(BF16), r(b_x), r(a_param)]
    full = lambda a: pl.BlockSpec(a.shape, lambda b, s: (0, 0))
    return pl.pallas_call(
        functools.partial(_rglru_kernel, ts=ts),
        name="rglru",
        grid=(batch, nst),
        in_specs=[pl.BlockSpec((ts, C_W), lambda b, s: (b * nst + s, 0))] + [full(a) for a in consts],
        out_specs=pl.BlockSpec((ts, GROUP_WIDTH), lambda b, s: (b * nst + s, 0)),
        out_shape=jax.ShapeDtypeStruct((batch * seq, GROUP_WIDTH), BF16),
        scratch_shapes=[pltpu.VMEM((SUBLANE, GROUP_WIDTH), F32), pltpu.VMEM((1, GROUP_WIDTH), F32)],
        compiler_params=_cparams(("parallel", "arbitrary")),
    )(pc, *consts)


def _gla_kernel(d_ref, lr_ref, wlr_ref, blr_ref, gn_ref, bd_ref, bdt_ref, lt_ref, o_ref, state_ref,
                *, ts, nb):
    @pl.when(pl.program_id(1) == 0)
    def _init():
        state_ref[...] = jnp.zeros_like(state_ref)

    c = CHUNK
    bd = bd_ref[...]
    bdt = bdt_ref[...]
    lt = lt_ref[...]
    kmasks = _head_mask(N_HEADS * GLA_DK, GLA_DK)
    ri = lax.broadcasted_iota(jnp.int32, (c, GROUP_WIDTH), 0)
    cj = lax.broadcasted_iota(jnp.int32, (c, GROUP_WIDTH), 1) % HEAD_DIM
    causal = ri >= cj
    q, k, gcum = [], [], []
    for b in range(nb):
        q.append(d_ref[b, :, 0:128].astype(F32) * (GLA_DK ** -0.5))
        k.append(d_ref[b, :, 128:256].astype(F32))
        lr_hi, lr_lo = _split2(lr_ref[b])
        w_hi, w_lo = _split2(wlr_ref[...])
        z = _dot(lr_hi, w_hi) + (_dot(lr_hi, w_lo) + _dot(lr_lo, w_hi)) + blr_ref[...]
        la = (jnp.minimum(z, 0.0) - jnp.log1p(jnp.exp(-jnp.abs(z)))) * (1.0 / GLA_TAU)
        la_hi, la_lo = _split2(la)
        la_lo2 = (la - la_hi.astype(F32) - la_lo.astype(F32)).astype(BF16)
        gcum.append(_dot(lt, la_hi) + (_dot(lt, la_lo) + _dot(lt, la_lo2)))

    items = [(b, slice(ci * c, (ci + 1) * c)) for ci in range(ts // c) for b in range(nb)]
    o_intra, upd = [], []
    for b, sl in items:
        qc, kc, gc = q[b][sl], k[b][sl], gcum[b][sl]
        vc = d_ref[b, sl, 256:512]
        ref = gc[c // 2:c // 2 + 1, :]
        kst = _stack_heads(kc * jnp.exp(ref - gc), kmasks).astype(BF16)
        a_in = _dot_nt((qc * jnp.exp(gc - ref)).astype(BF16), kst)
        a_in = jnp.where(causal, a_in, 0.0).astype(BF16)
        o_intra.append(_dot(a_in, _bd_tile(vc, bd)))
        kd = kc * jnp.exp(gc[c - 1:c, :] - gc)
        upd.append(_dot_tn(vc, kd.astype(BF16)) * bdt)

    state = [state_ref[b] for b in range(nb)]
    outs = [[] for _ in range(nb)]
    for i, (b, sl) in enumerate(items):
        gc = gcum[b][sl]
        o_inter = _dot_nt((q[b][sl] * jnp.exp(gc)).astype(BF16), state[b].astype(BF16))
        outs[b].append(o_intra[i] + o_inter)
        state[b] = state[b] * jnp.exp(gc[c - 1:c, :]) + upd[i]
    for b in range(nb):
        state_ref[b] = state[b]
        o = jnp.concatenate(outs[b], axis=0)
        o = o * lax.rsqrt(_dot_x2(o * o, bd) * (1.0 / HEAD_DIM) + EPS) * gn_ref[...]
        rt = d_ref[b, :, 512:768].astype(F32)
        o_ref[b] = (o * (rt * jax.nn.sigmoid(rt))).astype(o_ref.dtype)


def _gla(pd, plr, w_lr, b_lr, norm_g, batch, seq):
    ts = min(MIX_TS, seq)
    nst = seq // ts
    kw = N_HEADS * GLA_DK
    wlr = jnp.zeros((LANE, kw), F32).at[SIDE_LR:SIDE_LR + GLA_RANK, :].set(w_lr.astype(F32))
    bd = _block_diag_ones(GROUP_WIDTH, HEAD_DIM).astype(BF16)
    iv = np.arange(GROUP_WIDTH)[:, None] // HEAD_DIM
    ik = np.arange(kw)[None, :] // GLA_DK
    bdt = jnp.asarray(iv == ik, F32)
    r = np.arange(ts)
    lt = jnp.asarray((r[:, None] // CHUNK == r[None, :] // CHUNK) & (r[:, None] >= r[None, :]), BF16)
    consts = [wlr, b_lr.astype(F32)[None, :], jnp.tile(norm_g.astype(F32), N_HEADS)[None, :], bd, bdt, lt]
    nb = GDN_NB if batch % GDN_NB == 0 else 1
    full = lambda a: pl.BlockSpec(a.shape, lambda b, s: (0, 0))
    out = pl.pallas_call(
        functools.partial(_gla_kernel, ts=ts, nb=nb),
        name="gla",
        grid=(batch // nb, nst),
        in_specs=[pl.BlockSpec((nb, ts, D_W), lambda b, s: (b, s, 0)),
                  pl.BlockSpec((nb, ts, LANE), lambda b, s: (b, s, 0))] + [full(a) for a in consts],
        out_specs=pl.BlockSpec((nb, ts, GROUP_WIDTH), lambda b, s: (b, s, 0)),
        out_shape=jax.ShapeDtypeStruct((batch, seq, GROUP_WIDTH), BF16),
        scratch_shapes=[pltpu.VMEM((nb, GROUP_WIDTH, kw), F32)],
        compiler_params=_cparams(("parallel", "arbitrary")),
    )(pd.reshape(batch, seq, D_W), plr.reshape(batch, seq, LANE), *consts)
    return out.reshape(batch * seq, GROUP_WIDTH)


def _mix_outproj(x_ref, ma_ref, mb_ref, mc_ref, md_ref, wo_ref):
    mix = jnp.concatenate([ma_ref[...], mb_ref[...], mc_ref[...], md_ref[...]], axis=1)
    return x_ref[...] + _dot(mix, wo_ref[...])


def _ffn_kernel(x_ref, ma_ref, mb_ref, mc_ref, md_ref, wo_ref, g_ref, wg_ref, wu_ref, wd_ref,
                o_ref, h_ref, acc_ref):
    f = pl.program_id(1)

    @pl.when(f == 0)
    def _first():
        x1 = _mix_outproj(x_ref, ma_ref, mb_ref, mc_ref, md_ref, wo_ref)
        acc_ref[...] = x1
        h_ref[...] = _rms(x1, g_ref[...]).astype(BF16)

    h = h_ref[...]
    gt = _dot(h, wg_ref[...])
    act = (gt * jax.nn.sigmoid(gt) * _dot(h, wu_ref[...])).astype(BF16)
    acc_ref[...] += _dot(act, wd_ref[...])

    @pl.when(f == pl.num_programs(1) - 1)
    def _last():
        o_ref[...] = acc_ref[...]


def _outproj_ffn(x, mixes, w_out, g, w_gate, w_up, w_down):
    t = x.shape[0]
    tm = min(FFN_TM, t)
    tf = FFN_TF
    nf = D_FF // tf
    row = lambda w: pl.BlockSpec((tm, w), lambda i, f: (i, 0))
    return pl.pallas_call(
        _ffn_kernel,
        name="outproj_ffn",
        grid=(t // tm, nf),
        in_specs=[row(D_MODEL)] + [row(GROUP_WIDTH)] * 4 + [
            pl.BlockSpec((D_MODEL, D_MODEL), lambda i, f: (0, 0)),
            pl.BlockSpec((1, D_MODEL), lambda i, f: (0, 0)),
            pl.BlockSpec((D_MODEL, tf), lambda i, f: (0, f)),
            pl.BlockSpec((D_MODEL, tf), lambda i, f: (0, f)),
            pl.BlockSpec((tf, D_MODEL), lambda i, f: (f, 0))],
        out_specs=row(D_MODEL),
        out_shape=jax.ShapeDtypeStruct((t, D_MODEL), F32),
        scratch_shapes=[pltpu.VMEM((tm, D_MODEL), BF16), pltpu.VMEM((tm, D_MODEL), F32)],
        compiler_params=_cparams(("parallel", "arbitrary")),
    )(x, *mixes, w_out, g, w_gate, w_up, w_down)


def _router_kernel(x_ref, ma_ref, mb_ref, mc_ref, md_ref, wo_ref, g_ref, wr_ref, x1_ref, h_ref, r_ref):
    x1 = _mix_outproj(x_ref, ma_ref, mb_ref, mc_ref, md_ref, wo_ref)
    x1_ref[...] = x1
    h = _rms(x1, g_ref[...])
    h_hi, h_lo = _split2(h)
    h_ref[...] = h_hi
    lane = lax.broadcasted_iota(jnp.int32, (x1.shape[0], LANE), 1)
    hh = _dot(h_hi, wr_ref[...])
    logits = hh[:, 0:LANE] + (hh[:, LANE:2 * LANE] + _dot(h_lo, wr_ref[:, 0:LANE]))
    logits = jnp.where(lane < N_EXPERTS, logits, NEG)
    m1 = jnp.max(logits, axis=1, keepdims=True)
    e1 = jnp.min(jnp.where(logits == m1, lane, LANE), axis=1, keepdims=True)
    rest = jnp.where(lane == e1, NEG, logits)
    m2 = jnp.max(rest, axis=1, keepdims=True)
    e2 = jnp.min(jnp.where(rest == m2, lane, LANE), axis=1, keepdims=True)
    ex = jnp.exp(m2 - m1)
    w1 = 1.0 / (1.0 + ex)
    w2 = ex / (1.0 + ex)
    r_ref[...] = jnp.where(lane == 0, e1.astype(F32),
                           jnp.where(lane == 1, e2.astype(F32),
                                     jnp.where(lane == 2, w1, jnp.where(lane == 3, w2, 0.0))))


def _outproj_router(x, mixes, w_out, g, w_router):
    t = x.shape[0]
    tm = min(ROUTER_TM, t)
    wr = jnp.pad(w_router.astype(F32), ((0, 0), (0, LANE - N_EXPERTS)))
    wr_hi = wr.astype(BF16)
    wr_cat = jnp.concatenate([wr_hi, (wr - wr_hi.astype(F32)).astype(BF16)], axis=1)
    row = lambda w: pl.BlockSpec((tm, w), lambda i: (i, 0))
    full = lambda a: pl.BlockSpec(a.shape, lambda i: (0, 0))
    return pl.pallas_call(
        _router_kernel,
        name="outproj_router",
        grid=(t // tm,),
        in_specs=[row(D_MODEL)] + [row(GROUP_WIDTH)] * 4 + [full(w_out), full(g), full(wr_cat)],
        out_specs=[row(D_MODEL), row(D_MODEL), row(LANE)],
        out_shape=[jax.ShapeDtypeStruct((t, D_MODEL), F32), jax.ShapeDtypeStruct((t, D_MODEL), BF16),
                   jax.ShapeDtypeStruct((t, LANE), F32)],
        compiler_params=_cparams(("parallel",)),
    )(x, *mixes, w_out, g, wr_cat)


def _moe_kernel(te_ref, na_ref, x_ref, wg_ref, wu_ref, wd_ref, o_ref, acc_ref):
    i = pl.program_id(0)
    f = pl.program_id(1)

    @pl.when(i < na_ref[0])
    def _active():
        @pl.when(f == 0)
        def _zero():
            acc_ref[...] = jnp.zeros_like(acc_ref)

        x = x_ref[...]
        gt = _dot(x, wg_ref[...].astype(BF16))
        act = (gt * jax.nn.sigmoid(gt) * _dot(x, wu_ref[...].astype(BF16))).astype(BF16)
        acc_ref[...] += _dot(act, wd_ref[...].astype(BF16))

        @pl.when(f == pl.num_programs(1) - 1)
        def _last():
            o_ref[...] = acc_ref[...].astype(o_ref.dtype)

    @pl.when(jnp.logical_and(i >= na_ref[0], f == pl.num_programs(1) - 1))
    def _unused_tile():
        o_ref[...] = jnp.zeros_like(o_ref)


def _moe_experts(xs, tile_e, n_active, w_gate, w_up, w_down, n_tiles):
    tm, tf = MOE_TM, MOE_TF
    nf = D_FF_EXPERT // tf

    def tile(i, na):
        return jnp.minimum(i, na[0] - 1)

    def fidx(i, f, na):
        return jnp.where(i < na[0], f, nf - 1)

    grid_spec = pltpu.PrefetchScalarGridSpec(
        num_scalar_prefetch=2,
        grid=(n_tiles, nf),
        in_specs=[
            pl.BlockSpec((tm, D_MODEL), lambda i, f, te, na: (tile(i, na), 0)),
            pl.BlockSpec((None, D_MODEL, tf), lambda i, f, te, na: (te[tile(i, na)], 0, fidx(i, f, na))),
            pl.BlockSpec((None, D_MODEL, tf), lambda i, f, te, na: (te[tile(i, na)], 0, fidx(i, f, na))),
            pl.BlockSpec((None, tf, D_MODEL), lambda i, f, te, na: (te[tile(i, na)], fidx(i, f, na), 0)),
        ],
        out_specs=pl.BlockSpec((tm, D_MODEL), lambda i, f, te, na: (i, 0)),
        scratch_shapes=[pltpu.VMEM((tm, D_MODEL), F32)],
    )
    return pl.pallas_call(
        _moe_kernel,
        name="moe_experts",
        grid_spec=grid_spec,
        out_shape=jax.ShapeDtypeStruct((n_tiles * tm, D_MODEL), BF16),
        compiler_params=_cparams(("arbitrary", "arbitrary")),
    )(tile_e, n_active, xs, w_gate, w_up, w_down)


def _route(route, t):
    tm = MOE_TM
    n_assign = t * TOP_K
    flat_e = jnp.concatenate([route[:, k].astype(jnp.int32) for k in range(TOP_K)])
    onehot = (flat_e[:, None] == jnp.arange(N_EXPERTS)[None, :]).astype(jnp.int32)
    counts = jnp.sum(onehot, axis=0)
    padded = (counts + tm - 1) // tm * tm
    pad_end = jnp.cumsum(padded)
    pad_start = pad_end - padded
    grp_start = jnp.cumsum(counts) - counts
    dest = jnp.sum(onehot * (jnp.cumsum(onehot, axis=0) - 1 + pad_start[None, :]), axis=1)
    n_tiles = -(-n_assign // tm) + N_EXPERTS
    tile_e = jnp.minimum(jnp.searchsorted(pad_end, jnp.arange(n_tiles) * tm, side='right'),
                         N_EXPERTS - 1).astype(jnp.int32)
    n_active = (pad_end[-1] // tm).astype(jnp.int32).reshape(1)
    order = jnp.argsort(flat_e, stable=True).astype(jnp.int32)
    rank_s = ((jnp.arange(n_tiles, dtype=jnp.int32) * tm - pad_start[tile_e])[:, None]
              + jnp.arange(tm, dtype=jnp.int32)[None, :])
    src = jnp.minimum(grp_start[tile_e][:, None] + rank_s, n_assign - 1)
    slot = jnp.arange(n_tiles * tm, dtype=jnp.int32).reshape(n_tiles, tm)
    slot_tok = jnp.where(rank_s < counts[tile_e][:, None], order[src] % t, slot % t).reshape(n_tiles * tm)
    return slot_tok, tile_e, n_active, [dest[k * t:(k + 1) * t] for k in range(TOP_K)], n_tiles


def _combine_kernel(x_ref, y0_ref, y1_ref, r_ref, g_ref, o_ref, *, final):
    r = r_ref[...]
    y = x_ref[...] + (y0_ref[...].astype(F32) * r[:, TOP_K:TOP_K + 1]
                      + y1_ref[...].astype(F32) * r[:, TOP_K + 1:TOP_K + 2])
    o_ref[...] = _rms(y, g_ref[...]) if final else y


def _moe_combine(x1, y0, y1, route, g, final):
    t = x1.shape[0]
    tm = min(FFN_TM, t)
    row = pl.BlockSpec((tm, D_MODEL), lambda i: (i, 0))
    return pl.pallas_call(
        functools.partial(_combine_kernel, final=final),
        name="moe_combine",
        grid=(t // tm,),
        in_specs=[row, row, row, pl.BlockSpec((tm, LANE), lambda i: (i, 0)),
                  pl.BlockSpec((1, D_MODEL), lambda i: (0, 0))],
        out_specs=row,
        out_shape=jax.ShapeDtypeStruct((t, D_MODEL), F32),
        compiler_params=_cparams(("parallel",)),
    )(x1, y0, y1, route, g)


def _final_norm_kernel(x_ref, g_ref, o_ref):
    o_ref[...] = _rms(x_ref[...], g_ref[...])


def _split_w_in(w):
    def pad(a, width):
        return jnp.pad(a, ((0, 0), (0, width - a.shape[1])))
    a_end = A_W + 2 * N_HEADS
    b_end = a_end + B_W
    c_end = b_end + C_W
    d_end = c_end + D_W
    side = pad(jnp.concatenate([w[:, A_W:a_end], w[:, d_end:]], axis=1), LANE)
    parts = [(w[:, 0:A_W], BF16), (w[:, a_end:b_end], BF16), (w[:, b_end:c_end], BF16),
             (w[:, c_end:d_end], BF16), (side, F32)]
    return [(m.astype(BF16), dt) for m, dt in parts]


def kernel(x, norm_mix, w_in, a_conv, a_A_log, a_dt_bias, a_norm, b_lambda, b_norm, rel_bias,
           c_conv_w, c_conv_b, c_w_a, c_b_a, c_w_x, c_b_x, c_a_param, d_w_lr, d_b_lr, d_norm,
           w_out, norm_ffn, ffn_w_gate, ffn_w_up, ffn_w_down, moe_router, moe_w_gate, moe_w_up,
           moe_w_down, norm_final):
    batch, seq, _ = x.shape
    depth = w_in.shape[0]
    t = batch * seq
    xt = x.reshape(t, D_MODEL).astype(F32)
    bias_tiles = _attn_bias_tiles(rel_bias)
    row = lambda p: p.astype(F32)[None, :]
    out = None
    for l in range(depth):
        pa, pb, pc, pd, pside = _inproj(xt, row(norm_mix[l]), _split_w_in(w_in[l]))
        pba = plr = pside
        lam_init = 0.8 - 0.6 * math.exp(-0.3 * l)
        mixes = (
            _gdn(pa, pba, a_conv[l], a_A_log[l], a_dt_bias[l], a_norm[l], batch, seq),
            _diff_attn(pb, b_lambda[l], lam_init, bias_tiles, b_norm[l], batch, seq),
            _rglru(pc, c_conv_w[l], c_conv_b[l], c_w_a[l], c_b_a[l], c_w_x[l], c_b_x[l],
                   c_a_param[l], batch, seq),
            _gla(pd, plr, d_w_lr[l], d_b_lr[l], d_norm[l], batch, seq),
        )
        wo = w_out[l].astype(BF16)
        if l % 2 == 0:
            j = l // 2
            xt = _outproj_ffn(xt, mixes, wo, row(norm_ffn[l]), ffn_w_gate[j].astype(BF16),
                              ffn_w_up[j].astype(BF16), ffn_w_down[j].astype(BF16))
            out = None
        else:
            j = l // 2
            final = l == depth - 1
            take = lambda a, i: a.at[i].get(mode='promise_in_bounds')
            x1, h, route = _outproj_router(xt, mixes, wo, row(norm_ffn[l]), moe_router[j])
            slot_tok, tile_e, n_active, dest, n_tiles = _route(route, t)
            xs = take(h, slot_tok)
            ys = _moe_experts(xs, tile_e, n_active, moe_w_gate[j], moe_w_up[j], moe_w_down[j], n_tiles)
            y = _moe_combine(x1, take(ys, dest[0]), take(ys, dest[1]), route, row(norm_final), final)
            if final:
                out = y
            else:
                xt = y
    if out is None:
        tm = min(FFN_TM, t)
        rowspec = pl.BlockSpec((tm, D_MODEL), lambda i: (i, 0))
        out = pl.pallas_call(
            _final_norm_kernel, name="final_norm", grid=(t // tm,),
            in_specs=[rowspec, pl.BlockSpec((1, D_MODEL), lambda i: (0, 0))],
            out_specs=rowspec, out_shape=jax.ShapeDtypeStruct((t, D_MODEL), F32),
            compiler_params=_cparams(("parallel",)),
        )(xt, row(norm_final))
    return out.reshape(batch, seq, D_MODEL).astype(x.dtype)
```

```python
import functools
import math

import jax
import jax.numpy as jnp
import numpy as np
from jax import lax
from jax.experimental import pallas as pl
from jax.experimental.pallas import tpu as pltpu

D_MODEL = 1024
CHUNK = 64
N_HEADS = 4
HEAD_DIM = 64
GROUP_WIDTH = 256
CONV_WIDTH = 4
DIFF_DH = 32
Q_BLOCK = 128
REL_BUCKETS = 32
REL_MAX_DIST = 128
RG_C = 8.0
GLA_DK = 32
GLA_RANK = 16
GLA_TAU = 16.0
D_FF = 2816
N_EXPERTS = 8
TOP_K = 2
D_FF_EXPERT = 3584
EPS = 1e-6
assert CHUNK == HEAD_DIM

LANE = 128
SUBLANE = 8
VMEM_LIMIT = 56 * 1024 * 1024

F32 = jnp.float32
BF16 = jnp.bfloat16
HI = lax.Precision.HIGHEST
NEG = -1e30
LOG2E = math.log2(math.e)

A_W = 1024
B_W = 768
C_W = 512
D_W = 768
SIDE_LR = 2 * N_HEADS

PROJ_TM = 1024
MIX_TS = 256
GDN_BLK = 16
RGLRU_NB = 4
GDN_NB = 4
ATT_T = 128
ATT_VROWS = 80
FFN_TM = 512
ROUTER_TM = 1024
FFN_TF = 1408
MOE_TM = 1024
MOE_TF = 512


def _cparams(sem):
    return pltpu.CompilerParams(dimension_semantics=sem, vmem_limit_bytes=VMEM_LIMIT)


def _dot(a, b, precision=None):
    return jnp.dot(a, b, preferred_element_type=F32, precision=precision)


def _dot_nt(a, b, precision=None):
    return lax.dot_general(a, b, (((1,), (1,)), ((), ())), preferred_element_type=F32,
                           precision=precision)


def _dot_tn(a, b, precision=None):
    return lax.dot_general(a, b, (((0,), (0,)), ((), ())), preferred_element_type=F32,
                           precision=precision)


def _softplus(x):
    return jnp.maximum(x, 0.0) + jnp.log1p(jnp.exp(-jnp.abs(x)))


def _rms(x, g):
    return x * lax.rsqrt(jnp.mean(x * x, axis=-1, keepdims=True) + EPS) * g


def _causal_conv(x, tail, w):
    row = lax.broadcasted_iota(jnp.int32, (SUBLANE, x.shape[1]), 0)
    y = x * w[CONV_WIDTH - 1:CONV_WIDTH, :]
    for d in range(1, CONV_WIDTH):
        rolled = pltpu.roll(x, d, 0)
        first = jnp.where(row < d, pltpu.roll(tail, d, 0), rolled[:SUBLANE])
        shifted = jnp.concatenate([first, rolled[SUBLANE:]], axis=0)
        y = y + shifted * w[CONV_WIDTH - 1 - d:CONV_WIDTH - d, :]
    return y


def _head_mask(width, per_head):
    lane = lax.broadcasted_iota(jnp.int32, (1, width), 1)
    return [(lane // per_head) == h for h in range(N_HEADS)]


def _stack_heads(x, masks):
    return jnp.concatenate([jnp.where(m, x, 0.0) for m in masks], axis=0)


def _unstack_heads(r, masks, c):
    out = jnp.where(masks[0], r[0:c], 0.0)
    for h in range(1, N_HEADS):
        out = out + jnp.where(masks[h], r[h * c:(h + 1) * c], 0.0)
    return out


def _inproj_kernel(x_ref, g_ref, *refs):
    n = len(refs) // 2
    h = _rms(x_ref[...], g_ref[...]).astype(BF16)
    for w_ref, o_ref in zip(refs[:n], refs[n:]):
        o_ref[...] = _dot(h, w_ref[...]).astype(o_ref.dtype)


def _inproj(x, g, weights):
    t = x.shape[0]
    tm = min(PROJ_TM, t)
    row = lambda w: pl.BlockSpec((tm, w), lambda i: (i, 0))
    full = lambda a: pl.BlockSpec(a.shape, lambda i: (0, 0))
    ws = [w for w, _ in weights]
    return pl.pallas_call(
        _inproj_kernel,
        name="inproj",
        grid=(t // tm,),
        in_specs=[row(D_MODEL), full(g)] + [full(w) for w in ws],
        out_specs=[row(w.shape[1]) for w in ws],
        out_shape=[jax.ShapeDtypeStruct((t, w.shape[1]), dt) for w, dt in weights],
        compiler_params=_cparams(("parallel",)),
    )(x, g, *ws)


def _split2(x):
    hi = x.astype(BF16)
    return hi, (x - hi.astype(F32)).astype(BF16)


def _dot_x2(x, w):
    hi, lo = _split2(x)
    return _dot(hi, w) + _dot(lo, w)


def _bd_tile(x, bd):
    return jnp.concatenate([x] * N_HEADS, axis=0) * bd


def _dot_bd(a, bs, bd):
    ab = a.astype(BF16)
    return [_dot(ab, _bd_tile(b.astype(BF16), bd)) for b in bs]


def _gdn_kernel(a_ref, ba_ref, convw_ref, alog_ref, dtb_ref, gn_ref, bd_ref, lt_ref, e_ref,
                o_ref, tail_ref, state_ref, *, ts, nb):
    @pl.when(pl.program_id(1) == 0)
    def _init():
        tail_ref[...] = jnp.zeros_like(tail_ref)
        state_ref[...] = jnp.zeros_like(state_ref)

    c = CHUNK
    bd = bd_ref[...]
    lt = lt_ref[...]
    q, k, v, beta, gcum = [], [], [], [], []
    for b in range(nb):
        xin = a_ref[b, :, 0:768].astype(F32)
        y = _causal_conv(xin, tail_ref[b], convw_ref[...])
        tail_ref[b] = xin[ts - SUBLANE:ts, :]
        y = y * jax.nn.sigmoid(y)
        qb, kb_ = y[:, 0:256], y[:, 256:512]
        q.append(qb * lax.rsqrt(_dot_x2(qb * qb, bd) + EPS) * (HEAD_DIM ** -0.5))
        k.append(kb_ * lax.rsqrt(_dot_x2(kb_ * kb_, bd) + EPS))
        v.append(y[:, 512:768])
        e = _dot_x2(ba_ref[b], e_ref[...])
        beta.append(jax.nn.sigmoid(e[:, 0:256]))
        g = -jnp.exp(alog_ref[...]) * _softplus(e[:, 256:512] + dtb_ref[...])
        g_hi, g_lo = _split2(g)
        g_lo2 = (g - g_hi.astype(F32) - g_lo.astype(F32)).astype(BF16)
        gcum.append(_dot(lt, g_hi) + (_dot(lt, g_lo) + _dot(lt, g_lo2)))

    ri = lax.broadcasted_iota(jnp.int32, (c, GROUP_WIDTH), 0)
    cj = lax.broadcasted_iota(jnp.int32, (c, GROUP_WIDTH), 1) % HEAD_DIM
    causal = ri >= cj
    strict = ri > cj
    diag = ri == cj
    same_blk = (ri // GDN_BLK) == (cj // GDN_BLK)
    eye = diag.astype(F32)
    masks = _head_mask(GROUP_WIDTH, HEAD_DIM)
    items = [(b, slice(ci * c, (ci + 1) * c)) for ci in range(ts // c) for b in range(nb)]
    idx = range(len(items))

    a_qk, p0, nn, rhs_u, rhs_w = [], [], [], [], []
    for b, sl in items:
        kc, gc = k[b][sl], gcum[b][sl]
        kb = kc * beta[b][sl]
        grow = jnp.sum(jnp.where(diag, gc, 0.0), axis=0, keepdims=True)
        gamma = jnp.exp(jnp.where(causal, gc - grow, NEG))
        kst = _stack_heads(kc, masks).astype(BF16)
        aa = _dot_nt(jnp.concatenate([kb, q[b][sl]], axis=0).astype(BF16), kst)
        a_kk = jnp.where(strict, aa[0:c] * gamma, 0.0)
        a_qk.append(aa[c:2 * c] * gamma)
        p0.append(jnp.where(same_blk, -a_kk, 0.0))
        nn.append(jnp.where(same_blk, 0.0, a_kk))
        rhs_u.append(v[b][sl] * beta[b][sl])
        rhs_w.append(kb * jnp.exp(gc))
    t1 = [eye + p for p in p0]
    p1 = [_dot_bd(p0[i], [p0[i]], bd)[0] for i in idx]
    pr = [_dot_bd(p1[i], [p1[i], t1[i]], bd) for i in idx]
    p2 = [x[0] for x in pr]
    t2 = [t1[i] + pr[i][1] for i in idx]
    pr = [_dot_bd(p2[i], [p2[i], t2[i]], bd) for i in idx]
    p3 = [x[0] for x in pr]
    t3 = [t2[i] + pr[i][1] for i in idx]
    dinv = [t3[i] + _dot_bd(p3[i], [t3[i]], bd)[0] for i in idx]
    m1 = [_dot_bd(dinv[i], [nn[i]], bd)[0] for i in idx]
    m2 = [_dot_bd(m1[i], [m1[i]], bd)[0] for i in idx]
    im = [eye - m for m in m1]
    qq = [im[i] + _dot_bd(im[i], [m2[i]], bd)[0] for i in idx]
    inv = [_dot_bd(qq[i], [dinv[i]], bd)[0] for i in idx]
    uw = [_dot_bd(inv[i], [rhs_u[i], rhs_w[i]], bd) for i in idx]

    bdf = bd.astype(F32)
    state = [state_ref[b] for b in range(nb)]
    outs = [[] for _ in range(nb)]
    for i, (b, sl) in enumerate(items):
        gc = gcum[b][sl]
        g_last = gc[c - 1:c, :]
        u, w = uw[i]
        ws_qs = _dot(jnp.concatenate([w, q[b][sl] * jnp.exp(gc)], axis=0).astype(BF16), state[b].astype(BF16))
        v_new = (u - ws_qs[0:c]).astype(BF16)
        outs[b].append(ws_qs[c:2 * c] + _dot(a_qk[i].astype(BF16), _bd_tile(v_new, bd)))
        kd = k[b][sl] * jnp.exp(g_last - gc)
        state[b] = state[b] * jnp.exp(g_last) + _dot_tn(kd.astype(BF16), v_new) * bdf
    for b in range(nb):
        state_ref[b] = state[b]
        o = jnp.concatenate(outs[b], axis=0)
        o = o * lax.rsqrt(_dot_x2(o * o, bd) * (1.0 / HEAD_DIM) + EPS) * gn_ref[...]
        gate = a_ref[b, :, 768:1024].astype(F32)
        o_ref[b] = (o * (gate * jax.nn.sigmoid(gate))).astype(o_ref.dtype)


def _block_diag_ones(n, blk):
    i = np.arange(n)
    return jnp.asarray((i[:, None] // blk) == (i[None, :] // blk), F32)


def _gdn(pa, pba, conv_w, a_log, dt_bias, norm_g, batch, seq):
    ts = min(MIX_TS, seq)
    nst = seq // ts
    bd = _block_diag_ones(GROUP_WIDTH, HEAD_DIM).astype(BF16)
    r = np.arange(ts)
    lt = jnp.asarray((r[:, None] // CHUNK == r[None, :] // CHUNK) & (r[:, None] >= r[None, :]), BF16)
    lane = np.arange(GROUP_WIDTH)
    e = np.zeros((LANE, 2 * GROUP_WIDTH), np.float32)
    for h in range(N_HEADS):
        e[h, np.nonzero(lane // HEAD_DIM == h)[0]] = 1.0
        e[N_HEADS + h, GROUP_WIDTH + np.nonzero(lane // HEAD_DIM == h)[0]] = 1.0
    rep = lambda p: jnp.repeat(p.astype(F32), HEAD_DIM)[None, :]
    consts = [conv_w.astype(F32), rep(a_log), rep(dt_bias),
              jnp.tile(norm_g.astype(F32), N_HEADS)[None, :], bd, lt, jnp.asarray(e, BF16)]
    nb = GDN_NB if batch % GDN_NB == 0 else 1
    full = lambda a: pl.BlockSpec(a.shape, lambda b, s: (0, 0))
    out = pl.pallas_call(
        functools.partial(_gdn_kernel, ts=ts, nb=nb),
        name="gdn",
        grid=(batch // nb, nst),
        in_specs=[pl.BlockSpec((nb, ts, A_W), lambda b, s: (b, s, 0)),
                  pl.BlockSpec((nb, ts, LANE), lambda b, s: (b, s, 0))] + [full(a) for a in consts],
        out_specs=pl.BlockSpec((nb, ts, GROUP_WIDTH), lambda b, s: (b, s, 0)),
        out_shape=jax.ShapeDtypeStruct((batch, seq, GROUP_WIDTH), BF16),
        scratch_shapes=[pltpu.VMEM((nb, SUBLANE, 768), F32),
                        pltpu.VMEM((nb, GROUP_WIDTH, GROUP_WIDTH), F32)],
        compiler_params=_cparams(("parallel", "arbitrary")),
    )(pa.reshape(batch, seq, A_W), pba.reshape(batch, seq, LANE), *consts)
    return out.reshape(batch * seq, GROUP_WIDTH)


def _attn_kernel(qa_ref, qb_ref, k_ref, v_ref, bias_ref, lamv_ref, gn_ref, oa_ref, ob_ref,
                 va_ref, qt_ref, s_ref, p_ref, al_ref, mt_ref, m_ref, l_ref, acc_ref, *, lam_init, seq):
    t = ATT_T
    nq = seq // t
    i = pl.program_id(1)
    na = i + 1
    nlan = 2 * N_HEADS * t
    nsteps = nq + 1

    @pl.when(i == 0)
    def _stage_v():
        def body(j, carry):
            rows = pl.ds(pl.multiple_of(j * t, t), t)
            vt = v_ref[rows, :].astype(F32).T.astype(BF16)
            for h in range(N_HEADS):
                va_ref[j, h, 0:HEAD_DIM, :] = vt[h * HEAD_DIM:(h + 1) * HEAD_DIM, :]
                va_ref[j, h, HEAD_DIM:ATT_VROWS, :] = jnp.ones((ATT_VROWS - HEAD_DIM, t), BF16)
            return carry
        lax.fori_loop(0, seq // t, body, 0)

    feat = lax.broadcasted_iota(jnp.int32, (GROUP_WIDTH, t), 0) // DIFF_DH
    for w, q_ref in enumerate((qa_ref, qb_ref)):
        qt = (q_ref[...].astype(F32) * (DIFF_DH ** -0.5 * LOG2E)).T
        for idx in range(2 * N_HEADS):
            qt_ref[w, :, idx * t:(idx + 1) * t] = jnp.where(feat == idx, qt, 0.0).astype(BF16)
    m_ref[...] = jnp.full((2, 1, nlan), NEG, F32)
    l_ref[...] = jnp.zeros((2, 1, nlan), F32)
    acc_ref[...] = jnp.zeros((2, HEAD_DIM, nlan), F32)

    head_cols = [slice(2 * h * t, (2 * h + 2) * t) for h in range(N_HEADS)]

    def tile_of(s):
        if s >= nq // 2:
            bias = 2 if s == nq else (1 if s == nq - 1 else None)
            return 1, s - na, bias
        w = (s >= na).astype(jnp.int32)
        bias = jnp.where(w == 1, 0, jnp.where(s == na - 1, 2, jnp.where(s == na - 2, 1, 0)))
        return w, s - na * w, bias

    def scores(s, h):
        w, kt, bias = tile_of(s)
        rows = pl.ds(pl.multiple_of(kt * t, t), t)
        sc = _dot(k_ref[rows, :], qt_ref[w, :, head_cols[h]])
        if bias is not None:
            sc = sc + bias_ref[bias, :, head_cols[h]]
        s_ref[s % 2, :, head_cols[h]] = sc
        mt_ref[s % 2, :, head_cols[h]] = jnp.max(sc, axis=0, keepdims=True)

    def softmax(s, h):
        w, _, _ = tile_of(s)
        cols = head_cols[h]
        sc = s_ref[s % 2, :, cols]
        m_prev = m_ref[w, :, cols]
        m_new = jnp.maximum(m_prev, mt_ref[s % 2, :, cols])
        al_ref[s % 2, :, cols] = jnp.exp2(m_prev - m_new)
        m_ref[w, :, cols] = m_new
        p_ref[s % 2, :, cols] = jnp.exp2(sc - m_new).astype(BF16)

    def values(s, h):
        w, kt, _ = tile_of(s)
        cols = head_cols[h]
        alpha = al_ref[s % 2, :, cols]
        pv = _dot(va_ref[kt, h], p_ref[s % 2, :, cols])
        acc_ref[w, :, cols] = alpha * acc_ref[w, :, cols] + pv[0:HEAD_DIM]
        l_ref[w, :, cols] = alpha * l_ref[w, :, cols] + pv[HEAD_DIM:HEAD_DIM + 1]

    for s in range(nsteps + 2):
        for h in range(N_HEADS):
            if s < nsteps:
                scores(s, h)
            if 1 <= s <= nsteps:
                softmax(s - 1, h)
            if s >= 2:
                values(s - 2, h)

    lv = lamv_ref[...]
    lam = (jnp.exp(jnp.sum(lv[0:1] * lv[1:2], axis=1, keepdims=True))
           - jnp.exp(jnp.sum(lv[2:3] * lv[3:4], axis=1, keepdims=True)) + lam_init)
    for w, o_ref in enumerate((oa_ref, ob_ref)):
        inv_l = 1.0 / l_ref[w]
        outs = []
        for h in range(N_HEADS):
            c0 = slice(2 * h * t, (2 * h + 1) * t)
            c1 = slice((2 * h + 1) * t, (2 * h + 2) * t)
            oh = acc_ref[w, :, c0] * inv_l[:, c0] - lam * (acc_ref[w, :, c1] * inv_l[:, c1])
            oh = oh * lax.rsqrt(jnp.mean(oh * oh, axis=0, keepdims=True) + EPS)
            outs.append(oh)
        o = jnp.concatenate(outs, axis=0).T
        o_ref[...] = (o * gn_ref[...] * (1.0 - lam_init)).astype(o_ref.dtype)


def _t5_bucket(rel):
    nb = REL_BUCKETS // 2
    bucket = jnp.where(rel > 0, nb, 0)
    n = jnp.abs(rel)
    max_exact = nb // 2
    large = max_exact + (jnp.log(jnp.maximum(n, 1).astype(F32) / max_exact)
                         / math.log(REL_MAX_DIST / max_exact) * (nb - max_exact)).astype(jnp.int32)
    large = jnp.minimum(large, nb - 1)
    return bucket + jnp.where(n < max_exact, n, large)


def _attn_bias_tiles(rel_bias):
    t = ATT_T
    table = rel_bias.astype(F32)
    kk = jnp.arange(t)[:, None]
    qq = jnp.arange(t)[None, :]

    def expand(b):
        b = jnp.transpose(b, (0, 2, 1))
        b = jnp.broadcast_to(b[:, :, None, :], (t, N_HEADS, 2, t))
        return b.reshape(t, 2 * N_HEADS * t)

    table = table * LOG2E

    def lookup(bucket):
        onehot = (bucket[..., None] == jnp.arange(REL_BUCKETS)).astype(F32)
        return jnp.einsum('kqb,bh->kqh', onehot, table, precision=HI)

    diag = lookup(_t5_bucket(kk - qq))
    diag = jnp.where(((kk // CHUNK) <= (qq // CHUNK))[:, :, None], diag, NEG)
    near = lookup(_t5_bucket(kk - qq - t))
    far = lookup(_t5_bucket(jnp.full((1, 1), -(REL_MAX_DIST + 1), jnp.int32)))
    far = jnp.broadcast_to(far, (t, t, N_HEADS))
    return jnp.stack([jnp.zeros((t, 2 * N_HEADS * t), F32), expand(near - far), expand(diag - far)])


def _diff_attn(pb, lam_vecs, lam_init, bias_tiles, norm_g, batch, seq):
    t = ATT_T
    nq = seq // t
    gn = jnp.tile(norm_g.astype(F32), N_HEADS)[None, :]
    lamv = lam_vecs.astype(F32)
    nlan = 2 * N_HEADS * t
    assert nq % 2 == 0
    nh = nq // 2
    full = lambda a: pl.BlockSpec(a.shape, lambda b, i: (0,) * a.ndim)
    half = jax.ShapeDtypeStruct((batch * nh * t, GROUP_WIDTH), BF16)
    lo, hi = pl.pallas_call(
        functools.partial(_attn_kernel, lam_init=lam_init, seq=seq),
        name="diffattn",
        grid=(batch, nh),
        in_specs=[pl.BlockSpec((t, GROUP_WIDTH), lambda b, i: (b * nq + i, 0)),
                  pl.BlockSpec((t, GROUP_WIDTH), lambda b, i: (b * nq + nq - 1 - i, 0)),
                  pl.BlockSpec((seq, GROUP_WIDTH), lambda b, i: (b, 1)),
                  pl.BlockSpec((seq, GROUP_WIDTH), lambda b, i: (b, 2)),
                  full(bias_tiles), full(lamv), full(gn)],
        out_specs=[pl.BlockSpec((t, GROUP_WIDTH), lambda b, i: (b * nh + i, 0)),
                   pl.BlockSpec((t, GROUP_WIDTH), lambda b, i: (b * nh + nh - 1 - i, 0))],
        out_shape=[half, half],
        scratch_shapes=[pltpu.VMEM((seq // t, N_HEADS, ATT_VROWS, t), BF16),
                        pltpu.VMEM((2, GROUP_WIDTH, nlan), BF16),
                        pltpu.VMEM((2, t, nlan), F32),
                        pltpu.VMEM((2, t, nlan), BF16),
                        pltpu.VMEM((2, 1, nlan), F32),
                        pltpu.VMEM((2, 1, nlan), F32),
                        pltpu.VMEM((2, 1, nlan), F32), pltpu.VMEM((2, 1, nlan), F32),
                        pltpu.VMEM((2, HEAD_DIM, nlan), F32)],
        compiler_params=_cparams(("parallel", "arbitrary")),
    )(pb, pb, pb, pb, bias_tiles, lamv, gn)
    out = jnp.concatenate([lo.reshape(batch, nh * t, GROUP_WIDTH), hi.reshape(batch, nh * t, GROUP_WIDTH)],
                          axis=1)
    return out.reshape(batch * seq, GROUP_WIDTH)


def _rglru_kernel(c_ref, convw_ref, convb_ref, wa_ref, ba_ref, wx_ref, bx_ref, ap_ref, o_ref,
                  tail_ref, h_ref, *, ts, nb):
    @pl.when(pl.program_id(1) == 0)
    def _init():
        tail_ref[...] = jnp.zeros_like(tail_ref)
        h_ref[...] = jnp.zeros_like(h_ref)

    row = lax.broadcasted_iota(jnp.int32, (ts, GROUP_WIDTH), 0)
    for b in range(nb):
        xb = c_ref[b, :, 0:256].astype(F32)
        gb = c_ref[b, :, 256:512].astype(F32)
        xc = _causal_conv(xb, tail_ref[b], convw_ref[...]) + convb_ref[...]
        tail_ref[b] = xb[ts - SUBLANE:ts, :]
        xcb = xc.astype(BF16)
        gate_a = jax.nn.sigmoid(_dot(xcb, wa_ref[...]) + ba_ref[...])
        gate_x = jax.nn.sigmoid(_dot(xcb, wx_ref[...]) + bx_ref[...])
        log_a = -RG_C * gate_a * _softplus(ap_ref[...])
        a = jnp.exp(log_a)
        th = jnp.tanh(log_a)
        u = xc * gate_x * jnp.sqrt(-2.0 * th / (1.0 - th))
        d = 1
        while d < ts:
            keep = row >= d
            a_sh = jnp.where(keep, pltpu.roll(a, d, 0), 1.0)
            u_sh = jnp.where(keep, pltpu.roll(u, d, 0), 0.0)
            u = u + a * u_sh
            a = a * a_sh
            d *= 2
        h = u + a * h_ref[b]
        h_ref[b] = h[ts - 1:ts, :]
        gelu = 0.5 * gb * (1.0 + jnp.tanh(math.sqrt(2.0 / math.pi) * (gb + 0.044715 * (gb * gb * gb))))
        o_ref[b] = (h * gelu).astype(o_ref.dtype)


def _block_diag_weight(w):
    nb, wi, wo = w.shape
    out = jnp.zeros((nb * wi, nb * wo), w.dtype)
    for i in range(nb):
        out = out.at[i * wi:(i + 1) * wi, i * wo:(i + 1) * wo].set(w[i])
    return out


def _rglru(pc, conv_w, conv_b, w_a, b_a, w_x, b_x, a_param, batch, seq):
    ts = min(MIX_TS, seq)
    nst = seq // ts
    r = lambda p: p.astype(F32)[None, :]
    consts = [conv_w.astype(F32), r(conv_b), _block_diag_weight(w_a).astype(BF16), r(b_a),
              _block_diag_weight(w_x).astype(BF16), r(b_x), r(a_param)]
    nb = RGLRU_NB if batch % RGLRU_NB == 0 else 1
    full = lambda a: pl.BlockSpec(a.shape, lambda b, s: (0, 0))
    out = pl.pallas_call(
        functools.partial(_rglru_kernel, ts=ts, nb=nb),
        name="rglru",
        grid=(batch // nb, nst),
        in_specs=[pl.BlockSpec((nb, ts, C_W), lambda b, s: (b, s, 0))] + [full(a) for a in consts],
        out_specs=pl.BlockSpec((nb, ts, GROUP_WIDTH), lambda b, s: (b, s, 0)),
        out_shape=jax.ShapeDtypeStruct((batch, seq, GROUP_WIDTH), BF16),
        scratch_shapes=[pltpu.VMEM((nb, SUBLANE, GROUP_WIDTH), F32), pltpu.VMEM((nb, 1, GROUP_WIDTH), F32)],
        compiler_params=_cparams(("parallel", "arbitrary")),
    )(pc.reshape(batch, seq, C_W), *consts)
    return out.reshape(batch * seq, GROUP_WIDTH)


def _gla_kernel(d_ref, lr_ref, wlr_ref, blr_ref, gn_ref, bd_ref, bdt_ref, lt_ref, o_ref, state_ref,
                *, ts, nb):
    @pl.when(pl.program_id(1) == 0)
    def _init():
        state_ref[...] = jnp.zeros_like(state_ref)

    c = CHUNK
    bd = bd_ref[...]
    bdt = bdt_ref[...]
    lt = lt_ref[...]
    kmasks = _head_mask(N_HEADS * GLA_DK, GLA_DK)
    ri = lax.broadcasted_iota(jnp.int32, (c, GROUP_WIDTH), 0)
    cj = lax.broadcasted_iota(jnp.int32, (c, GROUP_WIDTH), 1) % HEAD_DIM
    causal = ri >= cj
    q, k, gcum = [], [], []
    for b in range(nb):
        q.append(d_ref[b, :, 0:128].astype(F32) * (GLA_DK ** -0.5))
        k.append(d_ref[b, :, 128:256].astype(F32))
        lr_hi, lr_lo = _split2(lr_ref[b])
        w_hi, w_lo = _split2(wlr_ref[...])
        z = _dot(lr_hi, w_hi) + (_dot(lr_hi, w_lo) + _dot(lr_lo, w_hi)) + blr_ref[...]
        la = (jnp.minimum(z, 0.0) - jnp.log1p(jnp.exp(-jnp.abs(z)))) * (1.0 / GLA_TAU)
        la_hi, la_lo = _split2(la)
        la_lo2 = (la - la_hi.astype(F32) - la_lo.astype(F32)).astype(BF16)
        gcum.append(_dot(lt, la_hi) + (_dot(lt, la_lo) + _dot(lt, la_lo2)))

    items = [(b, slice(ci * c, (ci + 1) * c)) for ci in range(ts // c) for b in range(nb)]
    o_intra, upd = [], []
    for b, sl in items:
        qc, kc, gc = q[b][sl], k[b][sl], gcum[b][sl]
        vc = d_ref[b, sl, 256:512]
        ref = gc[c // 2:c // 2 + 1, :]
        kst = _stack_heads(kc * jnp.exp(ref - gc), kmasks).astype(BF16)
        a_in = _dot_nt((qc * jnp.exp(gc - ref)).astype(BF16), kst)
        a_in = jnp.where(causal, a_in, 0.0).astype(BF16)
        o_intra.append(_dot(a_in, _bd_tile(vc, bd)))
        kd = kc * jnp.exp(gc[c - 1:c, :] - gc)
        upd.append(_dot_tn(vc, kd.astype(BF16)) * bdt)

    state = [state_ref[b] for b in range(nb)]
    outs = [[] for _ in range(nb)]
    for i, (b, sl) in enumerate(items):
        gc = gcum[b][sl]
        o_inter = _dot_nt((q[b][sl] * jnp.exp(gc)).astype(BF16), state[b].astype(BF16))
        outs[b].append(o_intra[i] + o_inter)
        state[b] = state[b] * jnp.exp(gc[c - 1:c, :]) + upd[i]
    for b in range(nb):
        state_ref[b] = state[b]
        o = jnp.concatenate(outs[b], axis=0)
        o = o * lax.rsqrt(_dot_x2(o * o, bd) * (1.0 / HEAD_DIM) + EPS) * gn_ref[...]
        rt = d_ref[b, :, 512:768].astype(F32)
        o_ref[b] = (o * (rt * jax.nn.sigmoid(rt))).astype(o_ref.dtype)


def _gla(pd, plr, w_lr, b_lr, norm_g, batch, seq):
    ts = min(MIX_TS, seq)
    nst = seq // ts
    kw = N_HEADS * GLA_DK
    wlr = jnp.zeros((LANE, kw), F32).at[SIDE_LR:SIDE_LR + GLA_RANK, :].set(w_lr.astype(F32))
    bd = _block_diag_ones(GROUP_WIDTH, HEAD_DIM).astype(BF16)
    iv = np.arange(GROUP_WIDTH)[:, None] // HEAD_DIM
    ik = np.arange(kw)[None, :] // GLA_DK
    bdt = jnp.asarray(iv == ik, F32)
    r = np.arange(ts)
    lt = jnp.asarray((r[:, None] // CHUNK == r[None, :] // CHUNK) & (r[:, None] >= r[None, :]), BF16)
    consts = [wlr, b_lr.astype(F32)[None, :], jnp.tile(norm_g.astype(F32), N_HEADS)[None, :], bd, bdt, lt]
    nb = GDN_NB if batch % GDN_NB == 0 else 1
    full = lambda a: pl.BlockSpec(a.shape, lambda b, s: (0, 0))
    out = pl.pallas_call(
        functools.partial(_gla_kernel, ts=ts, nb=nb),
        name="gla",
        grid=(batch // nb, nst),
        in_specs=[pl.BlockSpec((nb, ts, D_W), lambda b, s: (b, s, 0)),
                  pl.BlockSpec((nb, ts, LANE), lambda b, s: (b, s, 0))] + [full(a) for a in consts],
        out_specs=pl.BlockSpec((nb, ts, GROUP_WIDTH), lambda b, s: (b, s, 0)),
        out_shape=jax.ShapeDtypeStruct((batch, seq, GROUP_WIDTH), BF16),
        scratch_shapes=[pltpu.VMEM((nb, GROUP_WIDTH, kw), F32)],
        compiler_params=_cparams(("parallel", "arbitrary")),
    )(pd.reshape(batch, seq, D_W), plr.reshape(batch, seq, LANE), *consts)
    return out.reshape(batch * seq, GROUP_WIDTH)


def _mix_outproj(x_ref, ma_ref, mb_ref, mc_ref, md_ref, wo_ref):
    mix = jnp.concatenate([ma_ref[...], mb_ref[...], mc_ref[...], md_ref[...]], axis=1)
    return x_ref[...] + _dot(mix, wo_ref[...])


def _ffn_kernel(x_ref, ma_ref, mb_ref, mc_ref, md_ref, wo_ref, g_ref, wg_ref, wu_ref, wd_ref,
                o_ref, h_ref, acc_ref):
    f = pl.program_id(1)

    @pl.when(f == 0)
    def _first():
        x1 = _mix_outproj(x_ref, ma_ref, mb_ref, mc_ref, md_ref, wo_ref)
        acc_ref[...] = x1
        h_ref[...] = _rms(x1, g_ref[...]).astype(BF16)

    h = h_ref[...]
    gt = _dot(h, wg_ref[...])
    act = (gt * jax.nn.sigmoid(gt) * _dot(h, wu_ref[...])).astype(BF16)
    acc_ref[...] += _dot(act, wd_ref[...])

    @pl.when(f == pl.num_programs(1) - 1)
    def _last():
        o_ref[...] = acc_ref[...]


def _outproj_ffn(x, mixes, w_out, g, w_gate, w_up, w_down):
    t = x.shape[0]
    tm = min(FFN_TM, t)
    tf = FFN_TF
    nf = D_FF // tf
    row = lambda w: pl.BlockSpec((tm, w), lambda i, f: (i, 0))
    return pl.pallas_call(
        _ffn_kernel,
        name="outproj_ffn",
        grid=(t // tm, nf),
        in_specs=[row(D_MODEL)] + [row(GROUP_WIDTH)] * 4 + [
            pl.BlockSpec((D_MODEL, D_MODEL), lambda i, f: (0, 0)),
            pl.BlockSpec((1, D_MODEL), lambda i, f: (0, 0)),
            pl.BlockSpec((D_MODEL, tf), lambda i, f: (0, f)),
            pl.BlockSpec((D_MODEL, tf), lambda i, f: (0, f)),
            pl.BlockSpec((tf, D_MODEL), lambda i, f: (f, 0))],
        out_specs=row(D_MODEL),
        out_shape=jax.ShapeDtypeStruct((t, D_MODEL), F32),
        scratch_shapes=[pltpu.VMEM((tm, D_MODEL), BF16), pltpu.VMEM((tm, D_MODEL), F32)],
        compiler_params=_cparams(("parallel", "arbitrary")),
    )(x, *mixes, w_out, g, w_gate, w_up, w_down)


def _router_kernel(x_ref, ma_ref, mb_ref, mc_ref, md_ref, wo_ref, g_ref, wr_ref, x1_ref, h_ref, r_ref):
    x1 = _mix_outproj(x_ref, ma_ref, mb_ref, mc_ref, md_ref, wo_ref)
    x1_ref[...] = x1
    h = _rms(x1, g_ref[...])
    h_hi, h_lo = _split2(h)
    h_ref[...] = h_hi
    lane = lax.broadcasted_iota(jnp.int32, (x1.shape[0], LANE), 1)
    hh = _dot(h_hi, wr_ref[...])
    logits = hh[:, 0:LANE] + (hh[:, LANE:2 * LANE] + _dot(h_lo, wr_ref[:, 0:LANE]))
    logits = jnp.where(lane < N_EXPERTS, logits, NEG)
    m1 = jnp.max(logits, axis=1, keepdims=True)
    e1 = jnp.min(jnp.where(logits == m1, lane, LANE), axis=1, keepdims=True)
    rest = jnp.where(lane == e1, NEG, logits)
    m2 = jnp.max(rest, axis=1, keepdims=True)
    e2 = jnp.min(jnp.where(rest == m2, lane, LANE), axis=1, keepdims=True)
    ex = jnp.exp(m2 - m1)
    w1 = 1.0 / (1.0 + ex)
    w2 = ex / (1.0 + ex)
    r_ref[...] = jnp.where(lane == 0, e1.astype(F32),
                           jnp.where(lane == 1, e2.astype(F32),
                                     jnp.where(lane == 2, w1, jnp.where(lane == 3, w2, 0.0))))


def _outproj_router(x, mixes, w_out, g, w_router):
    t = x.shape[0]
    tm = min(ROUTER_TM, t)
    wr = jnp.pad(w_router.astype(F32), ((0, 0), (0, LANE - N_EXPERTS)))
    wr_hi = wr.astype(BF16)
    wr_cat = jnp.concatenate([wr_hi, (wr - wr_hi.astype(F32)).astype(BF16)], axis=1)
    row = lambda w: pl.BlockSpec((tm, w), lambda i: (i, 0))
    full = lambda a: pl.BlockSpec(a.shape, lambda i: (0, 0))
    return pl.pallas_call(
        _router_kernel,
        name="outproj_router",
        grid=(t // tm,),
        in_specs=[row(D_MODEL)] + [row(GROUP_WIDTH)] * 4 + [full(w_out), full(g), full(wr_cat)],
        out_specs=[row(D_MODEL), row(D_MODEL), row(LANE)],
        out_shape=[jax.ShapeDtypeStruct((t, D_MODEL), F32), jax.ShapeDtypeStruct((t, D_MODEL), BF16),
                   jax.ShapeDtypeStruct((t, LANE), F32)],
        compiler_params=_cparams(("parallel",)),
    )(x, *mixes, w_out, g, wr_cat)


def _moe_kernel(te_ref, na_ref, x_ref, wg_ref, wu_ref, wd_ref, o_ref, acc_ref):
    i = pl.program_id(0)
    f = pl.program_id(1)

    @pl.when(i < na_ref[0])
    def _active():
        @pl.when(f == 0)
        def _zero():
            acc_ref[...] = jnp.zeros_like(acc_ref)

        x = x_ref[...]
        gt = _dot(x, wg_ref[...].astype(BF16))
        act = (gt * jax.nn.sigmoid(gt) * _dot(x, wu_ref[...].astype(BF16))).astype(BF16)
        acc_ref[...] += _dot(act, wd_ref[...].astype(BF16))

        @pl.when(f == pl.num_programs(1) - 1)
        def _last():
            o_ref[...] = acc_ref[...].astype(o_ref.dtype)

    @pl.when(jnp.logical_and(i >= na_ref[0], f == pl.num_programs(1) - 1))
    def _unused_tile():
        o_ref[...] = jnp.zeros_like(o_ref)


def _moe_experts(xs, tile_e, n_active, w_gate, w_up, w_down, n_tiles):
    tm, tf = MOE_TM, MOE_TF
    nf = D_FF_EXPERT // tf

    def tile(i, na):
        return jnp.minimum(i, na[0] - 1)

    def fidx(i, f, na):
        return jnp.where(i < na[0], f, nf - 1)

    grid_spec = pltpu.PrefetchScalarGridSpec(
        num_scalar_prefetch=2,
        grid=(n_tiles, nf),
        in_specs=[
            pl.BlockSpec((tm, D_MODEL), lambda i, f, te, na: (tile(i, na), 0)),
            pl.BlockSpec((None, D_MODEL, tf), lambda i, f, te, na: (te[tile(i, na)], 0, fidx(i, f, na))),
            pl.BlockSpec((None, D_MODEL, tf), lambda i, f, te, na: (te[tile(i, na)], 0, fidx(i, f, na))),
            pl.BlockSpec((None, tf, D_MODEL), lambda i, f, te, na: (te[tile(i, na)], fidx(i, f, na), 0)),
        ],
        out_specs=pl.BlockSpec((tm, D_MODEL), lambda i, f, te, na: (i, 0)),
        scratch_shapes=[pltpu.VMEM((tm, D_MODEL), F32)],
    )
    return pl.pallas_call(
        _moe_kernel,
        name="moe_experts",
        grid_spec=grid_spec,
        out_shape=jax.ShapeDtypeStruct((n_tiles * tm, D_MODEL), BF16),
        compiler_params=_cparams(("arbitrary", "arbitrary")),
    )(tile_e, n_active, xs, w_gate, w_up, w_down)


def _route(route, t):
    tm = MOE_TM
    n_assign = t * TOP_K
    flat_e = jnp.concatenate([route[:, k].astype(jnp.int32) for k in range(TOP_K)])
    onehot = (flat_e[:, None] == jnp.arange(N_EXPERTS)[None, :]).astype(jnp.int32)
    counts = jnp.sum(onehot, axis=0)
    padded = (counts + tm - 1) // tm * tm
    pad_end = jnp.cumsum(padded)
    pad_start = pad_end - padded
    grp_start = jnp.cumsum(counts) - counts
    dest = jnp.sum(onehot * (jnp.cumsum(onehot, axis=0) - 1 + pad_start[None, :]), axis=1)
    n_tiles = -(-n_assign // tm) + N_EXPERTS
    tile_e = jnp.minimum(jnp.searchsorted(pad_end, jnp.arange(n_tiles) * tm, side='right'),
                         N_EXPERTS - 1).astype(jnp.int32)
    n_active = (pad_end[-1] // tm).astype(jnp.int32).reshape(1)
    order = jnp.argsort(flat_e, stable=True).astype(jnp.int32)
    rank_s = ((jnp.arange(n_tiles, dtype=jnp.int32) * tm - pad_start[tile_e])[:, None]
              + jnp.arange(tm, dtype=jnp.int32)[None, :])
    src = jnp.minimum(grp_start[tile_e][:, None] + rank_s, n_assign - 1)
    slot = jnp.arange(n_tiles * tm, dtype=jnp.int32).reshape(n_tiles, tm)
    slot_tok = jnp.where(rank_s < counts[tile_e][:, None], order[src] % t, slot % t).reshape(n_tiles * tm)
    return slot_tok, tile_e, n_active, [dest[k * t:(k + 1) * t] for k in range(TOP_K)], n_tiles


def _combine_kernel(x_ref, y0_ref, y1_ref, r_ref, g_ref, o_ref, *, final):
    r = r_ref[...]
    y = x_ref[...] + (y0_ref[...].astype(F32) * r[:, TOP_K:TOP_K + 1]
                      + y1_ref[...].astype(F32) * r[:, TOP_K + 1:TOP_K + 2])
    o_ref[...] = _rms(y, g_ref[...]) if final else y


def _moe_combine(x1, y0, y1, route, g, final):
    t = x1.shape[0]
    tm = min(ROUTER_TM, t)
    row = pl.BlockSpec((tm, D_MODEL), lambda i: (i, 0))
    return pl.pallas_call(
        functools.partial(_combine_kernel, final=final),
        name="moe_combine",
        grid=(t // tm,),
        in_specs=[row, row, row, pl.BlockSpec((tm, LANE), lambda i: (i, 0)),
                  pl.BlockSpec((1, D_MODEL), lambda i: (0, 0))],
        out_specs=row,
        out_shape=jax.ShapeDtypeStruct((t, D_MODEL), F32),
        compiler_params=_cparams(("parallel",)),
    )(x1, y0, y1, route, g)


def _final_norm_kernel(x_ref, g_ref, o_ref):
    o_ref[...] = _rms(x_ref[...], g_ref[...])


def _split_w_in(w):
    def pad(a, width):
        return jnp.pad(a, ((0, 0), (0, width - a.shape[1])))
    a_end = A_W + 2 * N_HEADS
    b_end = a_end + B_W
    c_end = b_end + C_W
    d_end = c_end + D_W
    side = pad(jnp.concatenate([w[:, A_W:a_end], w[:, d_end:]], axis=1), LANE)
    parts = [(w[:, 0:A_W], BF16), (w[:, a_end:b_end], BF16), (w[:, b_end:c_end], BF16),
             (w[:, c_end:d_end], BF16), (side, F32)]
    return [(m.astype(BF16), dt) for m, dt in parts]


def kernel(x, norm_mix, w_in, a_conv, a_A_log, a_dt_bias, a_norm, b_lambda, b_norm, rel_bias,
           c_conv_w, c_conv_b, c_w_a, c_b_a, c_w_x, c_b_x, c_a_param, d_w_lr, d_b_lr, d_norm,
           w_out, norm_ffn, ffn_w_gate, ffn_w_up, ffn_w_down, moe_router, moe_w_gate, moe_w_up,
           moe_w_down, norm_final):
    batch, seq, _ = x.shape
    depth = w_in.shape[0]
    t = batch * seq
    xt = x.reshape(t, D_MODEL).astype(F32)
    bias_tiles = _attn_bias_tiles(rel_bias)
    row = lambda p: p.astype(F32)[None, :]
    out = None
    for l in range(depth):
        pa, pb, pc, pd, pside = _inproj(xt, row(norm_mix[l]), _split_w_in(w_in[l]))
        pba = plr = pside
        lam_init = 0.8 - 0.6 * math.exp(-0.3 * l)
        mixes = (
            _gdn(pa, pba, a_conv[l], a_A_log[l], a_dt_bias[l], a_norm[l], batch, seq),
            _diff_attn(pb, b_lambda[l], lam_init, bias_tiles, b_norm[l], batch, seq),
            _rglru(pc, c_conv_w[l], c_conv_b[l], c_w_a[l], c_b_a[l], c_w_x[l], c_b_x[l],
                   c_a_param[l], batch, seq),
            _gla(pd, plr, d_w_lr[l], d_b_lr[l], d_norm[l], batch, seq),
        )
        wo = w_out[l].astype(BF16)
        if l % 2 == 0:
            j = l // 2
            xt = _outproj_ffn(xt, mixes, wo, row(norm_ffn[l]), ffn_w_gate[j].astype(BF16),
                              ffn_w_up[j].astype(BF16), ffn_w_down[j].astype(BF16))
            out = None
        else:
            j = l // 2
            final = l == depth - 1
            take = lambda a, i: a.at[i].get(mode='promise_in_bounds')
            x1, h, route = _outproj_router(xt, mixes, wo, row(norm_ffn[l]), moe_router[j])
            slot_tok, tile_e, n_active, dest, n_tiles = _route(route, t)
            xs = take(h, slot_tok)
            ys = _moe_experts(xs, tile_e, n_active, moe_w_gate[j], moe_w_up[j], moe_w_down[j], n_tiles)
            y = _moe_combine(x1, take(ys, dest[0]), take(ys, dest[1]), route, row(norm_final), final)
            if final:
                out = y
            else:
                xt = y
    if out is None:
        tm = min(FFN_TM, t)
        rowspec = pl.BlockSpec((tm, D_MODEL), lambda i: (i, 0))
        out = pl.pallas_call(
            _final_norm_kernel, name="final_norm", grid=(t // tm,),
            in_specs=[rowspec, pl.BlockSpec((1, D_MODEL), lambda i: (0, 0))],
            out_specs=rowspec, out_shape=jax.ShapeDtypeStruct((t, D_MODEL), F32),
            compiler_params=_cparams(("parallel",)),
        )(xt, row(norm_final))
    return out.reshape(batch, seq, D_MODEL).astype(x.dtype)
```

```python
import functools
import math

import jax
import jax.numpy as jnp
import numpy as np
from jax import lax
from jax.experimental import pallas as pl
from jax.experimental.pallas import tpu as pltpu

D_MODEL = 1024
CHUNK = 64
N_HEADS = 4
HEAD_DIM = 64
GROUP_WIDTH = 256
CONV_WIDTH = 4
DIFF_DH = 32
Q_BLOCK = 128
REL_BUCKETS = 32
REL_MAX_DIST = 128
RG_C = 8.0
GLA_DK = 32
GLA_RANK = 16
GLA_TAU = 16.0
D_FF = 2816
N_EXPERTS = 8
TOP_K = 2
D_FF_EXPERT = 3584
EPS = 1e-6
assert CHUNK == HEAD_DIM

LANE = 128
SUBLANE = 8
VMEM_LIMIT = 56 * 1024 * 1024

F32 = jnp.float32
BF16 = jnp.bfloat16
HI = lax.Precision.HIGHEST
NEG = -1e30
LOG2E = math.log2(math.e)

A_W = 1024
B_W = 768
C_W = 512
D_W = 768
SIDE_LR = 2 * N_HEADS

PROJ_TM = 1024
MIX_TS = 256
GDN_BLK = 16
RGLRU_NB = 4
GDN_NB = 4
ATT_T = 128
ATT_VROWS = 80
FFN_TM = 1024
ROUTER_TM = 1024
FFN_TF = 1408
MOE_TM = 1024
MOE_TF = 512


def _cparams(sem):
    return pltpu.CompilerParams(dimension_semantics=sem, vmem_limit_bytes=VMEM_LIMIT)


def _dot(a, b, precision=None):
    return jnp.dot(a, b, preferred_element_type=F32, precision=precision)


def _dot_nt(a, b, precision=None):
    return lax.dot_general(a, b, (((1,), (1,)), ((), ())), preferred_element_type=F32,
                           precision=precision)


def _dot_tn(a, b, precision=None):
    return lax.dot_general(a, b, (((0,), (0,)), ((), ())), preferred_element_type=F32,
                           precision=precision)


def _softplus(x):
    return jnp.maximum(x, 0.0) + jnp.log1p(jnp.exp(-jnp.abs(x)))


def _rms(x, g):
    return x * lax.rsqrt(jnp.mean(x * x, axis=-1, keepdims=True) + EPS) * g


def _causal_conv(x, tail, w):
    row = lax.broadcasted_iota(jnp.int32, (SUBLANE, x.shape[1]), 0)
    y = x * w[CONV_WIDTH - 1:CONV_WIDTH, :]
    for d in range(1, CONV_WIDTH):
        rolled = pltpu.roll(x, d, 0)
        first = jnp.where(row < d, pltpu.roll(tail, d, 0), rolled[:SUBLANE])
        shifted = jnp.concatenate([first, rolled[SUBLANE:]], axis=0)
        y = y + shifted * w[CONV_WIDTH - 1 - d:CONV_WIDTH - d, :]
    return y


def _head_mask(width, per_head):
    lane = lax.broadcasted_iota(jnp.int32, (1, width), 1)
    return [(lane // per_head) == h for h in range(N_HEADS)]


def _stack_heads(x, masks):
    return jnp.concatenate([jnp.where(m, x, 0.0) for m in masks], axis=0)


def _unstack_heads(r, masks, c):
    out = jnp.where(masks[0], r[0:c], 0.0)
    for h in range(1, N_HEADS):
        out = out + jnp.where(masks[h], r[h * c:(h + 1) * c], 0.0)
    return out


def _inproj_kernel(x_ref, g_ref, *refs):
    n = len(refs) // 2
    h = _rms(x_ref[...], g_ref[...]).astype(BF16)
    for w_ref, o_ref in zip(refs[:n], refs[n:]):
        o_ref[...] = _dot(h, w_ref[...]).astype(o_ref.dtype)


def _inproj(x, g, weights):
    t = x.shape[0]
    tm = min(PROJ_TM, t)
    row = lambda w: pl.BlockSpec((tm, w), lambda i: (i, 0))
    full = lambda a: pl.BlockSpec(a.shape, lambda i: (0, 0))
    ws = [w for w, _ in weights]
    return pl.pallas_call(
        _inproj_kernel,
        name="inproj",
        grid=(t // tm,),
        in_specs=[row(D_MODEL), full(g)] + [full(w) for w in ws],
        out_specs=[row(w.shape[1]) for w in ws],
        out_shape=[jax.ShapeDtypeStruct((t, w.shape[1]), dt) for w, dt in weights],
        compiler_params=_cparams(("parallel",)),
    )(x, g, *ws)


def _split2(x):
    hi = x.astype(BF16)
    return hi, (x - hi.astype(F32)).astype(BF16)


def _dot_x2(x, w):
    hi, lo = _split2(x)
    return _dot(hi, w) + _dot(lo, w)


def _bd_tile(x, bd):
    return jnp.concatenate([x] * N_HEADS, axis=0) * bd


def _dot_bd(a, bs, bd):
    ab = a.astype(BF16)
    return [_dot(ab, _bd_tile(b.astype(BF16), bd)) for b in bs]


def _gdn_kernel(a_ref, ba_ref, convw_ref, alog_ref, dtb_ref, gn_ref, bd_ref, lt_ref, e_ref,
                o_ref, tail_ref, state_ref, *, ts, nb):
    @pl.when(pl.program_id(1) == 0)
    def _init():
        tail_ref[...] = jnp.zeros_like(tail_ref)
        state_ref[...] = jnp.zeros_like(state_ref)

    c = CHUNK
    bd = bd_ref[...]
    lt = lt_ref[...]
    q, k, v, beta, gcum = [], [], [], [], []
    for b in range(nb):
        xin = a_ref[b, :, 0:768].astype(F32)
        y = _causal_conv(xin, tail_ref[b], convw_ref[...])
        tail_ref[b] = xin[ts - SUBLANE:ts, :]
        y = y * jax.nn.sigmoid(y)
        qb, kb_ = y[:, 0:256], y[:, 256:512]
        q.append(qb * lax.rsqrt(_dot_x2(qb * qb, bd) + EPS) * (HEAD_DIM ** -0.5))
        k.append(kb_ * lax.rsqrt(_dot_x2(kb_ * kb_, bd) + EPS))
        v.append(y[:, 512:768])
        e = _dot_x2(ba_ref[b], e_ref[...])
        beta.append(jax.nn.sigmoid(e[:, 0:256]))
        g = -jnp.exp(alog_ref[...]) * _softplus(e[:, 256:512] + dtb_ref[...])
        g_hi, g_lo = _split2(g)
        g_lo2 = (g - g_hi.astype(F32) - g_lo.astype(F32)).astype(BF16)
        gcum.append(_dot(lt, g_hi) + (_dot(lt, g_lo) + _dot(lt, g_lo2)))

    ri = lax.broadcasted_iota(jnp.int32, (c, GROUP_WIDTH), 0)
    cj = lax.broadcasted_iota(jnp.int32, (c, GROUP_WIDTH), 1) % HEAD_DIM
    causal = ri >= cj
    strict = ri > cj
    diag = ri == cj
    same_blk = (ri // GDN_BLK) == (cj // GDN_BLK)
    eye = diag.astype(F32)
    masks = _head_mask(GROUP_WIDTH, HEAD_DIM)
    items = [(b, slice(ci * c, (ci + 1) * c)) for ci in range(ts // c) for b in range(nb)]
    idx = range(len(items))

    a_qk, p0, nn, rhs_u, rhs_w = [], [], [], [], []
    for b, sl in items:
        kc, gc = k[b][sl], gcum[b][sl]
        kb = kc * beta[b][sl]
        grow = jnp.sum(jnp.where(diag, gc, 0.0), axis=0, keepdims=True)
        gamma = jnp.exp(jnp.where(causal, gc - grow, NEG))
        kst = _stack_heads(kc, masks).astype(BF16)
        aa = _dot_nt(jnp.concatenate([kb, q[b][sl]], axis=0).astype(BF16), kst)
        a_kk = jnp.where(strict, aa[0:c] * gamma, 0.0)
        a_qk.append(aa[c:2 * c] * gamma)
        p0.append(jnp.where(same_blk, -a_kk, 0.0))
        nn.append(jnp.where(same_blk, 0.0, a_kk))
        rhs_u.append(v[b][sl] * beta[b][sl])
        rhs_w.append(kb * jnp.exp(gc))
    t1 = [eye + p for p in p0]
    p1 = [_dot_bd(p0[i], [p0[i]], bd)[0] for i in idx]
    pr = [_dot_bd(p1[i], [p1[i], t1[i]], bd) for i in idx]
    p2 = [x[0] for x in pr]
    t2 = [t1[i] + pr[i][1] for i in idx]
    pr = [_dot_bd(p2[i], [p2[i], t2[i]], bd) for i in idx]
    p3 = [x[0] for x in pr]
    t3 = [t2[i] + pr[i][1] for i in idx]
    dinv = [t3[i] + _dot_bd(p3[i], [t3[i]], bd)[0] for i in idx]
    m1 = [_dot_bd(dinv[i], [nn[i]], bd)[0] for i in idx]
    m2 = [_dot_bd(m1[i], [m1[i]], bd)[0] for i in idx]
    im = [eye - m for m in m1]
    qq = [im[i] + _dot_bd(im[i], [m2[i]], bd)[0] for i in idx]
    inv = [_dot_bd(qq[i], [dinv[i]], bd)[0] for i in idx]
    uw = [_dot_bd(inv[i], [rhs_u[i], rhs_w[i]], bd) for i in idx]

    bdf = bd.astype(F32)
    state = [state_ref[b] for b in range(nb)]
    outs = [[] for _ in range(nb)]
    for i, (b, sl) in enumerate(items):
        gc = gcum[b][sl]
        g_last = gc[c - 1:c, :]
        u, w = uw[i]
        ws_qs = _dot(jnp.concatenate([w, q[b][sl] * jnp.exp(gc)], axis=0).astype(BF16), state[b].astype(BF16))
        v_new = (u - ws_qs[0:c]).astype(BF16)
        outs[b].append(ws_qs[c:2 * c] + _dot(a_qk[i].astype(BF16), _bd_tile(v_new, bd)))
        kd = k[b][sl] * jnp.exp(g_last - gc)
        state[b] = state[b] * jnp.exp(g_last) + _dot_tn(kd.astype(BF16), v_new) * bdf
    for b in range(nb):
        state_ref[b] = state[b]
        o = jnp.concatenate(outs[b], axis=0)
        o = o * lax.rsqrt(_dot_x2(o * o, bd) * (1.0 / HEAD_DIM) + EPS) * gn_ref[...]
        gate = a_ref[b, :, 768:1024].astype(F32)
        o_ref[b] = (o * (gate * jax.nn.sigmoid(gate))).astype(o_ref.dtype)


def _block_diag_ones(n, blk):
    i = np.arange(n)
    return jnp.asarray((i[:, None] // blk) == (i[None, :] // blk), F32)


def _gdn(pa, pba, conv_w, a_log, dt_bias, norm_g, batch, seq):
    ts = min(MIX_TS, seq)
    nst = seq // ts
    bd = _block_diag_ones(GROUP_WIDTH, HEAD_DIM).astype(BF16)
    r = np.arange(ts)
    lt = jnp.asarray((r[:, None] // CHUNK == r[None, :] // CHUNK) & (r[:, None] >= r[None, :]), BF16)
    lane = np.arange(GROUP_WIDTH)
    e = np.zeros((LANE, 2 * GROUP_WIDTH), np.float32)
    for h in range(N_HEADS):
        e[h, np.nonzero(lane // HEAD_DIM == h)[0]] = 1.0
        e[N_HEADS + h, GROUP_WIDTH + np.nonzero(lane // HEAD_DIM == h)[0]] = 1.0
    rep = lambda p: jnp.repeat(p.astype(F32), HEAD_DIM)[None, :]
    consts = [conv_w.astype(F32), rep(a_log), rep(dt_bias),
              jnp.tile(norm_g.astype(F32), N_HEADS)[None, :], bd, lt, jnp.asarray(e, BF16)]
    nb = GDN_NB if batch % GDN_NB == 0 else 1
    full = lambda a: pl.BlockSpec(a.shape, lambda b, s: (0, 0))
    out = pl.pallas_call(
        functools.partial(_gdn_kernel, ts=ts, nb=nb),
        name="gdn",
        grid=(batch // nb, nst),
        in_specs=[pl.BlockSpec((nb, ts, A_W), lambda b, s: (b, s, 0)),
                  pl.BlockSpec((nb, ts, LANE), lambda b, s: (b, s, 0))] + [full(a) for a in consts],
        out_specs=pl.BlockSpec((nb, ts, GROUP_WIDTH), lambda b, s: (b, s, 0)),
        out_shape=jax.ShapeDtypeStruct((batch, seq, GROUP_WIDTH), BF16),
        scratch_shapes=[pltpu.VMEM((nb, SUBLANE, 768), F32),
                        pltpu.VMEM((nb, GROUP_WIDTH, GROUP_WIDTH), F32)],
        compiler_params=_cparams(("parallel", "arbitrary")),
    )(pa.reshape(batch, seq, A_W), pba.reshape(batch, seq, LANE), *consts)
    return out.reshape(batch * seq, GROUP_WIDTH)


def _attn_kernel(qa_ref, qb_ref, k_ref, v_ref, bias_ref, lamv_ref, gn_ref, oa_ref, ob_ref,
                 va_ref, qt_ref, s_ref, p_ref, al_ref, mt_ref, m_ref, l_ref, acc_ref, *, lam_init, seq):
    t = ATT_T
    nq = seq // t
    i = pl.program_id(1)
    na = i + 1
    nlan = 2 * N_HEADS * t
    nsteps = nq + 1

    @pl.when(i == 0)
    def _stage_v():
        def body(j, carry):
            rows = pl.ds(pl.multiple_of(j * t, t), t)
            vt = v_ref[rows, :].astype(F32).T.astype(BF16)
            for h in range(N_HEADS):
                va_ref[j, h, 0:HEAD_DIM, :] = vt[h * HEAD_DIM:(h + 1) * HEAD_DIM, :]
                va_ref[j, h, HEAD_DIM:ATT_VROWS, :] = jnp.ones((ATT_VROWS - HEAD_DIM, t), BF16)
            return carry
        lax.fori_loop(0, seq // t, body, 0)

    feat = lax.broadcasted_iota(jnp.int32, (GROUP_WIDTH, t), 0) // DIFF_DH
    for w, q_ref in enumerate((qa_ref, qb_ref)):
        qt = (q_ref[...].astype(F32) * (DIFF_DH ** -0.5 * LOG2E)).T
        for idx in range(2 * N_HEADS):
            qt_ref[w, :, idx * t:(idx + 1) * t] = jnp.where(feat == idx, qt, 0.0).astype(BF16)
    m_ref[...] = jnp.full((2, 1, nlan), NEG, F32)
    l_ref[...] = jnp.zeros((2, 1, nlan), F32)
    acc_ref[...] = jnp.zeros((2, HEAD_DIM, nlan), F32)

    head_cols = [slice(2 * h * t, (2 * h + 2) * t) for h in range(N_HEADS)]

    def tile_of(s):
        if s >= nq // 2:
            bias = 2 if s == nq else (1 if s == nq - 1 else None)
            return 1, s - na, bias
        w = (s >= na).astype(jnp.int32)
        bias = jnp.where(w == 1, 0, jnp.where(s == na - 1, 2, jnp.where(s == na - 2, 1, 0)))
        return w, s - na * w, bias

    def scores(s, h):
        w, kt, bias = tile_of(s)
        rows = pl.ds(pl.multiple_of(kt * t, t), t)
        sc = _dot(k_ref[rows, :], qt_ref[w, :, head_cols[h]])
        if bias is not None:
            sc = sc + bias_ref[bias, :, head_cols[h]]
        s_ref[s % 2, :, head_cols[h]] = sc
        mt_ref[s % 2, :, head_cols[h]] = jnp.max(sc, axis=0, keepdims=True)

    def softmax(s, h):
        w, _, _ = tile_of(s)
        cols = head_cols[h]
        sc = s_ref[s % 2, :, cols]
        m_prev = m_ref[w, :, cols]
        m_new = jnp.maximum(m_prev, mt_ref[s % 2, :, cols])
        al_ref[s % 2, :, cols] = jnp.exp2(m_prev - m_new)
        m_ref[w, :, cols] = m_new
        p_ref[s % 2, :, cols] = jnp.exp2(sc - m_new).astype(BF16)

    def values(s, h):
        w, kt, _ = tile_of(s)
        cols = head_cols[h]
        alpha = al_ref[s % 2, :, cols]
        pv = _dot(va_ref[kt, h], p_ref[s % 2, :, cols])
        acc_ref[w, :, cols] = alpha * acc_ref[w, :, cols] + pv[0:HEAD_DIM]
        l_ref[w, :, cols] = alpha * l_ref[w, :, cols] + pv[HEAD_DIM:HEAD_DIM + 1]

    for s in range(nsteps + 2):
        for h in range(N_HEADS):
            if s < nsteps:
                scores(s, h)
            if 1 <= s <= nsteps:
                softmax(s - 1, h)
            if s >= 2:
                values(s - 2, h)

    lv = lamv_ref[...]
    lam = (jnp.exp(jnp.sum(lv[0:1] * lv[1:2], axis=1, keepdims=True))
           - jnp.exp(jnp.sum(lv[2:3] * lv[3:4], axis=1, keepdims=True)) + lam_init)
    for w, o_ref in enumerate((oa_ref, ob_ref)):
        inv_l = 1.0 / l_ref[w]
        outs = []
        for h in range(N_HEADS):
            c0 = slice(2 * h * t, (2 * h + 1) * t)
            c1 = slice((2 * h + 1) * t, (2 * h + 2) * t)
            oh = acc_ref[w, :, c0] * inv_l[:, c0] - lam * (acc_ref[w, :, c1] * inv_l[:, c1])
            oh = oh * lax.rsqrt(jnp.mean(oh * oh, axis=0, keepdims=True) + EPS)
            outs.append(oh)
        o = jnp.concatenate(outs, axis=0).T
        o_ref[...] = (o * gn_ref[...] * (1.0 - lam_init)).astype(o_ref.dtype)


def _t5_bucket(rel):
    nb = REL_BUCKETS // 2
    bucket = jnp.where(rel > 0, nb, 0)
    n = jnp.abs(rel)
    max_exact = nb // 2
    large = max_exact + (jnp.log(jnp.maximum(n, 1).astype(F32) / max_exact)
                         / math.log(REL_MAX_DIST / max_exact) * (nb - max_exact)).astype(jnp.int32)
    large = jnp.minimum(large, nb - 1)
    return bucket + jnp.where(n < max_exact, n, large)


def _attn_bias_tiles(rel_bias):
    t = ATT_T
    table = rel_bias.astype(F32)
    kk = jnp.arange(t)[:, None]
    qq = jnp.arange(t)[None, :]

    def expand(b):
        b = jnp.transpose(b, (0, 2, 1))
        b = jnp.broadcast_to(b[:, :, None, :], (t, N_HEADS, 2, t))
        return b.reshape(t, 2 * N_HEADS * t)

    table = table * LOG2E

    def lookup(bucket):
        onehot = (bucket[..., None] == jnp.arange(REL_BUCKETS)).astype(F32)
        return jnp.einsum('kqb,bh->kqh', onehot, table, precision=HI)

    diag = lookup(_t5_bucket(kk - qq))
    diag = jnp.where(((kk // CHUNK) <= (qq // CHUNK))[:, :, None], diag, NEG)
    near = lookup(_t5_bucket(kk - qq - t))
    far = lookup(_t5_bucket(jnp.full((1, 1), -(REL_MAX_DIST + 1), jnp.int32)))
    far = jnp.broadcast_to(far, (t, t, N_HEADS))
    return jnp.stack([jnp.zeros((t, 2 * N_HEADS * t), F32), expand(near - far), expand(diag - far)])


def _diff_attn(pb, lam_vecs, lam_init, bias_tiles, norm_g, batch, seq):
    t = ATT_T
    nq = seq // t
    gn = jnp.tile(norm_g.astype(F32), N_HEADS)[None, :]
    lamv = lam_vecs.astype(F32)
    nlan = 2 * N_HEADS * t
    assert nq % 2 == 0
    nh = nq // 2
    full = lambda a: pl.BlockSpec(a.shape, lambda b, i: (0,) * a.ndim)
    half = jax.ShapeDtypeStruct((batch * nh * t, GROUP_WIDTH), BF16)
    lo, hi = pl.pallas_call(
        functools.partial(_attn_kernel, lam_init=lam_init, seq=seq),
        name="diffattn",
        grid=(batch, nh),
        in_specs=[pl.BlockSpec((t, GROUP_WIDTH), lambda b, i: (b * nq + i, 0)),
                  pl.BlockSpec((t, GROUP_WIDTH), lambda b, i: (b * nq + nq - 1 - i, 0)),
                  pl.BlockSpec((seq, GROUP_WIDTH), lambda b, i: (b, 1)),
                  pl.BlockSpec((seq, GROUP_WIDTH), lambda b, i: (b, 2)),
                  full(bias_tiles), full(lamv), full(gn)],
        out_specs=[pl.BlockSpec((t, GROUP_WIDTH), lambda b, i: (b * nh + i, 0)),
                   pl.BlockSpec((t, GROUP_WIDTH), lambda b, i: (b * nh + nh - 1 - i, 0))],
        out_shape=[half, half],
        scratch_shapes=[pltpu.VMEM((seq // t, N_HEADS, ATT_VROWS, t), BF16),
                        pltpu.VMEM((2, GROUP_WIDTH, nlan), BF16),
                        pltpu.VMEM((2, t, nlan), F32),
                        pltpu.VMEM((2, t, nlan), BF16),
                        pltpu.VMEM((2, 1, nlan), F32),
                        pltpu.VMEM((2, 1, nlan), F32),
                        pltpu.VMEM((2, 1, nlan), F32), pltpu.VMEM((2, 1, nlan), F32),
                        pltpu.VMEM((2, HEAD_DIM, nlan), F32)],
        compiler_params=_cparams(("parallel", "arbitrary")),
    )(pb, pb, pb, pb, bias_tiles, lamv, gn)
    out = jnp.concatenate([lo.reshape(batch, nh * t, GROUP_WIDTH), hi.reshape(batch, nh * t, GROUP_WIDTH)],
                          axis=1)
    return out.reshape(batch * seq, GROUP_WIDTH)


def _rglru_kernel(c_ref, convw_ref, convb_ref, wa_ref, ba_ref, wx_ref, bx_ref, ap_ref, o_ref,
                  tail_ref, h_ref, *, ts, nb):
    @pl.when(pl.program_id(1) == 0)
    def _init():
        tail_ref[...] = jnp.zeros_like(tail_ref)
        h_ref[...] = jnp.zeros_like(h_ref)

    row = lax.broadcasted_iota(jnp.int32, (ts, GROUP_WIDTH), 0)
    for b in range(nb):
        xb = c_ref[b, :, 0:256].astype(F32)
        gb = c_ref[b, :, 256:512].astype(F32)
        xc = _causal_conv(xb, tail_ref[b], convw_ref[...]) + convb_ref[...]
        tail_ref[b] = xb[ts - SUBLANE:ts, :]
        xcb = xc.astype(BF16)
        gate_a = jax.nn.sigmoid(_dot(xcb, wa_ref[...]) + ba_ref[...])
        gate_x = jax.nn.sigmoid(_dot(xcb, wx_ref[...]) + bx_ref[...])
        log_a = -RG_C * gate_a * _softplus(ap_ref[...])
        a = jnp.exp(log_a)
        th = jnp.tanh(log_a)
        u = xc * gate_x * jnp.sqrt(-2.0 * th / (1.0 - th))
        d = 1
        while d < ts:
            keep = row >= d
            a_sh = jnp.where(keep, pltpu.roll(a, d, 0), 1.0)
            u_sh = jnp.where(keep, pltpu.roll(u, d, 0), 0.0)
            u = u + a * u_sh
            a = a * a_sh
            d *= 2
        h = u + a * h_ref[b]
        h_ref[b] = h[ts - 1:ts, :]
        gelu = 0.5 * gb * (1.0 + jnp.tanh(math.sqrt(2.0 / math.pi) * (gb + 0.044715 * (gb * gb * gb))))
        o_ref[b] = (h * gelu).astype(o_ref.dtype)


def _block_diag_weight(w):
    nb, wi, wo = w.shape
    out = jnp.zeros((nb * wi, nb * wo), w.dtype)
    for i in range(nb):
        out = out.at[i * wi:(i + 1) * wi, i * wo:(i + 1) * wo].set(w[i])
    return out


def _rglru(pc, conv_w, conv_b, w_a, b_a, w_x, b_x, a_param, batch, seq):
    ts = min(MIX_TS, seq)
    nst = seq // ts
    r = lambda p: p.astype(F32)[None, :]
    consts = [conv_w.astype(F32), r(conv_b), _block_diag_weight(w_a).astype(BF16), r(b_a),
              _block_diag_weight(w_x).astype(BF16), r(b_x), r(a_param)]
    nb = RGLRU_NB if batch % RGLRU_NB == 0 else 1
    full = lambda a: pl.BlockSpec(a.shape, lambda b, s: (0, 0))
    out = pl.pallas_call(
        functools.partial(_rglru_kernel, ts=ts, nb=nb),
        name="rglru",
        grid=(batch // nb, nst),
        in_specs=[pl.BlockSpec((nb, ts, C_W), lambda b, s: (b, s, 0))] + [full(a) for a in consts],
        out_specs=pl.BlockSpec((nb, ts, GROUP_WIDTH), lambda b, s: (b, s, 0)),
        out_shape=jax.ShapeDtypeStruct((batch, seq, GROUP_WIDTH), BF16),
        scratch_shapes=[pltpu.VMEM((nb, SUBLANE, GROUP_WIDTH), F32), pltpu.VMEM((nb, 1, GROUP_WIDTH), F32)],
        compiler_params=_cparams(("parallel", "arbitrary")),
    )(pc.reshape(batch, seq, C_W), *consts)
    return out.reshape(batch * seq, GROUP_WIDTH)


def _gla_kernel(d_ref, lr_ref, wlr_ref, blr_ref, gn_ref, bd_ref, bdt_ref, lt_ref, o_ref, state_ref,
                *, ts, nb):
    @pl.when(pl.program_id(1) == 0)
    def _init():
        state_ref[...] = jnp.zeros_like(state_ref)

    c = CHUNK
    bd = bd_ref[...]
    bdt = bdt_ref[...]
    lt = lt_ref[...]
    kmasks = _head_mask(N_HEADS * GLA_DK, GLA_DK)
    ri = lax.broadcasted_iota(jnp.int32, (c, GROUP_WIDTH), 0)
    cj = lax.broadcasted_iota(jnp.int32, (c, GROUP_WIDTH), 1) % HEAD_DIM
    causal = ri >= cj
    q, k, gcum = [], [], []
    for b in range(nb):
        q.append(d_ref[b, :, 0:128].astype(F32) * (GLA_DK ** -0.5))
        k.append(d_ref[b, :, 128:256].astype(F32))
        lr_hi, lr_lo = _split2(lr_ref[b])
        w_hi, w_lo = _split2(wlr_ref[...])
        z = _dot(lr_hi, w_hi) + (_dot(lr_hi, w_lo) + _dot(lr_lo, w_hi)) + blr_ref[...]
        la = (jnp.minimum(z, 0.0) - jnp.log1p(jnp.exp(-jnp.abs(z)))) * (1.0 / GLA_TAU)
        la_hi, la_lo = _split2(la)
        la_lo2 = (la - la_hi.astype(F32) - la_lo.astype(F32)).astype(BF16)
        gcum.append(_dot(lt, la_hi) + (_dot(lt, la_lo) + _dot(lt, la_lo2)))

    items = [(b, slice(ci * c, (ci + 1) * c)) for ci in range(ts // c) for b in range(nb)]
    o_intra, upd = [], []
    for b, sl in items:
        qc, kc, gc = q[b][sl], k[b][sl], gcum[b][sl]
        vc = d_ref[b, sl, 256:512]
        ref = gc[c // 2:c // 2 + 1, :]
        kst = _stack_heads(kc * jnp.exp(ref - gc), kmasks).astype(BF16)
        a_in = _dot_nt((qc * jnp.exp(gc - ref)).astype(BF16), kst)
        a_in = jnp.where(causal, a_in, 0.0).astype(BF16)
        o_intra.append(_dot(a_in, _bd_tile(vc, bd)))
        kd = kc * jnp.exp(gc[c - 1:c, :] - gc)
        upd.append(_dot_tn(vc, kd.astype(BF16)) * bdt)

    state = [state_ref[b] for b in range(nb)]
    outs = [[] for _ in range(nb)]
    for i, (b, sl) in enumerate(items):
        gc = gcum[b][sl]
        o_inter = _dot_nt((q[b][sl] * jnp.exp(gc)).astype(BF16), state[b].astype(BF16))
        outs[b].append(o_intra[i] + o_inter)
        state[b] = state[b] * jnp.exp(gc[c - 1:c, :]) + upd[i]
    for b in range(nb):
        state_ref[b] = state[b]
        o = jnp.concatenate(outs[b], axis=0)
        o = o * lax.rsqrt(_dot_x2(o * o, bd) * (1.0 / HEAD_DIM) + EPS) * gn_ref[...]
        rt = d_ref[b, :, 512:768].astype(F32)
        o_ref[b] = (o * (rt * jax.nn.sigmoid(rt))).astype(o_ref.dtype)


def _gla(pd, plr, w_lr, b_lr, norm_g, batch, seq):
    ts = min(MIX_TS, seq)
    nst = seq // ts
    kw = N_HEADS * GLA_DK
    wlr = jnp.zeros((LANE, kw), F32).at[SIDE_LR:SIDE_LR + GLA_RANK, :].set(w_lr.astype(F32))
    bd = _block_diag_ones(GROUP_WIDTH, HEAD_DIM).astype(BF16)
    iv = np.arange(GROUP_WIDTH)[:, None] // HEAD_DIM
    ik = np.arange(kw)[None, :] // GLA_DK
    bdt = jnp.asarray(iv == ik, F32)
    r = np.arange(ts)
    lt = jnp.asarray((r[:, None] // CHUNK == r[None, :] // CHUNK) & (r[:, None] >= r[None, :]), BF16)
    consts = [wlr, b_lr.astype(F32)[None, :], jnp.tile(norm_g.astype(F32), N_HEADS)[None, :], bd, bdt, lt]
    nb = GDN_NB if batch % GDN_NB == 0 else 1
    full = lambda a: pl.BlockSpec(a.shape, lambda b, s: (0, 0))
    out = pl.pallas_call(
        functools.partial(_gla_kernel, ts=ts, nb=nb),
        name="gla",
        grid=(batch // nb, nst),
        in_specs=[pl.BlockSpec((nb, ts, D_W), lambda b, s: (b, s, 0)),
                  pl.BlockSpec((nb, ts, LANE), lambda b, s: (b, s, 0))] + [full(a) for a in consts],
        out_specs=pl.BlockSpec((nb, ts, GROUP_WIDTH), lambda b, s: (b, s, 0)),
        out_shape=jax.ShapeDtypeStruct((batch, seq, GROUP_WIDTH), BF16),
        scratch_shapes=[pltpu.VMEM((nb, GROUP_WIDTH, kw), F32)],
        compiler_params=_cparams(("parallel", "arbitrary")),
    )(pd.reshape(batch, seq, D_W), plr.reshape(batch, seq, LANE), *consts)
    return out.reshape(batch * seq, GROUP_WIDTH)


def _mix_outproj(x_ref, ma_ref, mb_ref, mc_ref, md_ref, wo_ref):
    mix = jnp.concatenate([ma_ref[...], mb_ref[...], mc_ref[...], md_ref[...]], axis=1)
    return x_ref[...] + _dot(mix, wo_ref[...])


def _ffn_kernel(x_ref, ma_ref, mb_ref, mc_ref, md_ref, wo_ref, g_ref, wg_ref, wu_ref, wd_ref,
                o_ref, h_ref):
    @pl.when(pl.program_id(1) == 0)
    def _first():
        x1 = _mix_outproj(x_ref, ma_ref, mb_ref, mc_ref, md_ref, wo_ref)
        o_ref[...] = x1
        h_ref[...] = _rms(x1, g_ref[...]).astype(BF16)

    h = h_ref[...]
    gt = _dot(h, wg_ref[...])
    act = (gt * jax.nn.sigmoid(gt) * _dot(h, wu_ref[...])).astype(BF16)
    o_ref[...] += _dot(act, wd_ref[...])


def _outproj_ffn(x, mixes, w_out, g, w_gate, w_up, w_down):
    t = x.shape[0]
    tm = min(FFN_TM, t)
    tf = FFN_TF
    nf = D_FF // tf
    row = lambda w: pl.BlockSpec((tm, w), lambda i, f: (i, 0))
    return pl.pallas_call(
        _ffn_kernel,
        name="outproj_ffn",
        grid=(t // tm, nf),
        in_specs=[row(D_MODEL)] + [row(GROUP_WIDTH)] * 4 + [
            pl.BlockSpec((D_MODEL, D_MODEL), lambda i, f: (0, 0)),
            pl.BlockSpec((1, D_MODEL), lambda i, f: (0, 0)),
            pl.BlockSpec((D_MODEL, tf), lambda i, f: (0, f)),
            pl.BlockSpec((D_MODEL, tf), lambda i, f: (0, f)),
            pl.BlockSpec((tf, D_MODEL), lambda i, f: (f, 0))],
        out_specs=row(D_MODEL),
        out_shape=jax.ShapeDtypeStruct((t, D_MODEL), F32),
        scratch_shapes=[pltpu.VMEM((tm, D_MODEL), BF16)],
        compiler_params=_cparams(("parallel", "arbitrary")),
    )(x, *mixes, w_out, g, w_gate, w_up, w_down)


def _router_kernel(x_ref, ma_ref, mb_ref, mc_ref, md_ref, wo_ref, g_ref, wr_ref, x1_ref, h_ref, r_ref):
    x1 = _mix_outproj(x_ref, ma_ref, mb_ref, mc_ref, md_ref, wo_ref)
    x1_ref[...] = x1
    h = _rms(x1, g_ref[...])
    h_hi, h_lo = _split2(h)
    h_ref[...] = h_hi
    lane = lax.broadcasted_iota(jnp.int32, (x1.shape[0], LANE), 1)
    hh = _dot(h_hi, wr_ref[...])
    logits = hh[:, 0:LANE] + (hh[:, LANE:2 * LANE] + _dot(h_lo, wr_ref[:, 0:LANE]))
    logits = jnp.where(lane < N_EXPERTS, logits, NEG)
    m1 = jnp.max(logits, axis=1, keepdims=True)
    e1 = jnp.min(jnp.where(logits == m1, lane, LANE), axis=1, keepdims=True)
    rest = jnp.where(lane == e1, NEG, logits)
    m2 = jnp.max(rest, axis=1, keepdims=True)
    e2 = jnp.min(jnp.where(rest == m2, lane, LANE), axis=1, keepdims=True)
    ex = jnp.exp(m2 - m1)
    w1 = 1.0 / (1.0 + ex)
    w2 = ex / (1.0 + ex)
    r_ref[...] = jnp.where(lane == 0, e1.astype(F32),
                           jnp.where(lane == 1, e2.astype(F32),
                                     jnp.where(lane == 2, w1, jnp.where(lane == 3, w2, 0.0))))


def _outproj_router(x, mixes, w_out, g, w_router):
    t = x.shape[0]
    tm = min(ROUTER_TM, t)
    wr = jnp.pad(w_router.astype(F32), ((0, 0), (0, LANE - N_EXPERTS)))
    wr_hi = wr.astype(BF16)
    wr_cat = jnp.concatenate([wr_hi, (wr - wr_hi.astype(F32)).astype(BF16)], axis=1)
    row = lambda w: pl.BlockSpec((tm, w), lambda i: (i, 0))
    full = lambda a: pl.BlockSpec(a.shape, lambda i: (0, 0))
    return pl.pallas_call(
        _router_kernel,
        name="outproj_router",
        grid=(t // tm,),
        in_specs=[row(D_MODEL)] + [row(GROUP_WIDTH)] * 4 + [full(w_out), full(g), full(wr_cat)],
        out_specs=[row(D_MODEL), row(D_MODEL), row(LANE)],
        out_shape=[jax.ShapeDtypeStruct((t, D_MODEL), F32), jax.ShapeDtypeStruct((t, D_MODEL), BF16),
                   jax.ShapeDtypeStruct((t, LANE), F32)],
        compiler_params=_cparams(("parallel",)),
    )(x, *mixes, w_out, g, wr_cat)


def _moe_kernel(te_ref, na_ref, x_ref, wg_ref, wu_ref, wd_ref, o_ref, acc_ref):
    i = pl.program_id(0)
    f = pl.program_id(1)

    def ff_step(first):
        x = x_ref[...]
        gt = _dot(x, wg_ref[...].astype(BF16))
        act = (gt * jax.nn.sigmoid(gt) * _dot(x, wu_ref[...].astype(BF16))).astype(BF16)
        part = _dot(act, wd_ref[...].astype(BF16))
        if first:
            acc_ref[...] = part
        else:
            acc_ref[...] += part

    @pl.when(i < na_ref[0])
    def _active():
        @pl.when(f == 0)
        def _first():
            ff_step(True)

        @pl.when(f > 0)
        def _rest():
            ff_step(False)

        @pl.when(f == pl.num_programs(1) - 1)
        def _last():
            o_ref[...] = acc_ref[...].astype(o_ref.dtype)

    @pl.when(jnp.logical_and(i >= na_ref[0], f == pl.num_programs(1) - 1))
    def _unused_tile():
        o_ref[...] = jnp.zeros_like(o_ref)


def _moe_experts(xs, tile_e, n_active, w_gate, w_up, w_down, n_tiles):
    tm, tf = MOE_TM, MOE_TF
    nf = D_FF_EXPERT // tf

    def tile(i, na):
        return jnp.minimum(i, na[0] - 1)

    def fidx(i, f, na):
        return jnp.where(i < na[0], f, nf - 1)

    grid_spec = pltpu.PrefetchScalarGridSpec(
        num_scalar_prefetch=2,
        grid=(n_tiles, nf),
        in_specs=[
            pl.BlockSpec((tm, D_MODEL), lambda i, f, te, na: (tile(i, na), 0)),
            pl.BlockSpec((None, D_MODEL, tf), lambda i, f, te, na: (te[tile(i, na)], 0, fidx(i, f, na))),
            pl.BlockSpec((None, D_MODEL, tf), lambda i, f, te, na: (te[tile(i, na)], 0, fidx(i, f, na))),
            pl.BlockSpec((None, tf, D_MODEL), lambda i, f, te, na: (te[tile(i, na)], fidx(i, f, na), 0)),
        ],
        out_specs=pl.BlockSpec((tm, D_MODEL), lambda i, f, te, na: (i, 0)),
        scratch_shapes=[pltpu.VMEM((tm, D_MODEL), F32)],
    )
    return pl.pallas_call(
        _moe_kernel,
        name="moe_experts",
        grid_spec=grid_spec,
        out_shape=jax.ShapeDtypeStruct((n_tiles * tm, D_MODEL), BF16),
        compiler_params=_cparams(("arbitrary", "arbitrary")),
    )(tile_e, n_active, xs, w_gate, w_up, w_down)


def _route(route, t):
    tm = MOE_TM
    n_assign = t * TOP_K
    flat_e = jnp.concatenate([route[:, k].astype(jnp.int32) for k in range(TOP_K)])
    onehot = (flat_e[:, None] == jnp.arange(N_EXPERTS)[None, :]).astype(jnp.int32)
    counts = jnp.sum(onehot, axis=0)
    padded = (counts + tm - 1) // tm * tm
    pad_end = jnp.cumsum(padded)
    pad_start = pad_end - padded
    grp_start = jnp.cumsum(counts) - counts
    dest = jnp.sum(onehot * (jnp.cumsum(onehot, axis=0) - 1 + pad_start[None, :]), axis=1)
    n_tiles = -(-n_assign // tm) + N_EXPERTS
    tile_e = jnp.minimum(jnp.searchsorted(pad_end, jnp.arange(n_tiles) * tm, side='right'),
                         N_EXPERTS - 1).astype(jnp.int32)
    n_active = (pad_end[-1] // tm).astype(jnp.int32).reshape(1)
    order = jnp.argsort(flat_e, stable=True).astype(jnp.int32)
    rank_s = ((jnp.arange(n_tiles, dtype=jnp.int32) * tm - pad_start[tile_e])[:, None]
              + jnp.arange(tm, dtype=jnp.int32)[None, :])
    src = jnp.minimum(grp_start[tile_e][:, None] + rank_s, n_assign - 1)
    slot = jnp.arange(n_tiles * tm, dtype=jnp.int32).reshape(n_tiles, tm)
    slot_tok = jnp.where(rank_s < counts[tile_e][:, None], order[src] % t, slot % t).reshape(n_tiles * tm)
    return slot_tok, tile_e, n_active, [dest[k * t:(k + 1) * t] for k in range(TOP_K)], n_tiles


def _combine_kernel(x_ref, y0_ref, y1_ref, r_ref, g_ref, o_ref, *, final):
    r = r_ref[...]
    y = x_ref[...] + (y0_ref[...].astype(F32) * r[:, TOP_K:TOP_K + 1]
                      + y1_ref[...].astype(F32) * r[:, TOP_K + 1:TOP_K + 2])
    o_ref[...] = _rms(y, g_ref[...]) if final else y


def _moe_combine(x1, y0, y1, route, g, final):
    t = x1.shape[0]
    tm = min(ROUTER_TM, t)
    row = pl.BlockSpec((tm, D_MODEL), lambda i: (i, 0))
    return pl.pallas_call(
        functools.partial(_combine_kernel, final=final),
        name="moe_combine",
        grid=(t // tm,),
        in_specs=[row, row, row, pl.BlockSpec((tm, LANE), lambda i: (i, 0)),
                  pl.BlockSpec((1, D_MODEL), lambda i: (0, 0))],
        out_specs=row,
        out_shape=jax.ShapeDtypeStruct((t, D_MODEL), F32),
        compiler_params=_cparams(("parallel",)),
    )(x1, y0, y1, route, g)


def _final_norm_kernel(x_ref, g_ref, o_ref):
    o_ref[...] = _rms(x_ref[...], g_ref[...])


def _split_w_in(w):
    def pad(a, width):
        return jnp.pad(a, ((0, 0), (0, width - a.shape[1])))
    a_end = A_W + 2 * N_HEADS
    b_end = a_end + B_W
    c_end = b_end + C_W
    d_end = c_end + D_W
    side = pad(jnp.concatenate([w[:, A_W:a_end], w[:, d_end:]], axis=1), LANE)
    parts = [(w[:, 0:A_W], BF16), (w[:, a_end:b_end], BF16), (w[:, b_end:c_end], BF16),
             (w[:, c_end:d_end], BF16), (side, F32)]
    return [(m.astype(BF16), dt) for m, dt in parts]


def kernel(x, norm_mix, w_in, a_conv, a_A_log, a_dt_bias, a_norm, b_lambda, b_norm, rel_bias,
           c_conv_w, c_conv_b, c_w_a, c_b_a, c_w_x, c_b_x, c_a_param, d_w_lr, d_b_lr, d_norm,
           w_out, norm_ffn, ffn_w_gate, ffn_w_up, ffn_w_down, moe_router, moe_w_gate, moe_w_up,
           moe_w_down, norm_final):
    batch, seq, _ = x.shape
    depth = w_in.shape[0]
    t = batch * seq
    xt = x.reshape(t, D_MODEL).astype(F32)
    bias_tiles = _attn_bias_tiles(rel_bias)
    row = lambda p: p.astype(F32)[None, :]
    out = None
    for l in range(depth):
        pa, pb, pc, pd, pside = _inproj(xt, row(norm_mix[l]), _split_w_in(w_in[l]))
        pba = plr = pside
        lam_init = 0.8 - 0.6 * math.exp(-0.3 * l)
        mixes = (
            _gdn(pa, pba, a_conv[l], a_A_log[l], a_dt_bias[l], a_norm[l], batch, seq),
            _diff_attn(pb, b_lambda[l], lam_init, bias_tiles, b_norm[l], batch, seq),
            _rglru(pc, c_conv_w[l], c_conv_b[l], c_w_a[l], c_b_a[l], c_w_x[l], c_b_x[l],
                   c_a_param[l], batch, seq),
            _gla(pd, plr, d_w_lr[l], d_b_lr[l], d_norm[l], batch, seq),
        )
        wo = w_out[l].astype(BF16)
        if l % 2 == 0:
            j = l // 2
            xt = _outproj_ffn(xt, mixes, wo, row(norm_ffn[l]), ffn_w_gate[j].astype(BF16),
                              ffn_w_up[j].astype(BF16), ffn_w_down[j].astype(BF16))
            out = None
        else:
            j = l // 2
            final = l == depth - 1
            take = lambda a, i: a.at[i].get(mode='promise_in_bounds')
            x1, h, route = _outproj_router(xt, mixes, wo, row(norm_ffn[l]), moe_router[j])
            slot_tok, tile_e, n_active, dest, n_tiles = _route(route, t)
            xs = take(h, slot_tok)
            ys = _moe_experts(xs, tile_e, n_active, moe_w_gate[j], moe_w_up[j], moe_w_down[j], n_tiles)
            y = _moe_combine(x1, take(ys, dest[0]), take(ys, dest[1]), route, row(norm_final), final)
            if final:
                out = y
            else:
                xt = y
    if out is None:
        tm = min(FFN_TM, t)
        rowspec = pl.BlockSpec((tm, D_MODEL), lambda i: (i, 0))
        out = pl.pallas_call(
            _final_norm_kernel, name="final_norm", grid=(t // tm,),
            in_specs=[rowspec, pl.BlockSpec((1, D_MODEL), lambda i: (0, 0))],
            out_specs=rowspec, out_shape=jax.ShapeDtypeStruct((t, D_MODEL), F32),
            compiler_params=_cparams(("parallel",)),
        )(xt, row(norm_final))
    return out.reshape(batch, seq, D_MODEL).astype(x.dtype)
```

```python
import functools
import math

import jax
import jax.numpy as jnp
import numpy as np
from jax import lax
from jax.experimental import pallas as pl
from jax.experimental.pallas import tpu as pltpu

D_MODEL = 1024
CHUNK = 64
N_HEADS = 4
HEAD_DIM = 64
GROUP_WIDTH = 256
CONV_WIDTH = 4
DIFF_DH = 32
Q_BLOCK = 128
REL_BUCKETS = 32
REL_MAX_DIST = 128
RG_C = 8.0
GLA_DK = 32
GLA_RANK = 16
GLA_TAU = 16.0
D_FF = 2816
N_EXPERTS = 8
TOP_K = 2
D_FF_EXPERT = 3584
EPS = 1e-6
assert CHUNK == HEAD_DIM

LANE = 128
SUBLANE = 8
VMEM_LIMIT = 56 * 1024 * 1024

F32 = jnp.float32
BF16 = jnp.bfloat16
HI = lax.Precision.HIGHEST
NEG = -1e30
LOG2E = math.log2(math.e)

A_W = 1024
B_W = 768
C_W = 512
D_W = 768
SIDE_LR = 2 * N_HEADS

PROJ_TM = 1024
MIX_TS = 256
GDN_BLK = 16
RGLRU_NB = 4
GDN_NB = 4
ATT_T = 128
ATT_VROWS = 80
FFN_TM = 1024
ROUTER_TM = 1024
FFN_TF = 1408
MOE_TM = 1024
MOE_TF = 512


def _cparams(sem):
    return pltpu.CompilerParams(dimension_semantics=sem, vmem_limit_bytes=VMEM_LIMIT)


def _dot(a, b, precision=None):
    return jnp.dot(a, b, preferred_element_type=F32, precision=precision)


def _dot_nt(a, b, precision=None):
    return lax.dot_general(a, b, (((1,), (1,)), ((), ())), preferred_element_type=F32,
                           precision=precision)


def _dot_tn(a, b, precision=None):
    return lax.dot_general(a, b, (((0,), (0,)), ((), ())), preferred_element_type=F32,
                           precision=precision)


def _softplus(x):
    return jnp.maximum(x, 0.0) + jnp.log1p(jnp.exp(-jnp.abs(x)))


def _rms(x, g):
    return x * lax.rsqrt(jnp.mean(x * x, axis=-1, keepdims=True) + EPS) * g


def _causal_conv(x, tail, w):
    row = lax.broadcasted_iota(jnp.int32, (SUBLANE, x.shape[1]), 0)
    y = x * w[CONV_WIDTH - 1:CONV_WIDTH, :]
    for d in range(1, CONV_WIDTH):
        rolled = pltpu.roll(x, d, 0)
        first = jnp.where(row < d, pltpu.roll(tail, d, 0), rolled[:SUBLANE])
        shifted = jnp.concatenate([first, rolled[SUBLANE:]], axis=0)
        y = y + shifted * w[CONV_WIDTH - 1 - d:CONV_WIDTH - d, :]
    return y


def _head_mask(width, per_head):
    lane = lax.broadcasted_iota(jnp.int32, (1, width), 1)
    return [(lane // per_head) == h for h in range(N_HEADS)]


def _stack_heads(x, masks):
    return jnp.concatenate([jnp.where(m, x, 0.0) for m in masks], axis=0)


def _unstack_heads(r, masks, c):
    out = jnp.where(masks[0], r[0:c], 0.0)
    for h in range(1, N_HEADS):
        out = out + jnp.where(masks[h], r[h * c:(h + 1) * c], 0.0)
    return out


def _inproj_kernel(x_ref, g_ref, *refs):
    n = len(refs) // 2
    h = _rms(x_ref[...], g_ref[...]).astype(BF16)
    for w_ref, o_ref in zip(refs[:n], refs[n:]):
        o_ref[...] = _dot(h, w_ref[...]).astype(o_ref.dtype)


def _inproj(x, g, weights):
    t = x.shape[0]
    tm = min(PROJ_TM, t)
    row = lambda w: pl.BlockSpec((tm, w), lambda i: (i, 0))
    full = lambda a: pl.BlockSpec(a.shape, lambda i: (0, 0))
    ws = [w for w, _ in weights]
    return pl.pallas_call(
        _inproj_kernel,
        name="inproj",
        grid=(t // tm,),
        in_specs=[row(D_MODEL), full(g)] + [full(w) for w in ws],
        out_specs=[row(w.shape[1]) for w in ws],
        out_shape=[jax.ShapeDtypeStruct((t, w.shape[1]), dt) for w, dt in weights],
        compiler_params=_cparams(("parallel",)),
    )(x, g, *ws)


def _split2(x):
    hi = x.astype(BF16)
    return hi, (x - hi.astype(F32)).astype(BF16)


def _dot_x2(x, w):
    hi, lo = _split2(x)
    return _dot(hi, w) + _dot(lo, w)


def _bd_tile(x, bd):
    return jnp.concatenate([x] * N_HEADS, axis=0) * bd


def _dot_bd(a, bs, bd):
    ab = a.astype(BF16)
    return [_dot(ab, _bd_tile(b.astype(BF16), bd)) for b in bs]


def _gdn_kernel(a_ref, ba_ref, convw_ref, alog_ref, dtb_ref, gn_ref, bd_ref, lt_ref, e_ref, sh_ref,
                o_ref, tail_ref, state_ref, *, ts, nb):
    @pl.when(pl.program_id(1) == 0)
    def _init():
        tail_ref[...] = jnp.zeros_like(tail_ref)
        state_ref[...] = jnp.zeros_like(state_ref)

    c = CHUNK
    bd = bd_ref[...]
    lt = lt_ref[...]
    q, k, v, beta, gcum = [], [], [], [], []
    for b in range(nb):
        xin_b = a_ref[b, :, 0:768]
        xin = xin_b.astype(F32)
        w = convw_ref[...]
        tail = tail_ref[b]
        row8 = lax.broadcasted_iota(jnp.int32, (SUBLANE, 768), 0)
        y = xin * w[CONV_WIDTH - 1:CONV_WIDTH, :]
        for d in range(1, CONV_WIDTH):
            sh = _dot(sh_ref[d - 1], xin_b)
            head = sh[:SUBLANE] + jnp.where(row8 < d, pltpu.roll(tail, d, 0), 0.0)
            sh = jnp.concatenate([head, sh[SUBLANE:]], axis=0)
            y = y + sh * w[CONV_WIDTH - 1 - d:CONV_WIDTH - d, :]
        tail_ref[b] = xin[ts - SUBLANE:ts, :]
        y = y * jax.nn.sigmoid(y)
        qb, kb_ = y[:, 0:256], y[:, 256:512]
        q.append(qb * lax.rsqrt(_dot_x2(qb * qb, bd) + EPS) * (HEAD_DIM ** -0.5))
        k.append(kb_ * lax.rsqrt(_dot_x2(kb_ * kb_, bd) + EPS))
        v.append(y[:, 512:768])
        e = _dot_x2(ba_ref[b], e_ref[...])
        beta.append(jax.nn.sigmoid(e[:, 0:256]))
        g = -jnp.exp(alog_ref[...]) * _softplus(e[:, 256:512] + dtb_ref[...])
        g_hi, g_lo = _split2(g)
        g_lo2 = (g - g_hi.astype(F32) - g_lo.astype(F32)).astype(BF16)
        gcum.append(_dot(lt, g_hi) + (_dot(lt, g_lo) + _dot(lt, g_lo2)))

    ri = lax.broadcasted_iota(jnp.int32, (c, GROUP_WIDTH), 0)
    cj = lax.broadcasted_iota(jnp.int32, (c, GROUP_WIDTH), 1) % HEAD_DIM
    causal = ri >= cj
    strict = ri > cj
    diag = ri == cj
    same_blk = (ri // GDN_BLK) == (cj // GDN_BLK)
    eye = diag.astype(F32)
    masks = _head_mask(GROUP_WIDTH, HEAD_DIM)
    items = [(b, slice(ci * c, (ci + 1) * c)) for ci in range(ts // c) for b in range(nb)]
    idx = range(len(items))

    a_qk, p0, nn, rhs_u, rhs_w = [], [], [], [], []
    for b, sl in items:
        kc, gc = k[b][sl], gcum[b][sl]
        kb = kc * beta[b][sl]
        grow = jnp.sum(jnp.where(diag, gc, 0.0), axis=0, keepdims=True)
        gamma = jnp.exp(jnp.where(causal, gc - grow, NEG))
        kst = _stack_heads(kc, masks).astype(BF16)
        aa = _dot_nt(jnp.concatenate([kb, q[b][sl]], axis=0).astype(BF16), kst)
        a_kk = jnp.where(strict, aa[0:c] * gamma, 0.0)
        a_qk.append(aa[c:2 * c] * gamma)
        p0.append(jnp.where(same_blk, -a_kk, 0.0))
        nn.append(jnp.where(same_blk, 0.0, a_kk))
        rhs_u.append(v[b][sl] * beta[b][sl])
        rhs_w.append(kb * jnp.exp(gc))
    t1 = [eye + p for p in p0]
    p1 = [_dot_bd(p0[i], [p0[i]], bd)[0] for i in idx]
    pr = [_dot_bd(p1[i], [p1[i], t1[i]], bd) for i in idx]
    p2 = [x[0] for x in pr]
    t2 = [t1[i] + pr[i][1] for i in idx]
    pr = [_dot_bd(p2[i], [p2[i], t2[i]], bd) for i in idx]
    p3 = [x[0] for x in pr]
    t3 = [t2[i] + pr[i][1] for i in idx]
    dinv = [t3[i] + _dot_bd(p3[i], [t3[i]], bd)[0] for i in idx]
    m1 = [_dot_bd(dinv[i], [nn[i]], bd)[0] for i in idx]
    m2 = [_dot_bd(m1[i], [m1[i]], bd)[0] for i in idx]
    im = [eye - m for m in m1]
    qq = [im[i] + _dot_bd(im[i], [m2[i]], bd)[0] for i in idx]
    inv = [_dot_bd(qq[i], [dinv[i]], bd)[0] for i in idx]
    uw = [_dot_bd(inv[i], [rhs_u[i], rhs_w[i]], bd) for i in idx]

    bdf = bd.astype(F32)
    state = [state_ref[b] for b in range(nb)]
    outs = [[] for _ in range(nb)]
    for i, (b, sl) in enumerate(items):
        gc = gcum[b][sl]
        g_last = gc[c - 1:c, :]
        u, w = uw[i]
        ws_qs = _dot(jnp.concatenate([w, q[b][sl] * jnp.exp(gc)], axis=0).astype(BF16), state[b].astype(BF16))
        v_new = (u - ws_qs[0:c]).astype(BF16)
        outs[b].append(ws_qs[c:2 * c] + _dot(a_qk[i].astype(BF16), _bd_tile(v_new, bd)))
        kd = k[b][sl] * jnp.exp(g_last - gc)
        state[b] = state[b] * jnp.exp(g_last) + _dot_tn(kd.astype(BF16), v_new) * bdf
    for b in range(nb):
        state_ref[b] = state[b]
        o = jnp.concatenate(outs[b], axis=0)
        o = o * lax.rsqrt(_dot_x2(o * o, bd) * (1.0 / HEAD_DIM) + EPS) * gn_ref[...]
        gate = a_ref[b, :, 768:1024].astype(F32)
        o_ref[b] = (o * (gate * jax.nn.sigmoid(gate))).astype(o_ref.dtype)


def _block_diag_ones(n, blk):
    i = np.arange(n)
    return jnp.asarray((i[:, None] // blk) == (i[None, :] // blk), F32)


def _gdn(pa, pba, conv_w, a_log, dt_bias, norm_g, batch, seq):
    ts = min(MIX_TS, seq)
    nst = seq // ts
    bd = _block_diag_ones(GROUP_WIDTH, HEAD_DIM).astype(BF16)
    r = np.arange(ts)
    lt = jnp.asarray((r[:, None] // CHUNK == r[None, :] // CHUNK) & (r[:, None] >= r[None, :]), BF16)
    lane = np.arange(GROUP_WIDTH)
    e = np.zeros((LANE, 2 * GROUP_WIDTH), np.float32)
    for h in range(N_HEADS):
        e[h, np.nonzero(lane // HEAD_DIM == h)[0]] = 1.0
        e[N_HEADS + h, GROUP_WIDTH + np.nonzero(lane // HEAD_DIM == h)[0]] = 1.0
    rep = lambda p: jnp.repeat(p.astype(F32), HEAD_DIM)[None, :]
    shifts = jnp.asarray(np.stack([np.eye(ts, k=-d) for d in range(1, CONV_WIDTH)]), BF16)
    consts = [conv_w.astype(F32), rep(a_log), rep(dt_bias),
              jnp.tile(norm_g.astype(F32), N_HEADS)[None, :], bd, lt, jnp.asarray(e, BF16), shifts]
    nb = GDN_NB if batch % GDN_NB == 0 else 1
    full = lambda a: pl.BlockSpec(a.shape, lambda b, s: (0,) * a.ndim)
    out = pl.pallas_call(
        functools.partial(_gdn_kernel, ts=ts, nb=nb),
        name="gdn",
        grid=(batch // nb, nst),
        in_specs=[pl.BlockSpec((nb, ts, A_W), lambda b, s: (b, s, 0)),
                  pl.BlockSpec((nb, ts, LANE), lambda b, s: (b, s, 0))] + [full(a) for a in consts],
        out_specs=pl.BlockSpec((nb, ts, GROUP_WIDTH), lambda b, s: (b, s, 0)),
        out_shape=jax.ShapeDtypeStruct((batch, seq, GROUP_WIDTH), BF16),
        scratch_shapes=[pltpu.VMEM((nb, SUBLANE, 768), F32),
                        pltpu.VMEM((nb, GROUP_WIDTH, GROUP_WIDTH), F32)],
        compiler_params=_cparams(("parallel", "arbitrary")),
    )(pa.reshape(batch, seq, A_W), pba.reshape(batch, seq, LANE), *consts)
    return out.reshape(batch * seq, GROUP_WIDTH)


def _attn_kernel(qa_ref, qb_ref, k_ref, v_ref, bias_ref, lamv_ref, gn_ref, oa_ref, ob_ref,
                 va_ref, qt_ref, s_ref, p_ref, al_ref, mt_ref, m_ref, l_ref, acc_ref, *, lam_init, seq):
    t = ATT_T
    nq = seq // t
    i = pl.program_id(1)
    na = i + 1
    nlan = 2 * N_HEADS * t
    nsteps = nq + 1

    @pl.when(i == 0)
    def _stage_v():
        def body(j, carry):
            rows = pl.ds(pl.multiple_of(j * t, t), t)
            vt = v_ref[rows, :].astype(F32).T.astype(BF16)
            for h in range(N_HEADS):
                va_ref[j, h, 0:HEAD_DIM, :] = vt[h * HEAD_DIM:(h + 1) * HEAD_DIM, :]
                va_ref[j, h, HEAD_DIM:ATT_VROWS, :] = jnp.ones((ATT_VROWS - HEAD_DIM, t), BF16)
            return carry
        lax.fori_loop(0, seq // t, body, 0)

    feat = lax.broadcasted_iota(jnp.int32, (GROUP_WIDTH, t), 0) // DIFF_DH
    for w, q_ref in enumerate((qa_ref, qb_ref)):
        qt = (q_ref[...].astype(F32) * (DIFF_DH ** -0.5 * LOG2E)).T
        for idx in range(2 * N_HEADS):
            qt_ref[w, :, idx * t:(idx + 1) * t] = jnp.where(feat == idx, qt, 0.0).astype(BF16)
    m_ref[...] = jnp.full((2, 1, nlan), NEG, F32)
    l_ref[...] = jnp.zeros((2, 1, nlan), F32)
    acc_ref[...] = jnp.zeros((2, HEAD_DIM, nlan), F32)

    head_cols = [slice(2 * h * t, (2 * h + 2) * t) for h in range(N_HEADS)]

    def tile_of(s):
        if s >= nq // 2:
            bias = 2 if s == nq else (1 if s == nq - 1 else None)
            return 1, s - na, bias
        w = (s >= na).astype(jnp.int32)
        bias = jnp.where(w == 1, 0, jnp.where(s == na - 1, 2, jnp.where(s == na - 2, 1, 0)))
        return w, s - na * w, bias

    def scores(s, h):
        w, kt, bias = tile_of(s)
        rows = pl.ds(pl.multiple_of(kt * t, t), t)
        sc = _dot(k_ref[rows, :], qt_ref[w, :, head_cols[h]])
        if bias is not None:
            sc = sc + bias_ref[bias, :, head_cols[h]]
        s_ref[s % 2, :, head_cols[h]] = sc
        mt_ref[s % 2, :, head_cols[h]] = jnp.max(sc, axis=0, keepdims=True)

    def softmax(s, h):
        w, _, _ = tile_of(s)
        cols = head_cols[h]
        sc = s_ref[s % 2, :, cols]
        m_prev = m_ref[w, :, cols]
        m_new = jnp.maximum(m_prev, mt_ref[s % 2, :, cols])
        al_ref[s % 2, :, cols] = jnp.exp2(m_prev - m_new)
        m_ref[w, :, cols] = m_new
        p_ref[s % 2, :, cols] = jnp.exp2(sc - m_new).astype(BF16)

    def values(s, h):
        w, kt, _ = tile_of(s)
        cols = head_cols[h]
        alpha = al_ref[s % 2, :, cols]
        pv = _dot(va_ref[kt, h], p_ref[s % 2, :, cols])
        acc_ref[w, :, cols] = alpha * acc_ref[w, :, cols] + pv[0:HEAD_DIM]
        l_ref[w, :, cols] = alpha * l_ref[w, :, cols] + pv[HEAD_DIM:HEAD_DIM + 1]

    for s in range(nsteps + 2):
        for h in range(N_HEADS):
            if s < nsteps:
                scores(s, h)
            if 1 <= s <= nsteps:
                softmax(s - 1, h)
            if s >= 2:
                values(s - 2, h)

    lv = lamv_ref[...]
    lam = (jnp.exp(jnp.sum(lv[0:1] * lv[1:2], axis=1, keepdims=True))
           - jnp.exp(jnp.sum(lv[2:3] * lv[3:4], axis=1, keepdims=True)) + lam_init)
    for w, o_ref in enumerate((oa_ref, ob_ref)):
        inv_l = 1.0 / l_ref[w]
        outs = []
        for h in range(N_HEADS):
            c0 = slice(2 * h * t, (2 * h + 1) * t)
            c1 = slice((2 * h + 1) * t, (2 * h + 2) * t)
            oh = acc_ref[w, :, c0] * inv_l[:, c0] - lam * (acc_ref[w, :, c1] * inv_l[:, c1])
            oh = oh * lax.rsqrt(jnp.mean(oh * oh, axis=0, keepdims=True) + EPS)
            outs.append(oh)
        o = jnp.concatenate(outs, axis=0).T
        o_ref[...] = (o * gn_ref[...] * (1.0 - lam_init)).astype(o_ref.dtype)


def _t5_bucket(rel):
    nb = REL_BUCKETS // 2
    bucket = jnp.where(rel > 0, nb, 0)
    n = jnp.abs(rel)
    max_exact = nb // 2
    large = max_exact + (jnp.log(jnp.maximum(n, 1).astype(F32) / max_exact)
                         / math.log(REL_MAX_DIST / max_exact) * (nb - max_exact)).astype(jnp.int32)
    large = jnp.minimum(large, nb - 1)
    return bucket + jnp.where(n < max_exact, n, large)


def _attn_bias_tiles(rel_bias):
    t = ATT_T
    table = rel_bias.astype(F32)
    kk = jnp.arange(t)[:, None]
    qq = jnp.arange(t)[None, :]

    def expand(b):
        b = jnp.transpose(b, (0, 2, 1))
        b = jnp.broadcast_to(b[:, :, None, :], (t, N_HEADS, 2, t))
        return b.reshape(t, 2 * N_HEADS * t)

    table = table * LOG2E

    def lookup(bucket):
        onehot = (bucket[..., None] == jnp.arange(REL_BUCKETS)).astype(F32)
        return jnp.einsum('kqb,bh->kqh', onehot, table, precision=HI)

    diag = lookup(_t5_bucket(kk - qq))
    diag = jnp.where(((kk // CHUNK) <= (qq // CHUNK))[:, :, None], diag, NEG)
    near = lookup(_t5_bucket(kk - qq - t))
    far = lookup(_t5_bucket(jnp.full((1, 1), -(REL_MAX_DIST + 1), jnp.int32)))
    far = jnp.broadcast_to(far, (t, t, N_HEADS))
    return jnp.stack([jnp.zeros((t, 2 * N_HEADS * t), F32), expand(near - far), expand(diag - far)])


def _diff_attn(pb, lam_vecs, lam_init, bias_tiles, norm_g, batch, seq):
    t = ATT_T
    nq = seq // t
    gn = jnp.tile(norm_g.astype(F32), N_HEADS)[None, :]
    lamv = lam_vecs.astype(F32)
    nlan = 2 * N_HEADS * t
    assert nq % 2 == 0
    nh = nq // 2
    full = lambda a: pl.BlockSpec(a.shape, lambda b, i: (0,) * a.ndim)
    half = jax.ShapeDtypeStruct((batch * nh * t, GROUP_WIDTH), BF16)
    lo, hi = pl.pallas_call(
        functools.partial(_attn_kernel, lam_init=lam_init, seq=seq),
        name="diffattn",
        grid=(batch, nh),
        in_specs=[pl.BlockSpec((t, GROUP_WIDTH), lambda b, i: (b * nq + i, 0)),
                  pl.BlockSpec((t, GROUP_WIDTH), lambda b, i: (b * nq + nq - 1 - i, 0)),
                  pl.BlockSpec((seq, GROUP_WIDTH), lambda b, i: (b, 1)),
                  pl.BlockSpec((seq, GROUP_WIDTH), lambda b, i: (b, 2)),
                  full(bias_tiles), full(lamv), full(gn)],
        out_specs=[pl.BlockSpec((t, GROUP_WIDTH), lambda b, i: (b * nh + i, 0)),
                   pl.BlockSpec((t, GROUP_WIDTH), lambda b, i: (b * nh + nh - 1 - i, 0))],
        out_shape=[half, half],
        scratch_shapes=[pltpu.VMEM((seq // t, N_HEADS, ATT_VROWS, t), BF16),
                        pltpu.VMEM((2, GROUP_WIDTH, nlan), BF16),
                        pltpu.VMEM((2, t, nlan), F32),
                        pltpu.VMEM((2, t, nlan), BF16),
                        pltpu.VMEM((2, 1, nlan), F32),
                        pltpu.VMEM((2, 1, nlan), F32),
                        pltpu.VMEM((2, 1, nlan), F32), pltpu.VMEM((2, 1, nlan), F32),
                        pltpu.VMEM((2, HEAD_DIM, nlan), F32)],
        compiler_params=_cparams(("parallel", "arbitrary")),
    )(pb, pb, pb, pb, bias_tiles, lamv, gn)
    out = jnp.concatenate([lo.reshape(batch, nh * t, GROUP_WIDTH), hi.reshape(batch, nh * t, GROUP_WIDTH)],
                          axis=1)
    return out.reshape(batch * seq, GROUP_WIDTH)


def _rglru_kernel(c_ref, convw_ref, convb_ref, wa_ref, ba_ref, wx_ref, bx_ref, ap_ref, o_ref,
                  tail_ref, h_ref, *, ts, nb):
    @pl.when(pl.program_id(1) == 0)
    def _init():
        tail_ref[...] = jnp.zeros_like(tail_ref)
        h_ref[...] = jnp.zeros_like(h_ref)

    row = lax.broadcasted_iota(jnp.int32, (ts, GROUP_WIDTH), 0)
    for b in range(nb):
        xb = c_ref[b, :, 0:256].astype(F32)
        gb = c_ref[b, :, 256:512].astype(F32)
        xc = _causal_conv(xb, tail_ref[b], convw_ref[...]) + convb_ref[...]
        tail_ref[b] = xb[ts - SUBLANE:ts, :]
        xcb = xc.astype(BF16)
        gate_a = jax.nn.sigmoid(_dot(xcb, wa_ref[...]) + ba_ref[...])
        gate_x = jax.nn.sigmoid(_dot(xcb, wx_ref[...]) + bx_ref[...])
        log_a = -RG_C * gate_a * _softplus(ap_ref[...])
        a = jnp.exp(log_a)
        th = jnp.tanh(log_a)
        u = xc * gate_x * jnp.sqrt(-2.0 * th / (1.0 - th))
        d = 1
        while d < ts:
            keep = row >= d
            a_sh = jnp.where(keep, pltpu.roll(a, d, 0), 1.0)
            u_sh = jnp.where(keep, pltpu.roll(u, d, 0), 0.0)
            u = u + a * u_sh
            a = a * a_sh
            d *= 2
        h = u + a * h_ref[b]
        h_ref[b] = h[ts - 1:ts, :]
        gelu = 0.5 * gb * (1.0 + jnp.tanh(math.sqrt(2.0 / math.pi) * (gb + 0.044715 * (gb * gb * gb))))
        o_ref[b] = (h * gelu).astype(o_ref.dtype)


def _block_diag_weight(w):
    nb, wi, wo = w.shape
    out = jnp.zeros((nb * wi, nb * wo), w.dtype)
    for i in range(nb):
        out = out.at[i * wi:(i + 1) * wi, i * wo:(i + 1) * wo].set(w[i])
    return out


def _rglru(pc, conv_w, conv_b, w_a, b_a, w_x, b_x, a_param, batch, seq):
    ts = min(MIX_TS, seq)
    nst = seq // ts
    r = lambda p: p.astype(F32)[None, :]
    consts = [conv_w.astype(F32), r(conv_b), _block_diag_weight(w_a).astype(BF16), r(b_a),
              _block_diag_weight(w_x).astype(BF16), r(b_x), r(a_param)]
    nb = RGLRU_NB if batch % RGLRU_NB == 0 else 1
    full = lambda a: pl.BlockSpec(a.shape, lambda b, s: (0, 0))
    out = pl.pallas_call(
        functools.partial(_rglru_kernel, ts=ts, nb=nb),
        name="rglru",
        grid=(batch // nb, nst),
        in_specs=[pl.BlockSpec((nb, ts, C_W), lambda b, s: (b, s, 0))] + [full(a) for a in consts],
        out_specs=pl.BlockSpec((nb, ts, GROUP_WIDTH), lambda b, s: (b, s, 0)),
        out_shape=jax.ShapeDtypeStruct((batch, seq, GROUP_WIDTH), BF16),
        scratch_shapes=[pltpu.VMEM((nb, SUBLANE, GROUP_WIDTH), F32), pltpu.VMEM((nb, 1, GROUP_WIDTH), F32)],
        compiler_params=_cparams(("parallel", "arbitrary")),
    )(pc.reshape(batch, seq, C_W), *consts)
    return out.reshape(batch * seq, GROUP_WIDTH)


def _gla_kernel(d_ref, lr_ref, wlr_ref, blr_ref, gn_ref, bd_ref, bdt_ref, lt_ref, o_ref, state_ref,
                *, ts, nb):
    @pl.when(pl.program_id(1) == 0)
    def _init():
        state_ref[...] = jnp.zeros_like(state_ref)

    c = CHUNK
    bd = bd_ref[...]
    bdt = bdt_ref[...]
    lt = lt_ref[...]
    kmasks = _head_mask(N_HEADS * GLA_DK, GLA_DK)
    ri = lax.broadcasted_iota(jnp.int32, (c, GROUP_WIDTH), 0)
    cj = lax.broadcasted_iota(jnp.int32, (c, GROUP_WIDTH), 1) % HEAD_DIM
    causal = ri >= cj
    q, k, gcum = [], [], []
    for b in range(nb):
        q.append(d_ref[b, :, 0:128].astype(F32) * (GLA_DK ** -0.5))
        k.append(d_ref[b, :, 128:256].astype(F32))
        lr_hi, lr_lo = _split2(lr_ref[b])
        w_hi, w_lo = _split2(wlr_ref[...])
        z = _dot(lr_hi, w_hi) + (_dot(lr_hi, w_lo) + _dot(lr_lo, w_hi)) + blr_ref[...]
        la = (jnp.minimum(z, 0.0) - jnp.log1p(jnp.exp(-jnp.abs(z)))) * (1.0 / GLA_TAU)
        la_hi, la_lo = _split2(la)
        la_lo2 = (la - la_hi.astype(F32) - la_lo.astype(F32)).astype(BF16)
        gcum.append(_dot(lt, la_hi) + (_dot(lt, la_lo) + _dot(lt, la_lo2)))

    items = [(b, slice(ci * c, (ci + 1) * c)) for ci in range(ts // c) for b in range(nb)]
    o_intra, upd = [], []
    for b, sl in items:
        qc, kc, gc = q[b][sl], k[b][sl], gcum[b][sl]
        vc = d_ref[b, sl, 256:512]
        ref = gc[c // 2:c // 2 + 1, :]
        kst = _stack_heads(kc * jnp.exp(ref - gc), kmasks).astype(BF16)
        a_in = _dot_nt((qc * jnp.exp(gc - ref)).astype(BF16), kst)
        a_in = jnp.where(causal, a_in, 0.0).astype(BF16)
        o_intra.append(_dot(a_in, _bd_tile(vc, bd)))
        kd = kc * jnp.exp(gc[c - 1:c, :] - gc)
        upd.append(_dot_tn(vc, kd.astype(BF16)) * bdt)

    state = [state_ref[b] for b in range(nb)]
    outs = [[] for _ in range(nb)]
    for i, (b, sl) in enumerate(items):
        gc = gcum[b][sl]
        o_inter = _dot_nt((q[b][sl] * jnp.exp(gc)).astype(BF16), state[b].astype(BF16))
        outs[b].append(o_intra[i] + o_inter)
        state[b] = state[b] * jnp.exp(gc[c - 1:c, :]) + upd[i]
    for b in range(nb):
        state_ref[b] = state[b]
        o = jnp.concatenate(outs[b], axis=0)
        o = o * lax.rsqrt(_dot_x2(o * o, bd) * (1.0 / HEAD_DIM) + EPS) * gn_ref[...]
        rt = d_ref[b, :, 512:768].astype(F32)
        o_ref[b] = (o * (rt * jax.nn.sigmoid(rt))).astype(o_ref.dtype)


def _gla(pd, plr, w_lr, b_lr, norm_g, batch, seq):
    ts = min(MIX_TS, seq)
    nst = seq // ts
    kw = N_HEADS * GLA_DK
    wlr = jnp.zeros((LANE, kw), F32).at[SIDE_LR:SIDE_LR + GLA_RANK, :].set(w_lr.astype(F32))
    bd = _block_diag_ones(GROUP_WIDTH, HEAD_DIM).astype(BF16)
    iv = np.arange(GROUP_WIDTH)[:, None] // HEAD_DIM
    ik = np.arange(kw)[None, :] // GLA_DK
    bdt = jnp.asarray(iv == ik, F32)
    r = np.arange(ts)
    lt = jnp.asarray((r[:, None] // CHUNK == r[None, :] // CHUNK) & (r[:, None] >= r[None, :]), BF16)
    consts = [wlr, b_lr.astype(F32)[None, :], jnp.tile(norm_g.astype(F32), N_HEADS)[None, :], bd, bdt, lt]
    nb = GDN_NB if batch % GDN_NB == 0 else 1
    full = lambda a: pl.BlockSpec(a.shape, lambda b, s: (0, 0))
    out = pl.pallas_call(
        functools.partial(_gla_kernel, ts=ts, nb=nb),
        name="gla",
        grid=(batch // nb, nst),
        in_specs=[pl.BlockSpec((nb, ts, D_W), lambda b, s: (b, s, 0)),
                  pl.BlockSpec((nb, ts, LANE), lambda b, s: (b, s, 0))] + [full(a) for a in consts],
        out_specs=pl.BlockSpec((nb, ts, GROUP_WIDTH), lambda b, s: (b, s, 0)),
        out_shape=jax.ShapeDtypeStruct((batch, seq, GROUP_WIDTH), BF16),
        scratch_shapes=[pltpu.VMEM((nb, GROUP_WIDTH, kw), F32)],
        compiler_params=_cparams(("parallel", "arbitrary")),
    )(pd.reshape(batch, seq, D_W), plr.reshape(batch, seq, LANE), *consts)
    return out.reshape(batch * seq, GROUP_WIDTH)


def _mix_outproj(x_ref, ma_ref, mb_ref, mc_ref, md_ref, wo_ref):
    mix = jnp.concatenate([ma_ref[...], mb_ref[...], mc_ref[...], md_ref[...]], axis=1)
    return x_ref[...] + _dot(mix, wo_ref[...])


def _ffn_kernel(x_ref, ma_ref, mb_ref, mc_ref, md_ref, wo_ref, g_ref, wg_ref, wu_ref, wd_ref,
                o_ref, h_ref):
    @pl.when(pl.program_id(1) == 0)
    def _first():
        x1 = _mix_outproj(x_ref, ma_ref, mb_ref, mc_ref, md_ref, wo_ref)
        o_ref[...] = x1
        h_ref[...] = _rms(x1, g_ref[...]).astype(BF16)

    h = h_ref[...]
    gt = _dot(h, wg_ref[...])
    act = (gt * jax.nn.sigmoid(gt) * _dot(h, wu_ref[...])).astype(BF16)
    o_ref[...] += _dot(act, wd_ref[...])


def _outproj_ffn(x, mixes, w_out, g, w_gate, w_up, w_down):
    t = x.shape[0]
    tm = min(FFN_TM, t)
    tf = FFN_TF
    nf = D_FF // tf
    row = lambda w: pl.BlockSpec((tm, w), lambda i, f: (i, 0))
    return pl.pallas_call(
        _ffn_kernel,
        name="outproj_ffn",
        grid=(t // tm, nf),
        in_specs=[row(D_MODEL)] + [row(GROUP_WIDTH)] * 4 + [
            pl.BlockSpec((D_MODEL, D_MODEL), lambda i, f: (0, 0)),
            pl.BlockSpec((1, D_MODEL), lambda i, f: (0, 0)),
            pl.BlockSpec((D_MODEL, tf), lambda i, f: (0, f)),
            pl.BlockSpec((D_MODEL, tf), lambda i, f: (0, f)),
            pl.BlockSpec((tf, D_MODEL), lambda i, f: (f, 0))],
        out_specs=row(D_MODEL),
        out_shape=jax.ShapeDtypeStruct((t, D_MODEL), F32),
        scratch_shapes=[pltpu.VMEM((tm, D_MODEL), BF16)],
        compiler_params=_cparams(("parallel", "arbitrary")),
    )(x, *mixes, w_out, g, w_gate, w_up, w_down)


def _router_kernel(x_ref, ma_ref, mb_ref, mc_ref, md_ref, wo_ref, g_ref, wr_ref, x1_ref, h_ref, r_ref):
    x1 = _mix_outproj(x_ref, ma_ref, mb_ref, mc_ref, md_ref, wo_ref)
    x1_ref[...] = x1
    h = _rms(x1, g_ref[...])
    h_hi, h_lo = _split2(h)
    h_ref[...] = h_hi
    lane = lax.broadcasted_iota(jnp.int32, (x1.shape[0], LANE), 1)
    hh = _dot(h_hi, wr_ref[...])
    logits = hh[:, 0:LANE] + (hh[:, LANE:2 * LANE] + _dot(h_lo, wr_ref[:, 0:LANE]))
    logits = jnp.where(lane < N_EXPERTS, logits, NEG)
    m1 = jnp.max(logits, axis=1, keepdims=True)
    e1 = jnp.min(jnp.where(logits == m1, lane, LANE), axis=1, keepdims=True)
    rest = jnp.where(lane == e1, NEG, logits)
    m2 = jnp.max(rest, axis=1, keepdims=True)
    e2 = jnp.min(jnp.where(rest == m2, lane, LANE), axis=1, keepdims=True)
    ex = jnp.exp(m2 - m1)
    w1 = 1.0 / (1.0 + ex)
    w2 = ex / (1.0 + ex)
    r_ref[...] = jnp.where(lane == 0, e1.astype(F32),
                           jnp.where(lane == 1, e2.astype(F32),
                                     jnp.where(lane == 2, w1, jnp.where(lane == 3, w2, 0.0))))


def _outproj_router(x, mixes, w_out, g, w_router):
    t = x.shape[0]
    tm = min(ROUTER_TM, t)
    wr = jnp.pad(w_router.astype(F32), ((0, 0), (0, LANE - N_EXPERTS)))
    wr_hi = wr.astype(BF16)
    wr_cat = jnp.concatenate([wr_hi, (wr - wr_hi.astype(F32)).astype(BF16)], axis=1)
    row = lambda w: pl.BlockSpec((tm, w), lambda i: (i, 0))
    full = lambda a: pl.BlockSpec(a.shape, lambda i: (0, 0))
    return pl.pallas_call(
        _router_kernel,
        name="outproj_router",
        grid=(t // tm,),
        in_specs=[row(D_MODEL)] + [row(GROUP_WIDTH)] * 4 + [full(w_out), full(g), full(wr_cat)],
        out_specs=[row(D_MODEL), row(D_MODEL), row(LANE)],
        out_shape=[jax.ShapeDtypeStruct((t, D_MODEL), F32), jax.ShapeDtypeStruct((t, D_MODEL), BF16),
                   jax.ShapeDtypeStruct((t, LANE), F32)],
        compiler_params=_cparams(("parallel",)),
    )(x, *mixes, w_out, g, wr_cat)


def _moe_kernel(te_ref, na_ref, x_ref, wg_ref, wu_ref, wd_ref, o_ref, acc_ref):
    i = pl.program_id(0)
    f = pl.program_id(1)

    def ff_step(first):
        x = x_ref[...]
        gt = _dot(x, wg_ref[...].astype(BF16))
        act = (gt * jax.nn.sigmoid(gt) * _dot(x, wu_ref[...].astype(BF16))).astype(BF16)
        part = _dot(act, wd_ref[...].astype(BF16))
        if first:
            acc_ref[...] = part
        else:
            acc_ref[...] += part

    @pl.when(i < na_ref[0])
    def _active():
        @pl.when(f == 0)
        def _first():
            ff_step(True)

        @pl.when(f > 0)
        def _rest():
            ff_step(False)

        @pl.when(f == pl.num_programs(1) - 1)
        def _last():
            o_ref[...] = acc_ref[...].astype(o_ref.dtype)

    @pl.when(jnp.logical_and(i >= na_ref[0], f == pl.num_programs(1) - 1))
    def _unused_tile():
        o_ref[...] = jnp.zeros_like(o_ref)


def _moe_experts(xs, tile_e, n_active, w_gate, w_up, w_down, n_tiles):
    tm, tf = MOE_TM, MOE_TF
    nf = D_FF_EXPERT // tf

    def tile(i, na):
        return jnp.minimum(i, na[0] - 1)

    def fidx(i, f, na):
        return jnp.where(i < na[0], f, nf - 1)

    grid_spec = pltpu.PrefetchScalarGridSpec(
        num_scalar_prefetch=2,
        grid=(n_tiles, nf),
        in_specs=[
            pl.BlockSpec((tm, D_MODEL), lambda i, f, te, na: (tile(i, na), 0)),
            pl.BlockSpec((None, D_MODEL, tf), lambda i, f, te, na: (te[tile(i, na)], 0, fidx(i, f, na))),
            pl.BlockSpec((None, D_MODEL, tf), lambda i, f, te, na: (te[tile(i, na)], 0, fidx(i, f, na))),
            pl.BlockSpec((None, tf, D_MODEL), lambda i, f, te, na: (te[tile(i, na)], fidx(i, f, na), 0)),
        ],
        out_specs=pl.BlockSpec((tm, D_MODEL), lambda i, f, te, na: (i, 0)),
        scratch_shapes=[pltpu.VMEM((tm, D_MODEL), F32)],
    )
    return pl.pallas_call(
        _moe_kernel,
        name="moe_experts",
        grid_spec=grid_spec,
        out_shape=jax.ShapeDtypeStruct((n_tiles * tm, D_MODEL), BF16),
        compiler_params=_cparams(("arbitrary", "arbitrary")),
    )(tile_e, n_active, xs, w_gate, w_up, w_down)


def _route(route, t):
    tm = MOE_TM
    n_assign = t * TOP_K
    flat_e = jnp.concatenate([route[:, k].astype(jnp.int32) for k in range(TOP_K)])
    onehot = (flat_e[:, None] == jnp.arange(N_EXPERTS)[None, :]).astype(jnp.int32)
    counts = jnp.sum(onehot, axis=0)
    padded = (counts + tm - 1) // tm * tm
    pad_end = jnp.cumsum(padded)
    pad_start = pad_end - padded
    grp_start = jnp.cumsum(counts) - counts
    dest = jnp.sum(onehot * (jnp.cumsum(onehot, axis=0) - 1 + pad_start[None, :]), axis=1)
    n_tiles = -(-n_assign // tm) + N_EXPERTS
    tile_e = jnp.minimum(jnp.searchsorted(pad_end, jnp.arange(n_tiles) * tm, side='right'),
                         N_EXPERTS - 1).astype(jnp.int32)
    n_active = (pad_end[-1] // tm).astype(jnp.int32).reshape(1)
    order = jnp.argsort(flat_e, stable=True).astype(jnp.int32)
    rank_s = ((jnp.arange(n_tiles, dtype=jnp.int32) * tm - pad_start[tile_e])[:, None]
              + jnp.arange(tm, dtype=jnp.int32)[None, :])
    src = jnp.minimum(grp_start[tile_e][:, None] + rank_s, n_assign - 1)
    slot = jnp.arange(n_tiles * tm, dtype=jnp.int32).reshape(n_tiles, tm)
    slot_tok = jnp.where(rank_s < counts[tile_e][:, None], order[src] % t, slot % t).reshape(n_tiles * tm)
    return slot_tok, tile_e, n_active, [dest[k * t:(k + 1) * t] for k in range(TOP_K)], n_tiles


def _combine_kernel(x_ref, y0_ref, y1_ref, r_ref, g_ref, o_ref, *, final):
    r = r_ref[...]
    y = x_ref[...] + (y0_ref[...].astype(F32) * r[:, TOP_K:TOP_K + 1]
                      + y1_ref[...].astype(F32) * r[:, TOP_K + 1:TOP_K + 2])
    o_ref[...] = _rms(y, g_ref[...]) if final else y


def _moe_combine(x1, y0, y1, route, g, final):
    t = x1.shape[0]
    tm = min(ROUTER_TM, t)
    row = pl.BlockSpec((tm, D_MODEL), lambda i: (i, 0))
    return pl.pallas_call(
        functools.partial(_combine_kernel, final=final),
        name="moe_combine",
        grid=(t // tm,),
        in_specs=[row, row, row, pl.BlockSpec((tm, LANE), lambda i: (i, 0)),
                  pl.BlockSpec((1, D_MODEL), lambda i: (0, 0))],
        out_specs=row,
        out_shape=jax.ShapeDtypeStruct((t, D_MODEL), F32),
        compiler_params=_cparams(("parallel",)),
    )(x1, y0, y1, route, g)


def _final_norm_kernel(x_ref, g_ref, o_ref):
    o_ref[...] = _rms(x_ref[...], g_ref[...])


def _split_w_in(w):
    def pad(a, width):
        return jnp.pad(a, ((0, 0), (0, width - a.shape[1])))
    a_end = A_W + 2 * N_HEADS
    b_end = a_end + B_W
    c_end = b_end + C_W
    d_end = c_end + D_W
    side = pad(jnp.concatenate([w[:, A_W:a_end], w[:, d_end:]], axis=1), LANE)
    parts = [(w[:, 0:A_W], BF16), (w[:, a_end:b_end], BF16), (w[:, b_end:c_end], BF16),
             (w[:, c_end:d_end], BF16), (side, F32)]
    return [(m.astype(BF16), dt) for m, dt in parts]


def kernel(x, norm_mix, w_in, a_conv, a_A_log, a_dt_bias, a_norm, b_lambda, b_norm, rel_bias,
           c_conv_w, c_conv_b, c_w_a, c_b_a, c_w_x, c_b_x, c_a_param, d_w_lr, d_b_lr, d_norm,
           w_out, norm_ffn, ffn_w_gate, ffn_w_up, ffn_w_down, moe_router, moe_w_gate, moe_w_up,
           moe_w_down, norm_final):
    batch, seq, _ = x.shape
    depth = w_in.shape[0]
    t = batch * seq
    xt = x.reshape(t, D_MODEL).astype(F32)
    bias_tiles = _attn_bias_tiles(rel_bias)
    row = lambda p: p.astype(F32)[None, :]
    out = None
    for l in range(depth):
        pa, pb, pc, pd, pside = _inproj(xt, row(norm_mix[l]), _split_w_in(w_in[l]))
        pba = plr = pside
        lam_init = 0.8 - 0.6 * math.exp(-0.3 * l)
        mixes = (
            _gdn(pa, pba, a_conv[l], a_A_log[l], a_dt_bias[l], a_norm[l], batch, seq),
            _diff_attn(pb, b_lambda[l], lam_init, bias_tiles, b_norm[l], batch, seq),
            _rglru(pc, c_conv_w[l], c_conv_b[l], c_w_a[l], c_b_a[l], c_w_x[l], c_b_x[l],
                   c_a_param[l], batch, seq),
            _gla(pd, plr, d_w_lr[l], d_b_lr[l], d_norm[l], batch, seq),
        )
        wo = w_out[l].astype(BF16)
        if l % 2 == 0:
            j = l // 2
            xt = _outproj_ffn(xt, mixes, wo, row(norm_ffn[l]), ffn_w_gate[j].astype(BF16),
                              ffn_w_up[j].astype(BF16), ffn_w_down[j].astype(BF16))
            out = None
        else:
            j = l // 2
            final = l == depth - 1
            take = lambda a, i: a.at[i].get(mode='promise_in_bounds')
            x1, h, route = _outproj_router(xt, mixes, wo, row(norm_ffn[l]), moe_router[j])
            slot_tok, tile_e, n_active, dest, n_tiles = _route(route, t)
            xs = take(h, slot_tok)
            ys = _moe_experts(xs, tile_e, n_active, moe_w_gate[j], moe_w_up[j], moe_w_down[j], n_tiles)
            y = _moe_combine(x1, take(ys, dest[0]), take(ys, dest[1]), route, row(norm_final), final)
            if final:
                out = y
            else:
                xt = y
    if out is None:
        tm = min(FFN_TM, t)
        rowspec = pl.BlockSpec((tm, D_MODEL), lambda i: (i, 0))
        out = pl.pallas_call(
            _final_norm_kernel, name="final_norm", grid=(t // tm,),
            in_specs=[rowspec, pl.BlockSpec((1, D_MODEL), lambda i: (0, 0))],
            out_specs=rowspec, out_shape=jax.ShapeDtypeStruct((t, D_MODEL), F32),
            compiler_params=_cparams(("parallel",)),
        )(xt, row(norm_final))
    return out.reshape(batch, seq, D_MODEL).astype(x.dtype)
```
